```python
import math
import jax, jax.numpy as jnp
from jax import lax
import numpy as np

D_MODEL = 2048
BATCH = 4
SEQ = 2048
DEPTH = 1

MOBA_HEADS = 8
HEAD_DIM = 128
MOBA_BLOCK = 256
MOBA_TOPK = 3
MLA_HEADS = 8
MLA_Q_LORA = 512
MLA_KV_LORA = 256
MLA_NOPE = 128
MLA_ROPE = 64
MLA_V = 128
ROPE_THETA = 10000.0
MEM_TOKENS = 256
MEM_HEADS = 4
MEM_HEAD_DIM = 128
N_BUCKETS = 32
MAX_DISTANCE = 128
Q_BLOCK = 128
EPS = 1e-6

MOBA_W = MOBA_HEADS * HEAD_DIM
MLA_W = MLA_HEADS * MLA_V
MEM_W = MEM_HEADS * MEM_HEAD_DIM
IN_SPLITS = (MOBA_W, MOBA_W, MOBA_W, MOBA_W,
             MLA_Q_LORA, MLA_KV_LORA, MLA_ROPE, MLA_W,
             MEM_W, MEM_W,
             D_MODEL, D_MODEL, D_MODEL)
IN_WIDTH = sum(IN_SPLITS)

kernel_name = "hybrid_moba_mla_memory_gated_block"


def rms_norm(x, g):
    xf = x.astype(jnp.float32)
    y = xf * lax.rsqrt(jnp.mean(xf * xf, axis=-1, keepdims=True) + EPS)
    return (y * g.astype(jnp.float32)).astype(x.dtype)


def rope(x, pos):
    half = x.shape[-1] // 2
    inv = ROPE_THETA ** (-jnp.arange(half, dtype=jnp.float32) / half)
    ang = pos.astype(jnp.float32)[:, None] * inv[None, :]
    cos = jnp.cos(ang)[None, :, None, :]
    sin = jnp.sin(ang)[None, :, None, :]
    x1, x2 = x[..., :half], x[..., half:]
    return jnp.concatenate([x1 * cos - x2 * sin, x2 * cos + x1 * sin], axis=-1).astype(x.dtype)


def t5_bucket(dist):
    n = jnp.maximum(dist, 0)
    max_exact = N_BUCKETS // 2
    nf = jnp.maximum(n, 1).astype(jnp.float32)
    large = max_exact + (jnp.log(nf / max_exact) / math.log(MAX_DISTANCE / max_exact)
                         * (N_BUCKETS - max_exact)).astype(jnp.int32)
    large = jnp.minimum(large, N_BUCKETS - 1)
    return jnp.where(n < max_exact, n, large)


def moba_attention(q, k, v, rel_bias):
    B, S, H, dh = q.shape
    nb = -(-S // MOBA_BLOCK)
    s_pad = nb * MOBA_BLOCK
    pad = ((0, 0), (0, s_pad - S), (0, 0), (0, 0))
    kb = jnp.pad(k, pad).reshape(B, nb, MOBA_BLOCK, H, dh).transpose(0, 3, 1, 2, 4)
    vb = jnp.pad(v, pad).reshape(B, nb, MOBA_BLOCK, H, dh).transpose(0, 3, 1, 2, 4)
    k_mean = jnp.mean(kb.astype(jnp.float32), axis=3)
    qh = q.transpose(0, 2, 1, 3)
    q_blk = jnp.arange(S, dtype=jnp.int32) // MOBA_BLOCK
    gate = jnp.einsum('bhsd,bhnd->bhsn', qh.astype(jnp.float32), k_mean)
    past = jnp.arange(nb, dtype=jnp.int32)[None, :] < q_blk[:, None]
    gate = jnp.where(past[None, None], gate, -jnp.inf)
    ksel = min(MOBA_TOPK, nb)
    _, sel = lax.top_k(gate, ksel)
    sel = sel.astype(jnp.int32)
    sel_valid = sel < q_blk[None, None, :, None]

    nc = S // Q_BLOCK
    n_items = B * H * nc
    q_items = qh.reshape(n_items, Q_BLOCK, dh)
    sel_items = sel.reshape(n_items, Q_BLOCK, ksel)
    valid_items = sel_valid.reshape(n_items, Q_BLOCK, ksel)
    ids = jnp.arange(n_items, dtype=jnp.int32)
    bh_ids = ids // nc
    c_ids = ids % nc
    h_ids = bh_ids % H
    kb_flat = kb.reshape(B * H, nb, MOBA_BLOCK, dh)
    vb_flat = vb.reshape(B * H, nb, MOBA_BLOCK, dh)
    scale = HEAD_DIM ** -0.5
    blk_off = jnp.arange(MOBA_BLOCK, dtype=jnp.int32)

    def one(args):
        qc, selc, validc, bh, c, hh = args
        kbh = kb_flat[bh]
        vbh = vb_flat[bh]
        qpos = c * Q_BLOCK + jnp.arange(Q_BLOCK, dtype=jnp.int32)
        own = (c * Q_BLOCK) // MOBA_BLOCK
        k_own, v_own = kbh[own], vbh[own]
        k_sel, v_sel = kbh[selc], vbh[selc]
        l_sel = jnp.einsum('qd,qjkd->qjk', qc, k_sel).reshape(Q_BLOCK, ksel * MOBA_BLOCK)
        l_own = jnp.einsum('qd,kd->qk', qc, k_own)
        kpos_sel = (selc[:, :, None] * MOBA_BLOCK + blk_off).reshape(Q_BLOCK, ksel * MOBA_BLOCK)
        kpos_own = own * MOBA_BLOCK + blk_off
        kpos = jnp.concatenate([kpos_sel, jnp.broadcast_to(kpos_own, (Q_BLOCK, MOBA_BLOCK))], axis=-1)
        bias = rel_bias[t5_bucket(qpos[:, None] - kpos), hh].astype(jnp.float32)
        logits = jnp.concatenate([l_sel, l_own], axis=-1).astype(jnp.float32) * scale + bias
        m_sel = jnp.broadcast_to(validc[:, :, None], (Q_BLOCK, ksel, MOBA_BLOCK)).reshape(Q_BLOCK, ksel * MOBA_BLOCK)
        m_own = kpos_own[None, :] <= qpos[:, None]
        mask = jnp.concatenate([m_sel, m_own], axis=-1)
        p = jax.nn.softmax(jnp.where(mask, logits, -jnp.inf), axis=-1).astype(v.dtype)
        p_sel, p_own = p[:, :ksel * MOBA_BLOCK], p[:, ksel * MOBA_BLOCK:]
        return (jnp.einsum('qk,qkd->qd', p_sel, v_sel.reshape(Q_BLOCK, ksel * MOBA_BLOCK, dh))
                + jnp.einsum('qk,kd->qd', p_own, v_own))

    out = lax.map(one, (q_items, sel_items, valid_items, bh_ids, c_ids, h_ids))
    return out.reshape(B, H, S, dh).transpose(0, 2, 1, 3).reshape(B, S, H * dh)


def causal_attention(q, k, v):
    B, S, H, dk = q.shape
    dv = v.shape[-1]
    nc = S // Q_BLOCK
    qb = q.reshape(B, nc, Q_BLOCK, H, dk).transpose(1, 0, 2, 3, 4)
    kpos = jnp.arange(S, dtype=jnp.int32)
    scale = dk ** -0.5

    def blk(args):
        qc, c = args
        qpos = c * Q_BLOCK + jnp.arange(Q_BLOCK, dtype=jnp.int32)
        logits = jnp.einsum('bqhd,bkhd->bhqk', qc, k).astype(jnp.float32) * scale
        logits = jnp.where(kpos[None, :] <= qpos[:, None], logits, -jnp.inf)
        p = jax.nn.softmax(logits, axis=-1).astype(v.dtype)
        return jnp.einsum('bhqk,bkhd->bqhd', p, v)

    out = lax.map(blk, (qb, jnp.arange(nc, dtype=jnp.int32)))
    return out.transpose(1, 0, 2, 3, 4).reshape(B, S, H * dv)


def memory_attention(q, k, v):
    B, S, H, d = q.shape
    logits = jnp.einsum('bshd,bmhd->bhsm', q, k).astype(jnp.float32) * (d ** -0.5)
    p = jax.nn.softmax(logits, axis=-1).astype(v.dtype)
    return jnp.einsum('bhsm,bmhd->bshd', p, v).reshape(B, S, H * d)


def setup_inputs(seed: int = 0) -> dict:
    key = jax.random.key(seed)
    ks = jax.random.split(key, 18)
    f32 = jnp.float32

    def nrm(k, shape, fan_in):
        return jax.random.normal(k, shape, f32) * fan_in ** -0.5

    def gain(k, shape):
        return 1.0 + 0.1 * jax.random.normal(k, shape, f32)

    L = DEPTH
    return {
        "x": jax.random.normal(ks[0], (BATCH, SEQ, D_MODEL), f32),
        "mem": jax.random.normal(ks[1], (BATCH, MEM_TOKENS, D_MODEL), f32),
        "g_norm": gain(ks[2], (L, D_MODEL)),
        "w_in": nrm(ks[3], (L, D_MODEL, IN_WIDTH), D_MODEL),
        "g_cq": gain(ks[4], (L, MLA_Q_LORA)),
        "w_uq": nrm(ks[5], (L, MLA_Q_LORA, MLA_HEADS * (MLA_NOPE + MLA_ROPE)), MLA_Q_LORA),
        "g_ckv": gain(ks[6], (L, MLA_KV_LORA)),
        "w_ukv": nrm(ks[7], (L, MLA_KV_LORA, MLA_HEADS * (MLA_NOPE + MLA_V)), MLA_KV_LORA),
        "g_mem": gain(ks[8], (L, D_MODEL)),
        "w_mem_kv": nrm(ks[9], (L, D_MODEL, 2 * MEM_W), D_MODEL),
        "rel_bias": 0.5 * jax.random.normal(ks[10], (N_BUCKETS, MOBA_HEADS), f32),
        "w_p_moba": nrm(ks[11], (L, MOBA_W, D_MODEL), MOBA_W),
        "w_p_mla": nrm(ks[12], (L, MLA_W, D_MODEL), MLA_W),
        "w_p_mem": nrm(ks[13], (L, MEM_W, D_MODEL), MEM_W),
        "w_out": nrm(ks[14], (L, D_MODEL, D_MODEL), D_MODEL),
        "g_final": gain(ks[15], (D_MODEL,)),
    }


def reference(x, mem, g_norm, w_in, g_cq, w_uq, g_ckv, w_ukv, g_mem, w_mem_kv, rel_bias,
              w_p_moba, w_p_mla, w_p_mem, w_out, g_final):
    B, S, _ = x.shape
    M = mem.shape[1]
    pos = jnp.arange(S, dtype=jnp.int32)
    offsets = [int(o) for o in np.cumsum(IN_SPLITS)[:-1]]
    for l in range(DEPTH):
        h = rms_norm(x, g_norm[l])
        u = h @ w_in[l]
        (q_a, k_a, v_a, z_a, c_q, c_kv, k_r, z_b, q_m, z_m,
         gl_a, gl_b, gl_m) = jnp.split(u, offsets, axis=-1)

        o_a = moba_attention(q_a.reshape(B, S, MOBA_HEADS, HEAD_DIM),
                             k_a.reshape(B, S, MOBA_HEADS, HEAD_DIM),
                             v_a.reshape(B, S, MOBA_HEADS, HEAD_DIM), rel_bias)

        q_b = (rms_norm(c_q, g_cq[l]) @ w_uq[l]).reshape(B, S, MLA_HEADS, MLA_NOPE + MLA_ROPE)
        q_b = jnp.concatenate([q_b[..., :MLA_NOPE], rope(q_b[..., MLA_NOPE:], pos)], axis=-1)
        kv_b = (rms_norm(c_kv, g_ckv[l]) @ w_ukv[l]).reshape(B, S, MLA_HEADS, MLA_NOPE + MLA_V)
        k_rope = jnp.broadcast_to(rope(k_r[:, :, None, :], pos), (B, S, MLA_HEADS, MLA_ROPE))
        k_b = jnp.concatenate([kv_b[..., :MLA_NOPE], k_rope], axis=-1)
        o_b = causal_attention(q_b, k_b, kv_b[..., MLA_NOPE:])

        mkv = rms_norm(mem, g_mem[l]) @ w_mem_kv[l]
        k_m = mkv[..., :MEM_W].reshape(B, M, MEM_HEADS, MEM_HEAD_DIM)
        v_m = mkv[..., MEM_W:].reshape(B, M, MEM_HEADS, MEM_HEAD_DIM)
        o_m = memory_attention(q_m.reshape(B, S, MEM_HEADS, MEM_HEAD_DIM), k_m, v_m)

        p_a = (o_a * jax.nn.silu(z_a)) @ w_p_moba[l]
        p_b = (o_b * jax.nn.silu(z_b)) @ w_p_mla[l]
        p_m = (o_m * jax.nn.silu(z_m)) @ w_p_mem[l]
        y = jax.nn.sigmoid(gl_a) * p_a + jax.nn.sigmoid(gl_b) * p_b + jax.nn.sigmoid(gl_m) * p_m
        x = x + y @ w_out[l]
    return rms_norm(x, g_final)
```

```python
import functools
import math

import jax
import jax.numpy as jnp
from jax import lax
from jax.experimental import pallas as pl
from jax.experimental.pallas import tpu as pltpu

D_MODEL = 2048
MOBA_HEADS = 8
HEAD_DIM = 128
MOBA_BLOCK = 256
MOBA_TOPK = 3
MLA_HEADS = 8
MLA_Q_LORA = 512
MLA_KV_LORA = 256
MLA_NOPE = 128
MLA_ROPE = 64
MLA_V = 128
ROPE_THETA = 10000.0
MEM_HEADS = 4
MEM_HEAD_DIM = 128
N_BUCKETS = 32
MAX_DISTANCE = 128
EPS = 1e-6

MOBA_W = MOBA_HEADS * HEAD_DIM
MLA_W = MLA_HEADS * MLA_V
MEM_W = MEM_HEADS * MEM_HEAD_DIM

LANES = 128
MXU_DIM = 256
V7X_VMEM_BYTES = 64 * 1024 * 1024

BF16 = jnp.bfloat16
F32 = jnp.float32
LOG2E = 1.4426950408889634

KR_PAD = MXU_DIM
OFF_GL = 0
OFF_QA = OFF_GL + 3 * D_MODEL
OFF_KA = OFF_QA + MOBA_W
OFF_VA = OFF_KA + MOBA_W
OFF_ZA = OFF_VA + MOBA_W
OFF_ZB = OFF_ZA + MOBA_W
OFF_CQ = OFF_ZB + MLA_W
OFF_CKV = OFF_CQ + MLA_Q_LORA
OFF_KR = OFF_CKV + MLA_KV_LORA
OFF_QM = OFF_KR + KR_PAD
OFF_ZM = OFF_QM + MEM_W
IN_WIDTH_P = OFF_ZM + MEM_W

TQ = MOBA_BLOCK
VT_ROWS = MLA_V + 16
MLA_QK = MXU_DIM


def _vmem_limit(nbytes):
    return int(min(nbytes + (8 << 20), V7X_VMEM_BYTES - (4 << 20)))


def _t5_thresholds():
    max_exact = N_BUCKETS // 2

    def bucket(d):
        if d < max_exact:
            return d
        large = max_exact + int(math.log(d / max_exact) / math.log(MAX_DISTANCE / max_exact)
                                * (N_BUCKETS - max_exact))
        return min(large, N_BUCKETS - 1)

    thr, d = [], 0
    for b in range(1, N_BUCKETS):
        while bucket(d) < b:
            d += 1
        thr.append(d)
    return tuple(thr)


T5_THRESHOLDS = _t5_thresholds()
assert T5_THRESHOLDS[-1] <= MOBA_BLOCK + 1


def _norm_matmul_kernel(x_ref, g_ref, w_ref, o_ref, h_ref, *, chunk):
    @pl.when(pl.program_id(1) == 0)
    def _():
        def body(r, carry):
            rows = pl.ds(pl.multiple_of(r * chunk, chunk), chunk)
            xv = x_ref[rows, :]
            ms = jnp.mean(xv * xv, axis=-1, keepdims=True)
            h_ref[rows, :] = (xv * lax.rsqrt(ms + EPS) * g_ref[...]).astype(BF16)
            return carry
        lax.fori_loop(0, x_ref.shape[0] // chunk, body, 0)

    o_ref[...] = jnp.dot(h_ref[...], w_ref[...], preferred_element_type=F32).astype(o_ref.dtype)


def _norm_matmul(x, g, w, *, tm, tn):
    t, d = x.shape
    n = w.shape[1]
    est = 2 * tm * d * 4 + tm * d * 2 + 2 * d * tn * 2 + 2 * tm * tn * 2 + tm * tn * 4
    return pl.pallas_call(
        functools.partial(_norm_matmul_kernel, chunk=64),
        grid=(t // tm, n // tn),
        in_specs=[
            pl.BlockSpec((tm, d), lambda i, j: (i, 0)),
            pl.BlockSpec((1, d), lambda i, j: (0, 0)),
            pl.BlockSpec((d, tn), lambda i, j: (0, j)),
        ],
        out_specs=pl.BlockSpec((tm, tn), lambda i, j: (i, j)),
        out_shape=jax.ShapeDtypeStruct((t, n), BF16),
        scratch_shapes=[pltpu.VMEM((tm, d), BF16)],
        compiler_params=pltpu.CompilerParams(
            dimension_semantics=("arbitrary", "arbitrary"), vmem_limit_bytes=_vmem_limit(est)),
        name="norm_matmul",
    )(x, g.reshape(1, d), w)


def _bias_tiles_kernel(rb_ref, own_ref, prev_ref, *, inv_scale):
    h = pl.program_id(0)
    key = lax.broadcasted_iota(jnp.int32, (TQ, TQ), 0)
    qry = lax.broadcasted_iota(jnp.int32, (TQ, TQ), 1)
    d_own = qry - key
    d_prev = d_own + MOBA_BLOCK

    def lookup(dist):
        val = jnp.zeros(dist.shape, F32) + rb_ref[0, h]
        for b in range(1, N_BUCKETS):
            val = jnp.where(dist >= T5_THRESHOLDS[b - 1], rb_ref[b, h], val)
        return val * inv_scale

    own_ref[0] = jnp.where(d_own >= 0, lookup(d_own), -jnp.inf)
    prev_ref[0] = lookup(d_prev)


def _bias_tiles(rel_bias, inv_scale):
    heads = rel_bias.shape[1]
    tile = jax.ShapeDtypeStruct((heads, TQ, TQ), F32)
    spec = pl.BlockSpec((1, TQ, TQ), lambda h: (h, 0, 0))
    return pl.pallas_call(
        functools.partial(_bias_tiles_kernel, inv_scale=inv_scale),
        grid=(heads,),
        in_specs=[pl.BlockSpec(memory_space=pltpu.SMEM)],
        out_specs=[spec, spec],
        out_shape=[tile, tile],
        name="bias_tiles",
    )(rel_bias)


def _build_vt(v_ref, vt_ref, n_keys):
    dv = v_ref.shape[-1]
    for j in range(n_keys // TQ):
        blk = slice(j * TQ, (j + 1) * TQ)
        vt_ref[0:dv, blk] = v_ref[0, blk, :].astype(F32).T.astype(BF16)
    row = lax.broadcasted_iota(jnp.int32, (VT_ROWS - dv, n_keys), 0)
    vt_ref[dv:VT_ROWS, :] = jnp.where(row == 0, 1.0, 0.0).astype(BF16)


def _softmax_pv(t_scr, p_scr, vt_ref, n_blk, scale, adds, consts, sels):
    inv_scale = 1.0 / scale
    m_log = None
    for j in range(n_blk):
        blk = slice(j * TQ, (j + 1) * TQ)
        tj = t_scr[blk, :]
        if adds[j] is not None:
            tj = tj + adds[j]
            t_scr[blk, :] = tj
        mj = jnp.max(tj, axis=0, keepdims=True) * scale + consts[j]
        if sels[j] is not None:
            mj = jnp.where(sels[j], mj, -jnp.inf)
        m_log = mj if m_log is None else jnp.maximum(m_log, mj)
    for j in range(n_blk):
        blk = slice(j * TQ, (j + 1) * TQ)
        off = (m_log - consts[j]) * inv_scale
        if sels[j] is not None:
            off = jnp.where(sels[j], off, jnp.inf)
        p_scr[blk, :] = jnp.exp2((t_scr[blk, :] - off) * (scale * LOG2E)).astype(BF16)
    n_keys = n_blk * TQ
    ot = jnp.dot(vt_ref[:, 0:n_keys], p_scr[0:n_keys, :], preferred_element_type=F32)
    dv = vt_ref.shape[0] - (VT_ROWS - MLA_V)
    return (ot[0:dv, :] / ot[dv:dv + 1, :]).T


def _scores(k_ref, q_tile, t_scr, n_keys):
    t_scr[0:n_keys, :] = lax.dot_general(
        k_ref[0, 0:n_keys, :], q_tile, (((1,), (1,)), ((), ())), preferred_element_type=F32)


def _silu_gate(o, z):
    zf = z.astype(F32)
    return (o * (zf / (1.0 + jnp.exp(-zf)))).astype(BF16)


def _causal_tile():
    key = lax.broadcasted_iota(jnp.int32, (TQ, TQ), 0)
    qry = lax.broadcasted_iota(jnp.int32, (TQ, TQ), 1)
    return jnp.where(key <= qry, 0.0, -jnp.inf).astype(F32)


def _moba_kernel(rb_ref, q_ref, k_ref, v_ref, z_ref, own_ref, prev_ref, o_ref,
                 vt_ref, t_scr, p_scr, *, scale):
    seq = q_ref.shape[1]
    n_tiles = seq // TQ
    far_const = rb_ref[N_BUCKETS - 1, pl.program_id(1)]
    _build_vt(v_ref, vt_ref, seq)

    gate_rows = 16
    row = lax.broadcasted_iota(jnp.int32, (gate_rows, seq), 0)
    col_blk = lax.broadcasted_iota(jnp.int32, (gate_rows, seq), 1) // MOBA_BLOCK
    avg = jnp.where(row == col_blk, 1.0 / MOBA_BLOCK, 0.0).astype(BF16)
    k_mean = jnp.dot(avg, k_ref[0], preferred_element_type=F32)
    km1 = k_mean.astype(BF16)
    rem = k_mean - km1.astype(F32)
    km2 = rem.astype(BF16)
    km3 = (rem - km2.astype(F32)).astype(BF16)

    nt = (((1,), (1,)), ((), ()))
    for i in range(n_tiles):
        rows = slice(i * TQ, (i + 1) * TQ)
        q_tile = q_ref[0, rows, :]
        _scores(k_ref, q_tile, t_scr, (i + 1) * TQ)
        if i > MOBA_TOPK:
            gate = (lax.dot_general(km1, q_tile, nt, preferred_element_type=F32)
                    + lax.dot_general(km2, q_tile, nt, preferred_element_type=F32)
                    + lax.dot_general(km3, q_tile, nt, preferred_element_type=F32))
            sels = []
            for j in range(i):
                gj = gate[j:j + 1, :]
                cnt = jnp.zeros(gj.shape, F32)
                for jp in range(i):
                    if jp == j:
                        continue
                    gp = gate[jp:jp + 1, :]
                    beats = (gp >= gj) if jp < j else (gp > gj)
                    cnt = cnt + jnp.where(beats, 1.0, 0.0)
                sels.append(cnt < MOBA_TOPK)
        else:
            sels = [None] * i
        sels.append(None)
        adds = [None] * (i + 1)
        consts = [far_const] * (i + 1)
        adds[i], consts[i] = own_ref[0], 0.0
        if i >= 1:
            adds[i - 1], consts[i - 1] = prev_ref[0], 0.0
        o = _softmax_pv(t_scr, p_scr, vt_ref, i + 1, scale, adds, consts, sels)
        o_ref[0, rows, :] = _silu_gate(o, z_ref[0, rows, :])


def _moba_attn(u3, rel_bias, own, prev):
    b, s, _ = u3.shape
    col = lambda off: (lambda bi, h: (bi, 0, off // HEAD_DIM + h))
    blk = (1, s, HEAD_DIM)
    tile_spec = pl.BlockSpec((1, TQ, TQ), lambda bi, h: (h, 0, 0))
    est = 10 * s * HEAD_DIM * 2 + 4 * TQ * TQ * 4 + VT_ROWS * s * 2 + s * TQ * 6
    return pl.pallas_call(
        functools.partial(_moba_kernel, scale=HEAD_DIM ** -0.5),
        grid=(b, MOBA_HEADS),
        in_specs=[
            pl.BlockSpec(memory_space=pltpu.SMEM),
            pl.BlockSpec(blk, col(OFF_QA)),
            pl.BlockSpec(blk, col(OFF_KA)),
            pl.BlockSpec(blk, col(OFF_VA)),
            pl.BlockSpec(blk, col(OFF_ZA)),
            tile_spec, tile_spec,
        ],
        out_specs=pl.BlockSpec(blk, lambda bi, h: (bi, 0, h)),
        out_shape=jax.ShapeDtypeStruct((b, s, MOBA_W), BF16),
        scratch_shapes=[pltpu.VMEM((VT_ROWS, s), BF16), pltpu.VMEM((s, TQ), F32),
                        pltpu.VMEM((s, TQ), BF16)],
        compiler_params=pltpu.CompilerParams(
            dimension_semantics=("arbitrary", "arbitrary"), vmem_limit_bytes=_vmem_limit(est)),
        name="moba_attn",
    )(rel_bias, u3, u3, u3, u3, own, prev)


def _mla_prep_kernel(cq_ref, ckv_ref, kr_ref, gq_ref, gkv_ref, wuq_ref, wukv_ref, cos_ref, sin_ref,
                     q_out, k_out, v_out):
    def rms(x_ref, g_ref):
        xf = x_ref[...].astype(F32)
        ms = jnp.mean(xf * xf, axis=-1, keepdims=True)
        return (xf * lax.rsqrt(ms + EPS) * g_ref[...]).astype(BF16)

    cos = cos_ref[...]
    sin = sin_ref[...]
    half = MLA_ROPE // 2
    first_half = lax.broadcasted_iota(jnp.int32, cos.shape, 1) < half

    def rope(xr):
        partner = jnp.where(first_half, pltpu.roll(xr, LANES - half, 1), pltpu.roll(xr, half, 1))
        return xr * cos + partner * sin

    qb = jnp.dot(rms(cq_ref, gq_ref), wuq_ref[...], preferred_element_type=F32)
    for h in range(MLA_HEADS):
        base = h * MLA_QK
        q_out[:, base:base + MLA_NOPE] = qb[:, base:base + MLA_NOPE].astype(BF16)
        q_out[:, base + MLA_NOPE:base + MLA_QK] = rope(qb[:, base + MLA_NOPE:base + MLA_QK]).astype(BF16)

    kvb = jnp.dot(rms(ckv_ref, gkv_ref), wukv_ref[...], preferred_element_type=F32)
    k_rope = rope(kr_ref[:, 0:LANES].astype(F32)).astype(BF16)
    for h in range(MLA_HEADS):
        base = h * MLA_QK
        k_out[:, base:base + MLA_NOPE] = kvb[:, h * MLA_NOPE:(h + 1) * MLA_NOPE].astype(BF16)
        k_out[:, base + MLA_NOPE:base + MLA_QK] = k_rope
    v_out[...] = kvb[:, MLA_W:2 * MLA_W].astype(BF16)


def _mla_prep(u, g_cq, g_ckv, wuq_p, wukv_p, cos_t, sin_t, *, seq, tm):
    t = u.shape[0]
    s_tiles = seq // tm
    const = lambda i: (0, 0)
    qk_shape = jax.ShapeDtypeStruct((t, MLA_HEADS * MLA_QK), BF16)
    est = (2 * tm * (MLA_Q_LORA + 2 * MLA_KV_LORA) * 2 + 2 * (wuq_p.size + wukv_p.size) * 2
           + 4 * tm * LANES * 4 + 2 * tm * 5 * MLA_W * 2 + 4 * tm * 2 * MLA_W * 4)
    return pl.pallas_call(
        _mla_prep_kernel,
        grid=(t // tm,),
        in_specs=[
            pl.BlockSpec((tm, MLA_Q_LORA), lambda i: (i, OFF_CQ // MLA_Q_LORA)),
            pl.BlockSpec((tm, MLA_KV_LORA), lambda i: (i, OFF_CKV // MLA_KV_LORA)),
            pl.BlockSpec((tm, KR_PAD), lambda i: (i, OFF_KR // KR_PAD)),
            pl.BlockSpec((1, MLA_Q_LORA), const),
            pl.BlockSpec((1, MLA_KV_LORA), const),
            pl.BlockSpec(wuq_p.shape, const),
            pl.BlockSpec(wukv_p.shape, const),
            pl.BlockSpec((tm, LANES), lambda i: (i % s_tiles, 0)),
            pl.BlockSpec((tm, LANES), lambda i: (i % s_tiles, 0)),
        ],
        out_specs=[
            pl.BlockSpec((tm, MLA_HEADS * MLA_QK), lambda i: (i, 0)),
            pl.BlockSpec((tm, MLA_HEADS * MLA_QK), lambda i: (i, 0)),
            pl.BlockSpec((tm, MLA_W), lambda i: (i, 0)),
        ],
        out_shape=[qk_shape, qk_shape, jax.ShapeDtypeStruct((t, MLA_W), BF16)],
        compiler_params=pltpu.CompilerParams(
            dimension_semantics=("arbitrary",), vmem_limit_bytes=_vmem_limit(est)),
        name="mla_prep",
    )(u, u, u, g_cq.reshape(1, -1), g_ckv.reshape(1, -1), wuq_p, wukv_p, cos_t, sin_t)


def _mla_attn_kernel(q_ref, k_ref, v_ref, z_ref, o_ref, vt_ref, t_scr, p_scr, *, scale):
    seq = q_ref.shape[1]
    _build_vt(v_ref, vt_ref, seq)
    causal = _causal_tile()
    for i in range(seq // TQ):
        rows = slice(i * TQ, (i + 1) * TQ)
        _scores(k_ref, q_ref[0, rows, :], t_scr, (i + 1) * TQ)
        adds = [None] * i + [causal]
        o = _softmax_pv(t_scr, p_scr, vt_ref, i + 1, scale, adds, [0.0] * (i + 1), [None] * (i + 1))
        o_ref[0, rows, :] = _silu_gate(o, z_ref[0, rows, :])


def _mla_attn(q3, k3, v3, u3):
    b, s, _ = q3.shape
    est = 4 * s * MLA_QK * 2 + 6 * s * MLA_V * 2 + VT_ROWS * s * 2 + s * TQ * 6
    return pl.pallas_call(
        functools.partial(_mla_attn_kernel, scale=(MLA_NOPE + MLA_ROPE) ** -0.5),
        grid=(b, MLA_HEADS),
        in_specs=[
            pl.BlockSpec((1, s, MLA_QK), lambda bi, h: (bi, 0, h)),
            pl.BlockSpec((1, s, MLA_QK), lambda bi, h: (bi, 0, h)),
            pl.BlockSpec((1, s, MLA_V), lambda bi, h: (bi, 0, h)),
            pl.BlockSpec((1, s, MLA_V), lambda bi, h: (bi, 0, OFF_ZB // MLA_V + h)),
        ],
        out_specs=pl.BlockSpec((1, s, MLA_V), lambda bi, h: (bi, 0, h)),
        out_shape=jax.ShapeDtypeStruct((b, s, MLA_W), BF16),
        scratch_shapes=[pltpu.VMEM((VT_ROWS, s), BF16), pltpu.VMEM((s, TQ), F32),
                        pltpu.VMEM((s, TQ), BF16)],
        compiler_params=pltpu.CompilerParams(
            dimension_semantics=("arbitrary", "arbitrary"), vmem_limit_bytes=_vmem_limit(est)),
        name="mla_attn",
    )(q3, k3, v3, u3)


def _mem_attn_kernel(q_ref, k_ref, v_ref, z_ref, o_ref, vt_ref, t_scr, p_scr, *, scale):
    seq = q_ref.shape[1]
    n_mem = k_ref.shape[1]
    assert n_mem == TQ
    _build_vt(v_ref, vt_ref, n_mem)
    for i in range(seq // TQ):
        rows = slice(i * TQ, (i + 1) * TQ)
        _scores(k_ref, q_ref[0, rows, :], t_scr, n_mem)
        o = _softmax_pv(t_scr, p_scr, vt_ref, 1, scale, [None], [0.0], [None])
        o_ref[0, rows, :] = _silu_gate(o, z_ref[0, rows, :])


def _mem_attn(u3, kv3):
    b, s, _ = u3.shape
    m = kv3.shape[1]
    d = MEM_HEAD_DIM
    est = 6 * s * d * 2 + 4 * m * d * 2 + VT_ROWS * m * 2 + m * TQ * 6
    return pl.pallas_call(
        functools.partial(_mem_attn_kernel, scale=d ** -0.5),
        grid=(b, MEM_HEADS),
        in_specs=[
            pl.BlockSpec((1, s, d), lambda bi, h: (bi, 0, OFF_QM // d + h)),
            pl.BlockSpec((1, m, d), lambda bi, h: (bi, 0, h)),
            pl.BlockSpec((1, m, d), lambda bi, h: (bi, 0, MEM_HEADS + h)),
            pl.BlockSpec((1, s, d), lambda bi, h: (bi, 0, OFF_ZM // d + h)),
        ],
        out_specs=pl.BlockSpec((1, s, d), lambda bi, h: (bi, 0, h)),
        out_shape=jax.ShapeDtypeStruct((b, s, MEM_W), BF16),
        scratch_shapes=[pltpu.VMEM((VT_ROWS, m), BF16), pltpu.VMEM((m, TQ), F32),
                        pltpu.VMEM((m, TQ), BF16)],
        compiler_params=pltpu.CompilerParams(
            dimension_semantics=("arbitrary", "arbitrary"), vmem_limit_bytes=_vmem_limit(est)),
        name="mem_attn",
    )(u3, kv3, kv3, u3)


def _merge_kernel(ga_ref, gb_ref, gm_ref, gl_ref, x_ref, wpa_ref, wpb_ref, wpm_ref, wout_ref, gf_ref,
                  o_ref, *, final_norm):
    d = x_ref.shape[1]

    def gated(idx, g_ref, w_ref):
        logit = gl_ref[:, idx * d:(idx + 1) * d].astype(F32)
        return jnp.dot(g_ref[...], w_ref[...], preferred_element_type=F32) / (1.0 + jnp.exp(-logit))

    y = gated(0, ga_ref, wpa_ref) + gated(1, gb_ref, wpb_ref) + gated(2, gm_ref, wpm_ref)
    r = x_ref[...] + jnp.dot(y.astype(BF16), wout_ref[...], preferred_element_type=F32)
    if final_norm:
        ms = jnp.mean(r * r, axis=-1, keepdims=True)
        r = r * lax.rsqrt(ms + EPS) * gf_ref[...]
    o_ref[...] = r


def _merge(ga, gb, gm, u, x, wpa, wpb, wpm, wout, g_final, *, tm, final_norm):
    t, d = x.shape
    const = lambda i: (0, 0)
    resident = lambda w: pl.BlockSpec(w.shape, const, pipeline_mode=pl.Buffered(1))
    rows = lambda width: pl.BlockSpec((tm, width), lambda i: (i, 0))
    w_bytes = (wpa.size + wpb.size + wpm.size + wout.size) * 2
    est = w_bytes + 2 * tm * (2 * MOBA_W + MEM_W + 3 * d) * 2 + 4 * tm * d * 4 + 6 * tm * d * 4
    return pl.pallas_call(
        functools.partial(_merge_kernel, final_norm=final_norm),
        grid=(t // tm,),
        in_specs=[
            rows(MOBA_W), rows(MLA_W), rows(MEM_W),
            pl.BlockSpec((tm, 3 * d), lambda i: (i, OFF_GL // (3 * d))),
            rows(d),
            resident(wpa), resident(wpb), resident(wpm), resident(wout),
            pl.BlockSpec((1, d), const),
        ],
        out_specs=rows(d),
        out_shape=jax.ShapeDtypeStruct((t, d), F32),
        compiler_params=pltpu.CompilerParams(
            dimension_semantics=("arbitrary",), vmem_limit_bytes=_vmem_limit(est)),
        name="merge",
    )(ga, gb, gm, u, x, wpa, wpb, wpm, wout, g_final.reshape(1, d))


def _regroup_w_in(w):
    d = w.shape[0]
    o_za_end = 4 * MOBA_W
    o_cq = o_za_end
    o_kr = o_cq + MLA_Q_LORA + MLA_KV_LORA
    o_zb = o_kr + MLA_ROPE
    o_qm = o_zb + MLA_W
    o_gl = o_qm + 2 * MEM_W
    parts = [
        w[:, o_gl:o_gl + 3 * D_MODEL],
        w[:, 0:o_za_end],
        w[:, o_zb:o_zb + MLA_W],
        w[:, o_cq:o_kr],
        w[:, o_kr:o_zb],
        jnp.zeros((d, KR_PAD - MLA_ROPE), w.dtype),
        w[:, o_qm:o_gl],
    ]
    out = jnp.concatenate(parts, axis=1).astype(BF16)
    assert out.shape[1] == IN_WIDTH_P
    return out


def _regroup_w_uq(w):
    r = w.shape[0]
    w3 = w.reshape(r, MLA_HEADS, MLA_NOPE + MLA_ROPE)
    pad = jnp.zeros((r, MLA_HEADS, MLA_QK - MLA_NOPE - MLA_ROPE), w.dtype)
    return jnp.concatenate([w3, pad], axis=-1).reshape(r, MLA_HEADS * MLA_QK).astype(BF16)


def _regroup_w_ukv(w):
    r = w.shape[0]
    w3 = w.reshape(r, MLA_HEADS, MLA_NOPE + MLA_V)
    return jnp.concatenate([w3[:, :, :MLA_NOPE].reshape(r, MLA_W),
                            w3[:, :, MLA_NOPE:].reshape(r, MLA_W)], axis=1).astype(BF16)


def _rope_tables(seq):
    half = MLA_ROPE // 2
    inv = ROPE_THETA ** (-jnp.arange(half, dtype=F32) / half)
    ang = jnp.arange(seq, dtype=jnp.int32).astype(F32)[:, None] * inv[None, :]
    cos, sin = jnp.cos(ang), jnp.sin(ang)
    pad = LANES - MLA_ROPE
    cos_t = jnp.concatenate([cos, cos, jnp.ones((seq, pad), F32)], axis=1)
    sin_t = jnp.concatenate([-sin, sin, jnp.zeros((seq, pad), F32)], axis=1)
    return cos_t, sin_t


def kernel(x, mem, g_norm, w_in, g_cq, w_uq, g_ckv, w_ukv, g_mem, w_mem_kv, rel_bias,
           w_p_moba, w_p_mla, w_p_mem, w_out, g_final):
    b, s, d = x.shape
    m = mem.shape[1]
    depth = w_in.shape[0]
    t = b * s
    assert d == D_MODEL and s % TQ == 0 and m == TQ

    own, prev = _bias_tiles(rel_bias, HEAD_DIM ** 0.5)
    cos_t, sin_t = _rope_tables(s)
    mem2 = mem.reshape(b * m, d)
    xs = x.reshape(t, d)
    for l in range(depth):
        u = _norm_matmul(xs, g_norm[l], _regroup_w_in(w_in[l]), tm=1024, tn=1024)
        u3 = u.reshape(b, s, IN_WIDTH_P)
        ga = _moba_attn(u3, rel_bias, own, prev)
        q2, k2, v2 = _mla_prep(u, g_cq[l], g_ckv[l], _regroup_w_uq(w_uq[l]), _regroup_w_ukv(w_ukv[l]),
                               cos_t, sin_t, seq=s, tm=512)
        gb = _mla_attn(q2.reshape(b, s, -1), k2.reshape(b, s, -1), v2.reshape(b, s, -1), u3)
        kvm = _norm_matmul(mem2, g_mem[l], w_mem_kv[l].astype(BF16), tm=b * m, tn=2 * MEM_W)
        gm = _mem_attn(u3, kvm.reshape(b, m, 2 * MEM_W))
        xs = _merge(ga.reshape(t, MOBA_W), gb.reshape(t, MLA_W), gm.reshape(t, MEM_W), u, xs,
                    w_p_moba[l].astype(BF16), w_p_mla[l].astype(BF16), w_p_mem[l].astype(BF16),
                    w_out[l].astype(BF16), g_final, tm=256, final_norm=(l == depth - 1))
    return xs.reshape(b, s, d)
```

```python
import functools
import math

import jax
import jax.numpy as jnp
from jax import lax
from jax.experimental import pallas as pl
from jax.experimental.pallas import tpu as pltpu

D_MODEL = 2048
MOBA_HEADS = 8
HEAD_DIM = 128
MOBA_BLOCK = 256
MOBA_TOPK = 3
MLA_HEADS = 8
MLA_Q_LORA = 512
MLA_KV_LORA = 256
MLA_NOPE = 128
MLA_ROPE = 64
MLA_V = 128
ROPE_THETA = 10000.0
MEM_HEADS = 4
MEM_HEAD_DIM = 128
N_BUCKETS = 32
MAX_DISTANCE = 128
EPS = 1e-6

MOBA_W = MOBA_HEADS * HEAD_DIM
MLA_W = MLA_HEADS * MLA_V
MEM_W = MEM_HEADS * MEM_HEAD_DIM

LANES = 128
MXU_DIM = 256
V7X_VMEM_BYTES = 64 * 1024 * 1024

BF16 = jnp.bfloat16
F32 = jnp.float32
LOG2E = 1.4426950408889634

KR_PAD = MXU_DIM
OFF_GL = 0
OFF_QA = OFF_GL + 3 * D_MODEL
OFF_KA = OFF_QA + MOBA_W
OFF_VA = OFF_KA + MOBA_W
OFF_ZA = OFF_VA + MOBA_W
OFF_ZB = OFF_ZA + MOBA_W
OFF_CQ = OFF_ZB + MLA_W
OFF_CKV = OFF_CQ + MLA_Q_LORA
OFF_KR = OFF_CKV + MLA_KV_LORA
OFF_QM = OFF_KR + KR_PAD
OFF_ZM = OFF_QM + MEM_W
IN_WIDTH_P = OFF_ZM + MEM_W

TQ = MOBA_BLOCK
ONES_ROWS = 16
VT_ROWS = MLA_V + ONES_ROWS
MLA_QK = MXU_DIM

W_TILE = 1024
W_WIN = W_TILE + LANES


def _vmem_limit(nbytes):
    return int(min(nbytes + (8 << 20), V7X_VMEM_BYTES - (4 << 20)))


def _t5_thresholds():
    max_exact = N_BUCKETS // 2

    def bucket(d):
        if d < max_exact:
            return d
        large = max_exact + int(math.log(d / max_exact) / math.log(MAX_DISTANCE / max_exact)
                                * (N_BUCKETS - max_exact))
        return min(large, N_BUCKETS - 1)

    thr, d = [], 0
    for b in range(1, N_BUCKETS):
        while bucket(d) < b:
            d += 1
        thr.append(d)
    return tuple(thr)


T5_THRESHOLDS = _t5_thresholds()
assert T5_THRESHOLDS[-1] <= MOBA_BLOCK + 1


def _norm_matmul_kernel(x_ref, g_ref, w_ref, cs_ref, o_ref, h_ref, *, chunk):
    @pl.when(pl.program_id(1) == 0)
    def _():
        def body(r, carry):
            rows = pl.ds(pl.multiple_of(r * chunk, chunk), chunk)
            xv = x_ref[rows, :]
            ms = jnp.mean(xv * xv, axis=-1, keepdims=True)
            h_ref[rows, :] = (xv * lax.rsqrt(ms + EPS) * g_ref[...]).astype(BF16)
            return carry
        lax.fori_loop(0, x_ref.shape[0] // chunk, body, 0)

    acc = jnp.dot(h_ref[...], w_ref[...], preferred_element_type=F32)
    o_ref[...] = (acc * cs_ref[...]).astype(o_ref.dtype)


def _norm_matmul(x, g, w, col_scale, *, tm, tn):
    t, d = x.shape
    n = w.shape[1]
    est = 2 * tm * d * 4 + tm * d * 2 + 2 * d * tn * 2 + 2 * tm * tn * 2 + tm * tn * 4
    return pl.pallas_call(
        functools.partial(_norm_matmul_kernel, chunk=64),
        grid=(t // tm, n // tn),
        in_specs=[
            pl.BlockSpec((tm, d), lambda i, j: (i, 0)),
            pl.BlockSpec((1, d), lambda i, j: (0, 0)),
            pl.BlockSpec((d, tn), lambda i, j: (0, j)),
            pl.BlockSpec((1, tn), lambda i, j: (0, j)),
        ],
        out_specs=pl.BlockSpec((tm, tn), lambda i, j: (i, j)),
        out_shape=jax.ShapeDtypeStruct((t, n), BF16),
        scratch_shapes=[pltpu.VMEM((tm, d), BF16)],
        compiler_params=pltpu.CompilerParams(
            dimension_semantics=("arbitrary", "arbitrary"), vmem_limit_bytes=_vmem_limit(est)),
        name="norm_matmul",
    )(x, g.reshape(1, d), w, col_scale)


def _bias_tiles_kernel(rb_ref, own_ref, prev_ref):
    h = pl.program_id(0)
    key = lax.broadcasted_iota(jnp.int32, (TQ, TQ), 0)
    qry = lax.broadcasted_iota(jnp.int32, (TQ, TQ), 1)
    d_own = qry - key
    d_prev = d_own + MOBA_BLOCK

    def lookup(dist):
        val = jnp.zeros(dist.shape, F32) + rb_ref[0, h]
        for b in range(1, N_BUCKETS):
            val = jnp.where(dist >= T5_THRESHOLDS[b - 1], rb_ref[b, h], val)
        return val * LOG2E

    own_ref[0] = jnp.where(d_own >= 0, lookup(d_own), -jnp.inf)
    prev_ref[0] = lookup(d_prev)


def _bias_tiles(rel_bias):
    heads = rel_bias.shape[1]
    tile = jax.ShapeDtypeStruct((heads, TQ, TQ), F32)
    spec = pl.BlockSpec((1, TQ, TQ), lambda h: (h, 0, 0))
    return pl.pallas_call(
        _bias_tiles_kernel,
        grid=(heads,),
        in_specs=[pl.BlockSpec(memory_space=pltpu.SMEM)],
        out_specs=[spec, spec],
        out_shape=[tile, tile],
        name="bias_tiles",
    )(rel_bias)


def _build_vt(v_ref, vt_ref, n_keys):
    dv = v_ref.shape[-1]
    for j in range(n_keys // TQ):
        blk = slice(j * TQ, (j + 1) * TQ)
        vt_ref[0:dv, blk] = v_ref[0, blk, :].astype(F32).T.astype(BF16)
    row = lax.broadcasted_iota(jnp.int32, (VT_ROWS - dv, n_keys), 0)
    vt_ref[dv:VT_ROWS, :] = jnp.where(row == 0, 1.0, 0.0).astype(BF16)


def _attention(n_tiles, q_tile_of, k_block_of, terms_of, vt_ref, t_bufs, emit):
    nt = (((1,), (1,)), ((), ()))
    dv = vt_ref.shape[0] - ONES_ROWS
    state = {}

    def stage1(i):
        q = q_tile_of(i)
        adds, consts, sels = terms_of(i, q)
        st = state[i] = dict(consts=consts, sels=sels, m=None, acc=None)
        t_buf = t_bufs[i % 2]

        def item(j):
            blk = slice(j * TQ, (j + 1) * TQ)
            t = lax.dot_general(k_block_of(j), q, nt, preferred_element_type=F32)
            if adds[j] is not None:
                t = t + adds[j][0]
            t_buf[blk, :] = t
            mj = jnp.max(t, axis=0, keepdims=True) + consts[j]
            if sels[j] is not None:
                mj = jnp.where(sels[j], mj, -jnp.inf)
            st["m"] = mj if st["m"] is None else jnp.maximum(st["m"], mj)
        return [functools.partial(item, j) for j in range(len(consts))]

    def stage2(i):
        st = state.pop(i)
        t_buf = t_bufs[i % 2]

        def item(j):
            blk = slice(j * TQ, (j + 1) * TQ)
            off = st["m"] - st["consts"][j]
            if st["sels"][j] is not None:
                off = jnp.where(st["sels"][j], off, jnp.inf)
            p = jnp.exp2(t_buf[blk, :] - off).astype(BF16)
            part = jnp.dot(vt_ref[:, blk], p, preferred_element_type=F32)
            st["acc"] = part if st["acc"] is None else st["acc"] + part

        def finish():
            acc = st["acc"]
            emit(i, (acc[0:dv, :] / acc[dv:dv + 1, :]).T)
        return [functools.partial(item, j) for j in range(len(st["consts"]))], finish

    for item in stage1(0):
        item()
    for i in range(n_tiles):
        ahead = stage1(i + 1) if i + 1 < n_tiles else []
        behind, finish = stage2(i)
        for k in range(max(len(ahead), len(behind))):
            if k < len(ahead):
                ahead[k]()
            if k < len(behind):
                behind[k]()
        finish()


def _silu_gate(o, z):
    zf = z.astype(F32)
    return (o * (zf / (1.0 + jnp.exp(-zf)))).astype(BF16)


def _write_causal_tile(mask_ref):
    key = lax.broadcasted_iota(jnp.int32, (TQ, TQ), 0)
    qry = lax.broadcasted_iota(jnp.int32, (TQ, TQ), 1)
    mask_ref[0] = jnp.where(key <= qry, 0.0, -jnp.inf).astype(F32)


def _moba_kernel(rb_ref, q_ref, k_ref, v_ref, z_ref, own_ref, prev_ref, o_ref, vt_ref, t0, t1):
    seq = q_ref.shape[1]
    n_tiles = seq // TQ
    far_const = rb_ref[N_BUCKETS - 1, pl.program_id(1)] * LOG2E
    _build_vt(v_ref, vt_ref, seq)

    gate_rows = 16
    row = lax.broadcasted_iota(jnp.int32, (gate_rows, seq), 0)
    col_blk = lax.broadcasted_iota(jnp.int32, (gate_rows, seq), 1) // MOBA_BLOCK
    avg = jnp.where(row == col_blk, 1.0 / MOBA_BLOCK, 0.0).astype(BF16)
    k_mean = jnp.dot(avg, k_ref[0], preferred_element_type=F32)
    km1 = k_mean.astype(BF16)
    rem = k_mean - km1.astype(F32)
    km2 = rem.astype(BF16)
    km3 = (rem - km2.astype(F32)).astype(BF16)

    nt = (((1,), (1,)), ((), ()))

    def terms_of(i, q_tile):
        if i > MOBA_TOPK:
            gate = (lax.dot_general(km1, q_tile, nt, preferred_element_type=F32)
                    + lax.dot_general(km2, q_tile, nt, preferred_element_type=F32)
                    + lax.dot_general(km3, q_tile, nt, preferred_element_type=F32))
            sels = []
            for j in range(i):
                gj = gate[j:j + 1, :]
                cnt = jnp.zeros(gj.shape, F32)
                for jp in range(i):
                    if jp == j:
                        continue
                    gp = gate[jp:jp + 1, :]
                    beats = (gp >= gj) if jp < j else (gp > gj)
                    cnt = cnt + jnp.where(beats, 1.0, 0.0)
                sels.append(cnt < MOBA_TOPK)
        else:
            sels = [None] * i
        sels.append(None)
        adds = [None] * (i + 1)
        consts = [far_const] * (i + 1)
        adds[i], consts[i] = own_ref, 0.0
        if i >= 1:
            adds[i - 1], consts[i - 1] = prev_ref, 0.0
        return adds, consts, sels

    def emit(i, o):
        rows = slice(i * TQ, (i + 1) * TQ)
        o_ref[0, rows, :] = _silu_gate(o, z_ref[0, rows, :])

    _attention(n_tiles, lambda i: q_ref[0, i * TQ:(i + 1) * TQ, :],
               lambda j: k_ref[0, j * TQ:(j + 1) * TQ, :], terms_of, vt_ref, (t0, t1), emit)


def _moba_attn(u3, rel_bias, own, prev):
    b, s, _ = u3.shape
    col = lambda off: (lambda bi, h: (bi, 0, off // HEAD_DIM + h))
    blk = (1, s, HEAD_DIM)
    tile_spec = pl.BlockSpec((1, TQ, TQ), lambda bi, h: (h, 0, 0))
    est = 10 * s * HEAD_DIM * 2 + 4 * TQ * TQ * 4 + VT_ROWS * s * 2 + 2 * s * TQ * 4
    return pl.pallas_call(
        _moba_kernel,
        grid=(b, MOBA_HEADS),
        in_specs=[
            pl.BlockSpec(memory_space=pltpu.SMEM),
            pl.BlockSpec(blk, col(OFF_QA)),
            pl.BlockSpec(blk, col(OFF_KA)),
            pl.BlockSpec(blk, col(OFF_VA)),
            pl.BlockSpec(blk, col(OFF_ZA)),
            tile_spec, tile_spec,
        ],
        out_specs=pl.BlockSpec(blk, lambda bi, h: (bi, 0, h)),
        out_shape=jax.ShapeDtypeStruct((b, s, MOBA_W), BF16),
        scratch_shapes=[pltpu.VMEM((VT_ROWS, s), BF16), pltpu.VMEM((s, TQ), F32), pltpu.VMEM((s, TQ), F32)],
        compiler_params=pltpu.CompilerParams(
            dimension_semantics=("arbitrary", "arbitrary"), vmem_limit_bytes=_vmem_limit(est)),
        name="moba_attn",
    )(rel_bias, u3, u3, u3, u3, own, prev)


def _mla_prep_kernel(cq_ref, ckv_ref, kr_ref, gq_ref, gkv_ref, wuq_ref, wukv_ref, cos_ref, sin_ref,
                     q_out, k_out, v_out, *, q_scale):
    def rms(x_ref, g_ref):
        xf = x_ref[...].astype(F32)
        ms = jnp.mean(xf * xf, axis=-1, keepdims=True)
        return (xf * lax.rsqrt(ms + EPS) * g_ref[...]).astype(BF16)

    cos = cos_ref[...]
    sin = sin_ref[...]
    half = MLA_ROPE // 2
    first_half = lax.broadcasted_iota(jnp.int32, cos.shape, 1) < half

    def rope(xr):
        partner = jnp.where(first_half, pltpu.roll(xr, LANES - half, 1), pltpu.roll(xr, half, 1))
        return xr * cos + partner * sin

    qb = jnp.dot(rms(cq_ref, gq_ref), wuq_ref[...], preferred_element_type=F32) * q_scale
    for h in range(MLA_HEADS):
        base = h * MLA_QK
        q_out[:, base:base + MLA_NOPE] = qb[:, base:base + MLA_NOPE].astype(BF16)
        q_out[:, base + MLA_NOPE:base + MLA_QK] = rope(qb[:, base + MLA_NOPE:base + MLA_QK]).astype(BF16)

    kvb = jnp.dot(rms(ckv_ref, gkv_ref), wukv_ref[...], preferred_element_type=F32)
    in_rope = lax.broadcasted_iota(jnp.int32, cos.shape, 1) < MLA_ROPE
    k_rope = rope(jnp.where(in_rope, kr_ref[:, 0:LANES].astype(F32), 0.0)).astype(BF16)
    for h in range(MLA_HEADS):
        base = h * MLA_QK
        k_out[:, base:base + MLA_NOPE] = kvb[:, h * MLA_NOPE:(h + 1) * MLA_NOPE].astype(BF16)
        k_out[:, base + MLA_NOPE:base + MLA_QK] = k_rope
    v_out[...] = kvb[:, MLA_W:2 * MLA_W].astype(BF16)


def _mla_prep(u, g_cq, g_ckv, wuq_p, wukv_p, cos_t, sin_t, *, seq, tm):
    t = u.shape[0]
    s_tiles = seq // tm
    const = lambda i: (0, 0)
    qk_shape = jax.ShapeDtypeStruct((t, MLA_HEADS * MLA_QK), BF16)
    est = (2 * tm * (MLA_Q_LORA + 2 * MLA_KV_LORA) * 2 + 2 * (wuq_p.size + wukv_p.size) * 2
           + 4 * tm * LANES * 4 + 2 * tm * 5 * MLA_W * 2 + 4 * tm * 2 * MLA_W * 4)
    return pl.pallas_call(
        functools.partial(_mla_prep_kernel, q_scale=(MLA_NOPE + MLA_ROPE) ** -0.5 * LOG2E),
        grid=(t // tm,),
        in_specs=[
            pl.BlockSpec((tm, MLA_Q_LORA), lambda i: (i, OFF_CQ // MLA_Q_LORA)),
            pl.BlockSpec((tm, MLA_KV_LORA), lambda i: (i, OFF_CKV // MLA_KV_LORA)),
            pl.BlockSpec((tm, KR_PAD), lambda i: (i, OFF_KR // KR_PAD)),
            pl.BlockSpec((1, MLA_Q_LORA), const),
            pl.BlockSpec((1, MLA_KV_LORA), const),
            pl.BlockSpec(wuq_p.shape, const),
            pl.BlockSpec(wukv_p.shape, const),
            pl.BlockSpec((tm, LANES), lambda i: (i % s_tiles, 0)),
            pl.BlockSpec((tm, LANES), lambda i: (i % s_tiles, 0)),
        ],
        out_specs=[
            pl.BlockSpec((tm, MLA_HEADS * MLA_QK), lambda i: (i, 0)),
            pl.BlockSpec((tm, MLA_HEADS * MLA_QK), lambda i: (i, 0)),
            pl.BlockSpec((tm, MLA_W), lambda i: (i, 0)),
        ],
        out_shape=[qk_shape, qk_shape, jax.ShapeDtypeStruct((t, MLA_W), BF16)],
        compiler_params=pltpu.CompilerParams(
            dimension_semantics=("arbitrary",), vmem_limit_bytes=_vmem_limit(est)),
        name="mla_prep",
    )(u, u, u, g_cq.reshape(1, -1), g_ckv.reshape(1, -1), wuq_p, wukv_p, cos_t, sin_t)


def _mla_attn_kernel(q_ref, k_ref, v_ref, z_ref, o_ref, vt_ref, t0, t1, mask_ref):
    seq = q_ref.shape[1]
    _build_vt(v_ref, vt_ref, seq)
    _write_causal_tile(mask_ref)

    def terms_of(i, q_tile):
        return [None] * i + [mask_ref], [0.0] * (i + 1), [None] * (i + 1)

    def emit(i, o):
        rows = slice(i * TQ, (i + 1) * TQ)
        o_ref[0, rows, :] = _silu_gate(o, z_ref[0, rows, :])

    _attention(seq // TQ, lambda i: q_ref[0, i * TQ:(i + 1) * TQ, :],
               lambda j: k_ref[0, j * TQ:(j + 1) * TQ, :], terms_of, vt_ref, (t0, t1), emit)


def _mla_attn(q3, k3, v3, u3):
    b, s, _ = q3.shape
    est = 4 * s * MLA_QK * 2 + 6 * s * MLA_V * 2 + VT_ROWS * s * 2 + 2 * s * TQ * 4 + TQ * TQ * 4
    return pl.pallas_call(
        _mla_attn_kernel,
        grid=(b, MLA_HEADS),
        in_specs=[
            pl.BlockSpec((1, s, MLA_QK), lambda bi, h: (bi, 0, h)),
            pl.BlockSpec((1, s, MLA_QK), lambda bi, h: (bi, 0, h)),
            pl.BlockSpec((1, s, MLA_V), lambda bi, h: (bi, 0, h)),
            pl.BlockSpec((1, s, MLA_V), lambda bi, h: (bi, 0, OFF_ZB // MLA_V + h)),
        ],
        out_specs=pl.BlockSpec((1, s, MLA_V), lambda bi, h: (bi, 0, h)),
        out_shape=jax.ShapeDtypeStruct((b, s, MLA_W), BF16),
        scratch_shapes=[pltpu.VMEM((VT_ROWS, s), BF16), pltpu.VMEM((s, TQ), F32), pltpu.VMEM((s, TQ), F32),
                        pltpu.VMEM((1, TQ, TQ), F32)],
        compiler_params=pltpu.CompilerParams(
            dimension_semantics=("arbitrary", "arbitrary"), vmem_limit_bytes=_vmem_limit(est)),
        name="mla_attn",
    )(q3, k3, v3, u3)


def _mem_attn_kernel(q_ref, k_ref, v_ref, z_ref, o_ref, vt_ref, t0, t1):
    seq = q_ref.shape[1]
    n_mem = k_ref.shape[1]
    assert n_mem == TQ
    _build_vt(v_ref, vt_ref, n_mem)

    def emit(i, o):
        rows = slice(i * TQ, (i + 1) * TQ)
        o_ref[0, rows, :] = _silu_gate(o, z_ref[0, rows, :])

    _attention(seq // TQ, lambda i: q_ref[0, i * TQ:(i + 1) * TQ, :], lambda j: k_ref[0],
               lambda i, q_tile: ([None], [0.0], [None]), vt_ref, (t0, t1), emit)


def _mem_attn(u3, kv3):
    b, s, _ = u3.shape
    m = kv3.shape[1]
    d = MEM_HEAD_DIM
    est = 6 * s * d * 2 + 4 * m * d * 2 + VT_ROWS * m * 2 + 2 * m * TQ * 4
    return pl.pallas_call(
        _mem_attn_kernel,
        grid=(b, MEM_HEADS),
        in_specs=[
            pl.BlockSpec((1, s, d), lambda bi, h: (bi, 0, OFF_QM // d + h)),
            pl.BlockSpec((1, m, d), lambda bi, h: (bi, 0, h)),
            pl.BlockSpec((1, m, d), lambda bi, h: (bi, 0, MEM_HEADS + h)),
            pl.BlockSpec((1, s, d), lambda bi, h: (bi, 0, OFF_ZM // d + h)),
        ],
        out_specs=pl.BlockSpec((1, s, d), lambda bi, h: (bi, 0, h)),
        out_shape=jax.ShapeDtypeStruct((b, s, MEM_W), BF16),
        scratch_shapes=[pltpu.VMEM((VT_ROWS, m), BF16), pltpu.VMEM((m, TQ), F32), pltpu.VMEM((m, TQ), F32)],
        compiler_params=pltpu.CompilerParams(
            dimension_semantics=("arbitrary", "arbitrary"), vmem_limit_bytes=_vmem_limit(est)),
        name="mem_attn",
    )(u3, kv3, kv3, u3)


def _merge_kernel(ga_ref, gb_ref, gm_ref, gl_ref, x_ref, wpa_ref, wpb_ref, wpm_ref, wout_ref, gf_ref,
                  o_ref, *, final_norm):
    d = x_ref.shape[1]

    def gated(idx, g_ref, w_ref):
        logit = gl_ref[:, idx * d:(idx + 1) * d].astype(F32)
        return jnp.dot(g_ref[...], w_ref[...], preferred_element_type=F32) / (1.0 + jnp.exp(-logit))

    y = gated(0, ga_ref, wpa_ref) + gated(1, gb_ref, wpb_ref) + gated(2, gm_ref, wpm_ref)
    r = x_ref[...] + jnp.dot(y.astype(BF16), wout_ref[...], preferred_element_type=F32)
    if final_norm:
        ms = jnp.mean(r * r, axis=-1, keepdims=True)
        r = r * lax.rsqrt(ms + EPS) * gf_ref[...]
    o_ref[...] = r


def _merge(ga, gb, gm, u, x, wpa, wpb, wpm, wout, g_final, *, tm, final_norm):
    t, d = x.shape
    const = lambda i: (0, 0)
    resident = lambda w: pl.BlockSpec(w.shape, const, pipeline_mode=pl.Buffered(1))
    rows = lambda width: pl.BlockSpec((tm, width), lambda i: (i, 0))
    w_bytes = (wpa.size + wpb.size + wpm.size + wout.size) * 2
    est = w_bytes + 2 * tm * (2 * MOBA_W + MEM_W + 3 * d) * 2 + 4 * tm * d * 4 + 6 * tm * d * 4
    return pl.pallas_call(
        functools.partial(_merge_kernel, final_norm=final_norm),
        grid=(t // tm,),
        in_specs=[
            rows(MOBA_W), rows(MLA_W), rows(MEM_W),
            pl.BlockSpec((tm, 3 * d), lambda i: (i, OFF_GL // (3 * d))),
            rows(d),
            resident(wpa), resident(wpb), resident(wpm), resident(wout),
            pl.BlockSpec((1, d), const),
        ],
        out_specs=rows(d),
        out_shape=jax.ShapeDtypeStruct((t, d), F32),
        compiler_params=pltpu.CompilerParams(
            dimension_semantics=("arbitrary",), vmem_limit_bytes=_vmem_limit(est)),
        name="merge",
    )(ga, gb, gm, u, x, wpa, wpb, wpm, wout, g_final.reshape(1, d))


def _w_in_tile_plan(n_cols):
    o_cq = 4 * MOBA_W
    o_zb = o_cq + MLA_Q_LORA + MLA_KV_LORA + MLA_ROPE
    o_qm = o_zb + MLA_W
    o_gl = o_qm + 2 * MEM_W
    srcs = ([o_gl + W_TILE * k for k in range(3 * D_MODEL // W_TILE)]
            + [W_TILE * k for k in range(4 * MOBA_W // W_TILE)] + [o_zb, o_cq, o_qm])
    assert len(srcs) * W_TILE == IN_WIDTH_P and o_gl + 3 * D_MODEL == n_cols
    full = n_cols // LANES * LANES
    plan = []
    for src in srcs:
        shift = src % LANES
        start = src - shift
        tail = start + W_WIN > full
        if tail:
            assert src + W_TILE == n_cols and n_cols - full == LANES // 2 and shift == LANES // 2
            start, shift = full - W_WIN, src - (full - W_WIN)
        assert start % LANES == 0 and 0 <= shift <= W_WIN - W_TILE + LANES // 2
        plan.append((start, shift, tail))
    return plan


def _regroup_kernel(start_ref, kind_ref, w_ref, tail_ref, o_ref, *, kinds, chunk):
    del start_ref
    kind = kind_ref[pl.program_id(1)]
    half = LANES // 2

    def convert(shift, tail):
        def body(r, carry):
            rows = pl.ds(pl.multiple_of(r * chunk, chunk), chunk)
            win = w_ref[rows, :]
            if shift:
                win = pltpu.roll(win, W_WIN - shift, 1)
            if tail:
                lane = lax.broadcasted_iota(jnp.int32, (chunk, LANES), 1)
                last = jnp.where(lane < half, win[:, W_TILE - LANES:W_TILE],
                                 pltpu.roll(tail_ref[rows, :], half, 1))
                o_ref[rows, 0:W_TILE - LANES] = win[:, 0:W_TILE - LANES].astype(BF16)
                o_ref[rows, W_TILE - LANES:W_TILE] = last.astype(BF16)
            else:
                o_ref[rows, :] = win[:, 0:W_TILE].astype(BF16)
            return carry
        lax.fori_loop(0, w_ref.shape[0] // chunk, body, 0)

    for idx, (shift, tail) in enumerate(kinds):
        pl.when(kind == idx)(functools.partial(convert, shift, tail))


def _regroup_w_in(w):
    d, n_cols = w.shape
    plan = _w_in_tile_plan(n_cols)
    kinds = sorted({(shift, tail) for _, shift, tail in plan})
    starts = jnp.asarray([start // LANES for start, _, _ in plan], jnp.int32)
    kind_ids = jnp.asarray([kinds.index((shift, tail)) for _, shift, tail in plan], jnp.int32)
    full = n_cols // LANES * LANES
    tail_cols = jnp.pad(w[:, full:], ((0, 0), (0, LANES - (n_cols - full))))
    tr = 1024
    est = 2 * tr * W_WIN * 4 + 2 * tr * LANES * 4 + 2 * tr * W_TILE * 2
    return pl.pallas_call(
        functools.partial(_regroup_kernel, kinds=kinds, chunk=64),
        grid_spec=pltpu.PrefetchScalarGridSpec(
            num_scalar_prefetch=2,
            grid=(d // tr, len(plan)),
            in_specs=[
                pl.BlockSpec((pl.Element(tr), pl.Element(W_WIN)), lambda r, j, st, kd: (r * tr, st[j] * LANES)),
                pl.BlockSpec((tr, LANES), lambda r, j, st, kd: (r, 0)),
            ],
            out_specs=pl.BlockSpec((tr, W_TILE), lambda r, j, st, kd: (r, j)),
        ),
        out_shape=jax.ShapeDtypeStruct((d, IN_WIDTH_P), BF16),
        compiler_params=pltpu.CompilerParams(
            dimension_semantics=("arbitrary", "arbitrary"), vmem_limit_bytes=_vmem_limit(est)),
        name="regroup_w_in",
    )(starts, kind_ids, w, tail_cols)


def _in_col_scale():
    cs = jnp.ones((1, IN_WIDTH_P), F32)
    cs = cs.at[:, OFF_QA:OFF_QA + MOBA_W].set(HEAD_DIM ** -0.5 * LOG2E)
    return cs.at[:, OFF_QM:OFF_QM + MEM_W].set(MEM_HEAD_DIM ** -0.5 * LOG2E)


def _regroup_w_uq(w):
    r = w.shape[0]
    w3 = w.reshape(r, MLA_HEADS, MLA_NOPE + MLA_ROPE)
    pad = jnp.zeros((r, MLA_HEADS, MLA_QK - MLA_NOPE - MLA_ROPE), w.dtype)
    return jnp.concatenate([w3, pad], axis=-1).reshape(r, MLA_HEADS * MLA_QK).astype(BF16)


def _regroup_w_ukv(w):
    r = w.shape[0]
    w3 = w.reshape(r, MLA_HEADS, MLA_NOPE + MLA_V)
    return jnp.concatenate([w3[:, :, :MLA_NOPE].reshape(r, MLA_W),
                            w3[:, :, MLA_NOPE:].reshape(r, MLA_W)], axis=1).astype(BF16)


def _rope_tables(seq):
    half = MLA_ROPE // 2
    inv = ROPE_THETA ** (-jnp.arange(half, dtype=F32) / half)
    ang = jnp.arange(seq, dtype=jnp.int32).astype(F32)[:, None] * inv[None, :]
    cos, sin = jnp.cos(ang), jnp.sin(ang)
    pad = LANES - MLA_ROPE
    cos_t = jnp.concatenate([cos, cos, jnp.ones((seq, pad), F32)], axis=1)
    sin_t = jnp.concatenate([-sin, sin, jnp.zeros((seq, pad), F32)], axis=1)
    return cos_t, sin_t


def kernel(x, mem, g_norm, w_in, g_cq, w_uq, g_ckv, w_ukv, g_mem, w_mem_kv, rel_bias,
           w_p_moba, w_p_mla, w_p_mem, w_out, g_final):
    b, s, d = x.shape
    m = mem.shape[1]
    depth = w_in.shape[0]
    t = b * s
    assert d == D_MODEL and s % TQ == 0 and m == TQ

    own, prev = _bias_tiles(rel_bias)
    cos_t, sin_t = _rope_tables(s)
    mem2 = mem.reshape(b * m, d)
    xs = x.reshape(t, d)
    in_scale = _in_col_scale()
    kv_scale = jnp.ones((1, 2 * MEM_W), F32)
    for l in range(depth):
        u = _norm_matmul(xs, g_norm[l], _regroup_w_in(w_in[l]), in_scale, tm=1024, tn=W_TILE)
        u3 = u.reshape(b, s, IN_WIDTH_P)
        ga = _moba_attn(u3, rel_bias, own, prev)
        q2, k2, v2 = _mla_prep(u, g_cq[l], g_ckv[l], _regroup_w_uq(w_uq[l]), _regroup_w_ukv(w_ukv[l]),
                               cos_t, sin_t, seq=s, tm=512)
        gb = _mla_attn(q2.reshape(b, s, -1), k2.reshape(b, s, -1), v2.reshape(b, s, -1), u3)
        kvm = _norm_matmul(mem2, g_mem[l], w_mem_kv[l].astype(BF16), kv_scale, tm=b * m, tn=2 * MEM_W)
        gm = _mem_attn(u3, kvm.reshape(b, m, 2 * MEM_W))
        xs = _merge(ga.reshape(t, MOBA_W), gb.reshape(t, MLA_W), gm.reshape(t, MEM_W), u, xs,
                    w_p_moba[l].astype(BF16), w_p_mla[l].astype(BF16), w_p_mem[l].astype(BF16),
                    w_out[l].astype(BF16), g_final, tm=256, final_norm=(l == depth - 1))
    return xs.reshape(b, s, d)
```

```python
import functools
import math

import jax
import jax.numpy as jnp
from jax import lax
from jax.experimental import pallas as pl
from jax.experimental.pallas import tpu as pltpu

D_MODEL = 2048
MOBA_HEADS = 8
HEAD_DIM = 128
MOBA_BLOCK = 256
MOBA_TOPK = 3
MLA_HEADS = 8
MLA_Q_LORA = 512
MLA_KV_LORA = 256
MLA_NOPE = 128
MLA_ROPE = 64
MLA_V = 128
ROPE_THETA = 10000.0
MEM_HEADS = 4
MEM_HEAD_DIM = 128
N_BUCKETS = 32
MAX_DISTANCE = 128
EPS = 1e-6

MOBA_W = MOBA_HEADS * HEAD_DIM
MLA_W = MLA_HEADS * MLA_V
MEM_W = MEM_HEADS * MEM_HEAD_DIM

LANES = 128
MXU_DIM = 256
V7X_VMEM_BYTES = 64 * 1024 * 1024

BF16 = jnp.bfloat16
F32 = jnp.float32
LOG2E = 1.4426950408889634

KR_PAD = MXU_DIM
OFF_GL = 0
OFF_QA = OFF_GL + 3 * D_MODEL
OFF_KA = OFF_QA + MOBA_W
OFF_VA = OFF_KA + MOBA_W
OFF_ZA = OFF_VA + MOBA_W
OFF_ZB = OFF_ZA + MOBA_W
OFF_CQ = OFF_ZB + MLA_W
OFF_CKV = OFF_CQ + MLA_Q_LORA
OFF_KR = OFF_CKV + MLA_KV_LORA
OFF_QM = OFF_KR + KR_PAD
OFF_ZM = OFF_QM + MEM_W
IN_WIDTH_P = OFF_ZM + MEM_W

TQ = MOBA_BLOCK
ONES_ROWS = 16
VT_ROWS = MLA_V + ONES_ROWS
MLA_QK = MXU_DIM

W_TILE = 1024
W_WIN = W_TILE + LANES


def _vmem_limit(nbytes):
    return int(min(nbytes + (8 << 20), V7X_VMEM_BYTES - (4 << 20)))


def _t5_thresholds():
    max_exact = N_BUCKETS // 2

    def bucket(d):
        if d < max_exact:
            return d
        large = max_exact + int(math.log(d / max_exact) / math.log(MAX_DISTANCE / max_exact)
                                * (N_BUCKETS - max_exact))
        return min(large, N_BUCKETS - 1)

    thr, d = [], 0
    for b in range(1, N_BUCKETS):
        while bucket(d) < b:
            d += 1
        thr.append(d)
    return tuple(thr)


T5_THRESHOLDS = _t5_thresholds()
assert T5_THRESHOLDS[-1] <= MOBA_BLOCK + 1


def _norm_matmul_kernel(x_ref, g_ref, w_ref, cs_ref, o_ref, h_ref, *, chunk):
    @pl.when(pl.program_id(1) == 0)
    def _():
        def body(r, carry):
            rows = pl.ds(pl.multiple_of(r * chunk, chunk), chunk)
            xv = x_ref[rows, :]
            ms = jnp.mean(xv * xv, axis=-1, keepdims=True)
            h_ref[rows, :] = (xv * lax.rsqrt(ms + EPS) * g_ref[...]).astype(BF16)
            return carry
        lax.fori_loop(0, x_ref.shape[0] // chunk, body, 0)

    acc = jnp.dot(h_ref[...], w_ref[...], preferred_element_type=F32)
    o_ref[...] = (acc * cs_ref[...]).astype(o_ref.dtype)


def _norm_matmul(x, g, w, col_scale, *, tm, tn):
    t, d = x.shape
    n = w.shape[1]
    est = 2 * tm * d * 4 + tm * d * 2 + 2 * d * tn * 2 + 2 * tm * tn * 2 + tm * tn * 4
    return pl.pallas_call(
        functools.partial(_norm_matmul_kernel, chunk=64),
        grid=(t // tm, n // tn),
        in_specs=[
            pl.BlockSpec((tm, d), lambda i, j: (i, 0)),
            pl.BlockSpec((1, d), lambda i, j: (0, 0)),
            pl.BlockSpec((d, tn), lambda i, j: (0, j)),
            pl.BlockSpec((1, tn), lambda i, j: (0, j)),
        ],
        out_specs=pl.BlockSpec((tm, tn), lambda i, j: (i, j)),
        out_shape=jax.ShapeDtypeStruct((t, n), BF16),
        scratch_shapes=[pltpu.VMEM((tm, d), BF16)],
        compiler_params=pltpu.CompilerParams(
            dimension_semantics=("arbitrary", "arbitrary"), vmem_limit_bytes=_vmem_limit(est)),
        name="norm_matmul",
    )(x, g.reshape(1, d), w, col_scale)


def _bias_tiles_kernel(rb_ref, own_ref, prev_ref):
    h = pl.program_id(0)
    key = lax.broadcasted_iota(jnp.int32, (TQ, TQ), 0)
    qry = lax.broadcasted_iota(jnp.int32, (TQ, TQ), 1)
    d_own = qry - key
    d_prev = d_own + MOBA_BLOCK

    def lookup(dist):
        val = jnp.zeros(dist.shape, F32) + rb_ref[0, h]
        for b in range(1, N_BUCKETS):
            val = jnp.where(dist >= T5_THRESHOLDS[b - 1], rb_ref[b, h], val)
        return val * LOG2E

    own_ref[0] = jnp.where(d_own >= 0, lookup(d_own), -jnp.inf)
    prev_ref[0] = lookup(d_prev)


def _bias_tiles(rel_bias):
    heads = rel_bias.shape[1]
    tile = jax.ShapeDtypeStruct((heads, TQ, TQ), F32)
    spec = pl.BlockSpec((1, TQ, TQ), lambda h: (h, 0, 0))
    return pl.pallas_call(
        _bias_tiles_kernel,
        grid=(heads,),
        in_specs=[pl.BlockSpec(memory_space=pltpu.SMEM)],
        out_specs=[spec, spec],
        out_shape=[tile, tile],
        name="bias_tiles",
    )(rel_bias)


def _build_vt_block(v_ref, vt_ref, j):
    dv = v_ref.shape[-1]
    blk = slice(j * TQ, (j + 1) * TQ)
    vt_ref[0:dv, blk] = v_ref[blk, :].astype(F32).T.astype(BF16)
    row = lax.broadcasted_iota(jnp.int32, (ONES_ROWS, TQ), 0)
    vt_ref[dv:dv + ONES_ROWS, blk] = jnp.where(row == 0, 1.0, 0.0).astype(BF16)


def _attention(n_tiles, q_tile_of, k_block_of, terms_of, prepare, vt_ref, t_bufs, emit):
    nt = (((1,), (1,)), ((), ()))
    dv = vt_ref.shape[0] - ONES_ROWS
    state = {}

    def stage1(i):
        prepare(i)
        q = q_tile_of(i)
        adds, consts, sels = terms_of(i, q)
        st = state[i] = dict(consts=consts, sels=sels, m=None, acc=None)
        t_buf = t_bufs[i % 2]

        def item(j):
            blk = slice(j * TQ, (j + 1) * TQ)
            t = lax.dot_general(k_block_of(j), q, nt, preferred_element_type=F32)
            if adds[j] is not None:
                t = t + adds[j][...]
            t_buf[blk, :] = t
            mj = jnp.max(t, axis=0, keepdims=True) + consts[j]
            if sels[j] is not None:
                mj = jnp.where(sels[j], mj, -jnp.inf)
            st["m"] = mj if st["m"] is None else jnp.maximum(st["m"], mj)
        return [functools.partial(item, j) for j in range(len(consts))]

    def stage2(i):
        st = state.pop(i)
        t_buf = t_bufs[i % 2]

        def item(j):
            blk = slice(j * TQ, (j + 1) * TQ)
            off = st["m"] - st["consts"][j]
            if st["sels"][j] is not None:
                off = jnp.where(st["sels"][j], off, jnp.inf)
            p = jnp.exp2(t_buf[blk, :] - off).astype(BF16)
            part = jnp.dot(vt_ref[:, blk], p, preferred_element_type=F32)
            st["acc"] = part if st["acc"] is None else st["acc"] + part

        def finish():
            acc = st["acc"]
            emit(i, (acc[0:dv, :] / acc[dv:dv + 1, :]).T)
        return [functools.partial(item, j) for j in range(len(st["consts"]))], finish

    for item in stage1(0):
        item()
        yield
    for i in range(n_tiles):
        ahead = stage1(i + 1) if i + 1 < n_tiles else []
        behind, finish = stage2(i)
        for k in range(max(len(ahead), len(behind))):
            if k < len(ahead):
                ahead[k]()
            if k < len(behind):
                behind[k]()
            yield
        finish()


def _run_streams(streams):
    active = list(streams)
    while active:
        for stream in list(active):
            if next(stream, StopIteration) is StopIteration:
                active.remove(stream)


def _silu_gate(o, z):
    zf = z.astype(F32)
    return (o * (zf / (1.0 + jnp.exp(-zf)))).astype(BF16)


def _write_causal_tile(mask_ref):
    key = lax.broadcasted_iota(jnp.int32, (TQ, TQ), 0)
    qry = lax.broadcasted_iota(jnp.int32, (TQ, TQ), 1)
    mask_ref[...] = jnp.where(key <= qry, 0.0, -jnp.inf).astype(F32)


def _head_view(ref, head, width):
    return ref.at[0, :, head * width:(head + 1) * width]


def _moba_kernel(rb_ref, q_ref, k_ref, v_ref, z_ref, own_ref, prev_ref, o_ref, *scratch, heads):
    per_head = len(scratch) // heads
    _run_streams([
        _moba_head(rb_ref, pl.program_id(1) * heads + s, _head_view(q_ref, s, HEAD_DIM),
                   _head_view(k_ref, s, HEAD_DIM), _head_view(v_ref, s, HEAD_DIM),
                   _head_view(z_ref, s, HEAD_DIM), own_ref.at[s], prev_ref.at[s],
                   _head_view(o_ref, s, HEAD_DIM), *scratch[s * per_head:(s + 1) * per_head])
        for s in range(heads)])


def _moba_head(rb_ref, head, q_ref, k_ref, v_ref, z_ref, own_ref, prev_ref, o_ref, vt_ref, t0, t1, km_ref):
    seq = q_ref.shape[0]
    n_tiles = seq // TQ
    far_const = rb_ref[N_BUCKETS - 1, head] * LOG2E
    km_ref[...] = jnp.zeros(km_ref.shape, F32)

    def prepare(i):
        _build_vt_block(v_ref, vt_ref, i)
        k_blk = k_ref[i * TQ:(i + 1) * TQ, :].astype(F32)
        km_ref[i:i + 1, :] = jnp.sum(k_blk, axis=0, keepdims=True) * (1.0 / MOBA_BLOCK)

    nt = (((1,), (1,)), ((), ()))

    def terms_of(i, q_tile):
        if i > MOBA_TOPK:
            k_mean = km_ref[...]
            km1 = k_mean.astype(BF16)
            rem = k_mean - km1.astype(F32)
            km2 = rem.astype(BF16)
            km3 = (rem - km2.astype(F32)).astype(BF16)
            gate = (lax.dot_general(km1, q_tile, nt, preferred_element_type=F32)
                    + lax.dot_general(km2, q_tile, nt, preferred_element_type=F32)
                    + lax.dot_general(km3, q_tile, nt, preferred_element_type=F32))
            sels = []
            for j in range(i):
                gj = gate[j:j + 1, :]
                cnt = jnp.zeros(gj.shape, F32)
                for jp in range(i):
                    if jp == j:
                        continue
                    gp = gate[jp:jp + 1, :]
                    beats = (gp >= gj) if jp < j else (gp > gj)
                    cnt = cnt + jnp.where(beats, 1.0, 0.0)
                sels.append(cnt < MOBA_TOPK)
        else:
            sels = [None] * i
        sels.append(None)
        adds = [None] * (i + 1)
        consts = [far_const] * (i + 1)
        adds[i], consts[i] = own_ref, 0.0
        if i >= 1:
            adds[i - 1], consts[i - 1] = prev_ref, 0.0
        return adds, consts, sels

    def emit(i, o):
        rows = slice(i * TQ, (i + 1) * TQ)
        o_ref[rows, :] = _silu_gate(o, z_ref[rows, :])

    return _attention(n_tiles, lambda i: q_ref[i * TQ:(i + 1) * TQ, :],
                      lambda j: k_ref[j * TQ:(j + 1) * TQ, :], terms_of, prepare, vt_ref, (t0, t1), emit)


HEADS_PER_STEP = 2


def _attn_scratch(n_keys, extra=()):
    per_head = [pltpu.VMEM((VT_ROWS, n_keys), BF16), pltpu.VMEM((n_keys, TQ), F32),
                pltpu.VMEM((n_keys, TQ), F32), *extra]
    return per_head * HEADS_PER_STEP


def _moba_attn(u3, rel_bias, own, prev):
    b, s, _ = u3.shape
    hp = HEADS_PER_STEP
    width = hp * HEAD_DIM
    col = lambda off: (lambda bi, h: (bi, 0, off // width + h))
    blk = (1, s, width)
    tile_spec = pl.BlockSpec((hp, TQ, TQ), lambda bi, h: (h, 0, 0))
    est = 10 * s * width * 2 + 4 * hp * TQ * TQ * 4 + hp * (VT_ROWS * s * 2 + 2 * s * TQ * 4)
    return pl.pallas_call(
        functools.partial(_moba_kernel, heads=hp),
        grid=(b, MOBA_HEADS // hp),
        in_specs=[
            pl.BlockSpec(memory_space=pltpu.SMEM),
            pl.BlockSpec(blk, col(OFF_QA)),
            pl.BlockSpec(blk, col(OFF_KA)),
            pl.BlockSpec(blk, col(OFF_VA)),
            pl.BlockSpec(blk, col(OFF_ZA)),
            tile_spec, tile_spec,
        ],
        out_specs=pl.BlockSpec(blk, lambda bi, h: (bi, 0, h)),
        out_shape=jax.ShapeDtypeStruct((b, s, MOBA_W), BF16),
        scratch_shapes=_attn_scratch(s, extra=(pltpu.VMEM((16, HEAD_DIM), F32),)),
        compiler_params=pltpu.CompilerParams(
            dimension_semantics=("arbitrary", "arbitrary"), vmem_limit_bytes=_vmem_limit(est)),
        name="moba_attn",
    )(rel_bias, u3, u3, u3, u3, own, prev)


def _mla_prep_kernel(cq_ref, ckv_ref, kr_ref, gq_ref, gkv_ref, wuq_ref, wukv_ref, cos_ref, sin_ref,
                     q_out, k_out, v_out, *, q_scale):
    def rms(x_ref, g_ref):
        xf = x_ref[...].astype(F32)
        ms = jnp.mean(xf * xf, axis=-1, keepdims=True)
        return (xf * lax.rsqrt(ms + EPS) * g_ref[...]).astype(BF16)

    cos = cos_ref[...]
    sin = sin_ref[...]
    half = MLA_ROPE // 2
    first_half = lax.broadcasted_iota(jnp.int32, cos.shape, 1) < half

    def rope(xr):
        partner = jnp.where(first_half, pltpu.roll(xr, LANES - half, 1), pltpu.roll(xr, half, 1))
        return xr * cos + partner * sin

    qb = jnp.dot(rms(cq_ref, gq_ref), wuq_ref[...], preferred_element_type=F32) * q_scale
    for h in range(MLA_HEADS):
        base = h * MLA_QK
        q_out[:, base:base + MLA_NOPE] = qb[:, base:base + MLA_NOPE].astype(BF16)
        q_out[:, base + MLA_NOPE:base + MLA_QK] = rope(qb[:, base + MLA_NOPE:base + MLA_QK]).astype(BF16)

    kvb = jnp.dot(rms(ckv_ref, gkv_ref), wukv_ref[...], preferred_element_type=F32)
    in_rope = lax.broadcasted_iota(jnp.int32, cos.shape, 1) < MLA_ROPE
    k_rope = rope(jnp.where(in_rope, kr_ref[:, 0:LANES].astype(F32), 0.0)).astype(BF16)
    for h in range(MLA_HEADS):
        base = h * MLA_QK
        k_out[:, base:base + MLA_NOPE] = kvb[:, h * MLA_NOPE:(h + 1) * MLA_NOPE].astype(BF16)
        k_out[:, base + MLA_NOPE:base + MLA_QK] = k_rope
    v_out[...] = kvb[:, MLA_W:2 * MLA_W].astype(BF16)


def _mla_prep(u, g_cq, g_ckv, wuq_p, wukv_p, cos_t, sin_t, *, seq, tm):
    t = u.shape[0]
    s_tiles = seq // tm
    const = lambda i: (0, 0)
    qk_shape = jax.ShapeDtypeStruct((t, MLA_HEADS * MLA_QK), BF16)
    est = (2 * tm * (MLA_Q_LORA + 2 * MLA_KV_LORA) * 2 + 2 * (wuq_p.size + wukv_p.size) * 2
           + 4 * tm * LANES * 4 + 2 * tm * 5 * MLA_W * 2 + 4 * tm * 2 * MLA_W * 4)
    return pl.pallas_call(
        functools.partial(_mla_prep_kernel, q_scale=(MLA_NOPE + MLA_ROPE) ** -0.5 * LOG2E),
        grid=(t // tm,),
        in_specs=[
            pl.BlockSpec((tm, MLA_Q_LORA), lambda i: (i, OFF_CQ // MLA_Q_LORA)),
            pl.BlockSpec((tm, MLA_KV_LORA), lambda i: (i, OFF_CKV // MLA_KV_LORA)),
            pl.BlockSpec((tm, KR_PAD), lambda i: (i, OFF_KR // KR_PAD)),
            pl.BlockSpec((1, MLA_Q_LORA), const),
            pl.BlockSpec((1, MLA_KV_LORA), const),
            pl.BlockSpec(wuq_p.shape, const),
            pl.BlockSpec(wukv_p.shape, const),
            pl.BlockSpec((tm, LANES), lambda i: (i % s_tiles, 0)),
            pl.BlockSpec((tm, LANES), lambda i: (i % s_tiles, 0)),
        ],
        out_specs=[
            pl.BlockSpec((tm, MLA_HEADS * MLA_QK), lambda i: (i, 0)),
            pl.BlockSpec((tm, MLA_HEADS * MLA_QK), lambda i: (i, 0)),
            pl.BlockSpec((tm, MLA_W), lambda i: (i, 0)),
        ],
        out_shape=[qk_shape, qk_shape, jax.ShapeDtypeStruct((t, MLA_W), BF16)],
        compiler_params=pltpu.CompilerParams(
            dimension_semantics=("arbitrary",), vmem_limit_bytes=_vmem_limit(est)),
        name="mla_prep",
    )(u, u, u, g_cq.reshape(1, -1), g_ckv.reshape(1, -1), wuq_p, wukv_p, cos_t, sin_t)


def _mla_attn_kernel(q_ref, k_ref, v_ref, z_ref, o_ref, mask_ref, *scratch, heads):
    _write_causal_tile(mask_ref)
    per_head = len(scratch) // heads
    _run_streams([
        _mla_head(_head_view(q_ref, s, MLA_QK), _head_view(k_ref, s, MLA_QK), _head_view(v_ref, s, MLA_V),
                  _head_view(z_ref, s, MLA_V), _head_view(o_ref, s, MLA_V), mask_ref,
                  *scratch[s * per_head:(s + 1) * per_head])
        for s in range(heads)])


def _mla_head(q_ref, k_ref, v_ref, z_ref, o_ref, mask_ref, vt_ref, t0, t1):
    seq = q_ref.shape[0]

    def terms_of(i, q_tile):
        return [None] * i + [mask_ref], [0.0] * (i + 1), [None] * (i + 1)

    def emit(i, o):
        rows = slice(i * TQ, (i + 1) * TQ)
        o_ref[rows, :] = _silu_gate(o, z_ref[rows, :])

    return _attention(seq // TQ, lambda i: q_ref[i * TQ:(i + 1) * TQ, :],
                      lambda j: k_ref[j * TQ:(j + 1) * TQ, :], terms_of,
                      functools.partial(_build_vt_block, v_ref, vt_ref), vt_ref, (t0, t1), emit)


def _mla_attn(q3, k3, v3, u3):
    b, s, _ = q3.shape
    hp = HEADS_PER_STEP
    est = (4 * s * hp * MLA_QK * 2 + 6 * s * hp * MLA_V * 2 + hp * (VT_ROWS * s * 2 + 2 * s * TQ * 4)
           + TQ * TQ * 4)
    return pl.pallas_call(
        functools.partial(_mla_attn_kernel, heads=hp),
        grid=(b, MLA_HEADS // hp),
        in_specs=[
            pl.BlockSpec((1, s, hp * MLA_QK), lambda bi, h: (bi, 0, h)),
            pl.BlockSpec((1, s, hp * MLA_QK), lambda bi, h: (bi, 0, h)),
            pl.BlockSpec((1, s, hp * MLA_V), lambda bi, h: (bi, 0, h)),
            pl.BlockSpec((1, s, hp * MLA_V), lambda bi, h: (bi, 0, OFF_ZB // (hp * MLA_V) + h)),
        ],
        out_specs=pl.BlockSpec((1, s, hp * MLA_V), lambda bi, h: (bi, 0, h)),
        out_shape=jax.ShapeDtypeStruct((b, s, MLA_W), BF16),
        scratch_shapes=[pltpu.VMEM((TQ, TQ), F32)] + _attn_scratch(s),
        compiler_params=pltpu.CompilerParams(
            dimension_semantics=("arbitrary", "arbitrary"), vmem_limit_bytes=_vmem_limit(est)),
        name="mla_attn",
    )(q3, k3, v3, u3)


def _mem_attn_kernel(q_ref, k_ref, v_ref, z_ref, o_ref, *scratch, heads):
    assert k_ref.shape[1] == TQ
    per_head = len(scratch) // heads
    d = MEM_HEAD_DIM
    _run_streams([
        _mem_head(_head_view(q_ref, s, d), _head_view(k_ref, s, d), _head_view(v_ref, s, d),
                  _head_view(z_ref, s, d), _head_view(o_ref, s, d), *scratch[s * per_head:(s + 1) * per_head])
        for s in range(heads)])


def _mem_head(q_ref, k_ref, v_ref, z_ref, o_ref, vt_ref, t0, t1):
    seq = q_ref.shape[0]

    def prepare(i):
        if i == 0:
            _build_vt_block(v_ref, vt_ref, 0)

    def emit(i, o):
        rows = slice(i * TQ, (i + 1) * TQ)
        o_ref[rows, :] = _silu_gate(o, z_ref[rows, :])

    return _attention(seq // TQ, lambda i: q_ref[i * TQ:(i + 1) * TQ, :], lambda j: k_ref[...],
                      lambda i, q_tile: ([None], [0.0], [None]), prepare, vt_ref, (t0, t1), emit)


def _mem_attn(u3, kv3):
    b, s, _ = u3.shape
    m = kv3.shape[1]
    hp = HEADS_PER_STEP
    d = hp * MEM_HEAD_DIM
    est = 6 * s * d * 2 + 4 * m * d * 2 + hp * (VT_ROWS * m * 2 + 2 * m * TQ * 4)
    return pl.pallas_call(
        functools.partial(_mem_attn_kernel, heads=hp),
        grid=(b, MEM_HEADS // hp),
        in_specs=[
            pl.BlockSpec((1, s, d), lambda bi, h: (bi, 0, OFF_QM // d + h)),
            pl.BlockSpec((1, m, d), lambda bi, h: (bi, 0, h)),
            pl.BlockSpec((1, m, d), lambda bi, h: (bi, 0, MEM_W // d + h)),
            pl.BlockSpec((1, s, d), lambda bi, h: (bi, 0, OFF_ZM // d + h)),
        ],
        out_specs=pl.BlockSpec((1, s, d), lambda bi, h: (bi, 0, h)),
        out_shape=jax.ShapeDtypeStruct((b, s, MEM_W), BF16),
        scratch_shapes=_attn_scratch(m),
        compiler_params=pltpu.CompilerParams(
            dimension_semantics=("arbitrary", "arbitrary"), vmem_limit_bytes=_vmem_limit(est)),
        name="mem_attn",
    )(u3, kv3, kv3, u3)


def _merge_kernel(ga_ref, gb_ref, gm_ref, gl_ref, x_ref, wpa_ref, wpb_ref, wpm_ref, wout_ref, gf_ref,
                  o_ref, *, final_norm):
    d = x_ref.shape[1]

    def gated(idx, g_ref, w_ref):
        logit = gl_ref[:, idx * d:(idx + 1) * d].astype(F32)
        return jnp.dot(g_ref[...], w_ref[...], preferred_element_type=F32) / (1.0 + jnp.exp(-logit))

    y = gated(0, ga_ref, wpa_ref) + gated(1, gb_ref, wpb_ref) + gated(2, gm_ref, wpm_ref)
    r = x_ref[...] + jnp.dot(y.astype(BF16), wout_ref[...], preferred_element_type=F32)
    if final_norm:
        ms = jnp.mean(r * r, axis=-1, keepdims=True)
        r = r * lax.rsqrt(ms + EPS) * gf_ref[...]
    o_ref[...] = r


def _merge(ga, gb, gm, u, x, wpa, wpb, wpm, wout, g_final, *, tm, final_norm):
    t, d = x.shape
    const = lambda i: (0, 0)
    resident = lambda w: pl.BlockSpec(w.shape, const, pipeline_mode=pl.Buffered(1))
    rows = lambda width: pl.BlockSpec((tm, width), lambda i: (i, 0))
    w_bytes = (wpa.size + wpb.size + wpm.size + wout.size) * 2
    est = w_bytes + 2 * tm * (2 * MOBA_W + MEM_W + 3 * d) * 2 + 4 * tm * d * 4 + 6 * tm * d * 4
    return pl.pallas_call(
        functools.partial(_merge_kernel, final_norm=final_norm),
        grid=(t // tm,),
        in_specs=[
            rows(MOBA_W), rows(MLA_W), rows(MEM_W),
            pl.BlockSpec((tm, 3 * d), lambda i: (i, OFF_GL // (3 * d))),
            rows(d),
            resident(wpa), resident(wpb), resident(wpm), resident(wout),
            pl.BlockSpec((1, d), const),
        ],
        out_specs=rows(d),
        out_shape=jax.ShapeDtypeStruct((t, d), F32),
        compiler_params=pltpu.CompilerParams(
            dimension_semantics=("arbitrary",), vmem_limit_bytes=_vmem_limit(est)),
        name="merge",
    )(ga, gb, gm, u, x, wpa, wpb, wpm, wout, g_final.reshape(1, d))


def _w_in_tile_plan(n_cols):
    o_cq = 4 * MOBA_W
    o_zb = o_cq + MLA_Q_LORA + MLA_KV_LORA + MLA_ROPE
    o_qm = o_zb + MLA_W
    o_gl = o_qm + 2 * MEM_W
    srcs = ([o_gl + W_TILE * k for k in range(3 * D_MODEL // W_TILE)]
            + [W_TILE * k for k in range(4 * MOBA_W // W_TILE)] + [o_zb, o_cq, o_qm])
    assert len(srcs) * W_TILE == IN_WIDTH_P and o_gl + 3 * D_MODEL == n_cols
    full = n_cols // LANES * LANES
    plan = []
    for src in srcs:
        shift = src % LANES
        start = src - shift
        tail = start + W_WIN > full
        if tail:
            assert src + W_TILE == n_cols and n_cols - full == LANES // 2 and shift == LANES // 2
            start, shift = full - W_WIN, src - (full - W_WIN)
        assert start % LANES == 0 and 0 <= shift <= W_WIN - W_TILE + LANES // 2
        plan.append((start, shift, tail))
    return plan


def _regroup_kernel(start_ref, kind_ref, w_ref, tail_ref, o_ref, *, kinds, chunk):
    del start_ref
    kind = kind_ref[pl.program_id(1)]
    half = LANES // 2

    def convert(shift, tail):
        def body(r, carry):
            rows = pl.ds(pl.multiple_of(r * chunk, chunk), chunk)
            win = w_ref[rows, :]
            if shift:
                win = pltpu.roll(win, W_WIN - shift, 1)
            if tail:
                lane = lax.broadcasted_iota(jnp.int32, (chunk, LANES), 1)
                last = jnp.where(lane < half, win[:, W_TILE - LANES:W_TILE],
                                 pltpu.roll(tail_ref[rows, :], half, 1))
                o_ref[rows, 0:W_TILE - LANES] = win[:, 0:W_TILE - LANES].astype(BF16)
                o_ref[rows, W_TILE - LANES:W_TILE] = last.astype(BF16)
            else:
                o_ref[rows, :] = win[:, 0:W_TILE].astype(BF16)
            return carry
        lax.fori_loop(0, w_ref.shape[0] // chunk, body, 0)

    for idx, (shift, tail) in enumerate(kinds):
        pl.when(kind == idx)(functools.partial(convert, shift, tail))


def _regroup_w_in(w_all, layer):
    _, d, n_cols = w_all.shape
    plan = _w_in_tile_plan(n_cols)
    kinds = sorted({(shift, tail) for _, shift, tail in plan})
    starts = jnp.asarray([start // LANES for start, _, _ in plan], jnp.int32)
    kind_ids = jnp.asarray([kinds.index((shift, tail)) for _, shift, tail in plan], jnp.int32)
    tr = 1024
    est = 2 * tr * W_WIN * 4 + 2 * tr * LANES * 4 + 2 * tr * W_TILE * 2
    return pl.pallas_call(
        functools.partial(_regroup_kernel, kinds=kinds, chunk=64),
        grid_spec=pltpu.PrefetchScalarGridSpec(
            num_scalar_prefetch=2,
            grid=(d // tr, len(plan)),
            in_specs=[
                pl.BlockSpec((pl.Squeezed(), pl.Element(tr), pl.Element(W_WIN)),
                             lambda r, j, st, kd: (layer, r * tr, st[j] * LANES)),
                pl.BlockSpec((None, tr, LANES), lambda r, j, st, kd: (layer, r, n_cols // LANES)),
            ],
            out_specs=pl.BlockSpec((tr, W_TILE), lambda r, j, st, kd: (r, j)),
        ),
        out_shape=jax.ShapeDtypeStruct((d, IN_WIDTH_P), BF16),
        compiler_params=pltpu.CompilerParams(
            dimension_semantics=("arbitrary", "arbitrary"), vmem_limit_bytes=_vmem_limit(est)),
        name="regroup_w_in",
    )(starts, kind_ids, w_all, w_all)


def _in_col_scale():
    cs = jnp.ones((1, IN_WIDTH_P), F32)
    cs = cs.at[:, OFF_QA:OFF_QA + MOBA_W].set(HEAD_DIM ** -0.5 * LOG2E)
    return cs.at[:, OFF_QM:OFF_QM + MEM_W].set(MEM_HEAD_DIM ** -0.5 * LOG2E)


def _regroup_w_uq(w):
    r = w.shape[0]
    w3 = w.reshape(r, MLA_HEADS, MLA_NOPE + MLA_ROPE)
    pad = jnp.zeros((r, MLA_HEADS, MLA_QK - MLA_NOPE - MLA_ROPE), w.dtype)
    return jnp.concatenate([w3, pad], axis=-1).reshape(r, MLA_HEADS * MLA_QK).astype(BF16)


def _regroup_w_ukv(w):
    r = w.shape[0]
    w3 = w.reshape(r, MLA_HEADS, MLA_NOPE + MLA_V)
    return jnp.concatenate([w3[:, :, :MLA_NOPE].reshape(r, MLA_W),
                            w3[:, :, MLA_NOPE:].reshape(r, MLA_W)], axis=1).astype(BF16)


def _rope_tables(seq):
    half = MLA_ROPE // 2
    inv = ROPE_THETA ** (-jnp.arange(half, dtype=F32) / half)
    ang = jnp.arange(seq, dtype=jnp.int32).astype(F32)[:, None] * inv[None, :]
    cos, sin = jnp.cos(ang), jnp.sin(ang)
    pad = LANES - MLA_ROPE
    cos_t = jnp.concatenate([cos, cos, jnp.ones((seq, pad), F32)], axis=1)
    sin_t = jnp.concatenate([-sin, sin, jnp.zeros((seq, pad), F32)], axis=1)
    return cos_t, sin_t


def kernel(x, mem, g_norm, w_in, g_cq, w_uq, g_ckv, w_ukv, g_mem, w_mem_kv, rel_bias,
           w_p_moba, w_p_mla, w_p_mem, w_out, g_final):
    b, s, d = x.shape
    m = mem.shape[1]
    depth = w_in.shape[0]
    t = b * s
    assert d == D_MODEL and s % TQ == 0 and m == TQ

    own, prev = _bias_tiles(rel_bias)
    cos_t, sin_t = _rope_tables(s)
    mem2 = mem.reshape(b * m, d)
    xs = x.reshape(t, d)
    in_scale = _in_col_scale()
    kv_scale = jnp.ones((1, 2 * MEM_W), F32)
    for l in range(depth):
        u = _norm_matmul(xs, g_norm[l], _regroup_w_in(w_in, l), in_scale, tm=1024, tn=W_TILE)
        u3 = u.reshape(b, s, IN_WIDTH_P)
        ga = _moba_attn(u3, rel_bias, own, prev)
        q2, k2, v2 = _mla_prep(u, g_cq[l], g_ckv[l], _regroup_w_uq(w_uq[l]), _regroup_w_ukv(w_ukv[l]),
                               cos_t, sin_t, seq=s, tm=512)
        gb = _mla_attn(q2.reshape(b, s, -1), k2.reshape(b, s, -1), v2.reshape(b, s, -1), u3)
        kvm = _norm_matmul(mem2, g_mem[l], w_mem_kv[l].astype(BF16), kv_scale, tm=b * m, tn=2 * MEM_W)
        gm = _mem_attn(u3, kvm.reshape(b, m, 2 * MEM_W))
        xs = _merge(ga.reshape(t, MOBA_W), gb.reshape(t, MLA_W), gm.reshape(t, MEM_W), u, xs,
                    w_p_moba[l].astype(BF16), w_p_mla[l].astype(BF16), w_p_mem[l].astype(BF16),
                    w_out[l].astype(BF16), g_final, tm=256, final_norm=(l == depth - 1))
    return xs.reshape(b, s, d)
```

```python
import functools
import math

import jax
import jax.numpy as jnp
from jax import lax
from jax.experimental import pallas as pl
from jax.experimental.pallas import tpu as pltpu

D_MODEL = 2048
MOBA_HEADS = 8
HEAD_DIM = 128
MOBA_BLOCK = 256
MOBA_TOPK = 3
MLA_HEADS = 8
MLA_Q_LORA = 512
MLA_KV_LORA = 256
MLA_NOPE = 128
MLA_ROPE = 64
MLA_V = 128
ROPE_THETA = 10000.0
MEM_HEADS = 4
MEM_HEAD_DIM = 128
N_BUCKETS = 32
MAX_DISTANCE = 128
EPS = 1e-6

MOBA_W = MOBA_HEADS * HEAD_DIM
MLA_W = MLA_HEADS * MLA_V
MEM_W = MEM_HEADS * MEM_HEAD_DIM

LANES = 128
MXU_DIM = 256
V7X_VMEM_BYTES = 64 * 1024 * 1024

BF16 = jnp.bfloat16
F32 = jnp.float32
LOG2E = 1.4426950408889634

KR_PAD = MXU_DIM
OFF_GL = 0
OFF_QA = OFF_GL + 3 * D_MODEL
OFF_KA = OFF_QA + MOBA_W
OFF_VA = OFF_KA + MOBA_W
OFF_ZA = OFF_VA + MOBA_W
OFF_ZB = OFF_ZA + MOBA_W
OFF_CQ = OFF_ZB + MLA_W
OFF_CKV = OFF_CQ + MLA_Q_LORA
OFF_KR = OFF_CKV + MLA_KV_LORA
OFF_QM = OFF_KR + KR_PAD
OFF_ZM = OFF_QM + MEM_W
IN_WIDTH_P = OFF_ZM + MEM_W

TQ = MOBA_BLOCK
ONES_ROWS = 16
VT_ROWS = MLA_V + ONES_ROWS
MLA_QK = MXU_DIM

W_TILE = 1024


def _vmem_limit(nbytes):
    return int(min(nbytes + (8 << 20), V7X_VMEM_BYTES - (4 << 20)))


def _t5_thresholds():
    max_exact = N_BUCKETS // 2

    def bucket(d):
        if d < max_exact:
            return d
        large = max_exact + int(math.log(d / max_exact) / math.log(MAX_DISTANCE / max_exact)
                                * (N_BUCKETS - max_exact))
        return min(large, N_BUCKETS - 1)

    thr, d = [], 0
    for b in range(1, N_BUCKETS):
        while bucket(d) < b:
            d += 1
        thr.append(d)
    return tuple(thr)


T5_THRESHOLDS = _t5_thresholds()
assert T5_THRESHOLDS[-1] <= MOBA_BLOCK + 1


def _norm_rows(x_ref, g_ref, h_ref, chunk):
    def body(r, carry):
        rows = pl.ds(pl.multiple_of(r * chunk, chunk), chunk)
        xv = x_ref[rows, :]
        ms = jnp.mean(xv * xv, axis=-1, keepdims=True)
        h_ref[rows, :] = (xv * lax.rsqrt(ms + EPS) * g_ref[...]).astype(BF16)
        return carry
    lax.fori_loop(0, x_ref.shape[0] // chunk, body, 0)


def _norm_matmul_kernel(x_ref, g_ref, w_ref, cs_ref, o_ref, h_ref, *, chunk):
    pl.when(pl.program_id(1) == 0)(functools.partial(_norm_rows, x_ref, g_ref, h_ref, chunk))
    acc = jnp.dot(h_ref[...], w_ref[...], preferred_element_type=F32)
    o_ref[...] = (acc * cs_ref[...]).astype(o_ref.dtype)


def _in_proj_kernel(src_ref, x_ref, g_ref, wt_ref, cs_ref, o_ref, h_ref, *, chunk):
    del src_ref
    pl.when(pl.program_id(1) == 0)(functools.partial(_norm_rows, x_ref, g_ref, h_ref, chunk))
    acc = lax.dot_general(h_ref[...], wt_ref[...].astype(BF16), (((1,), (1,)), ((), ())),
                          preferred_element_type=F32)
    o_ref[...] = (acc * cs_ref[...]).astype(o_ref.dtype)


def _in_proj(x, g, wt_all, layer, col_scale, *, tm):
    t, d = x.shape
    srcs = _w_in_tile_sources(wt_all.shape[1])
    unit = 64
    assert all(src % unit == 0 for src in srcs)
    src_units = jnp.asarray([src // unit for src in srcs], jnp.int32)
    est = 2 * tm * d * 4 + tm * d * 2 + 2 * W_TILE * d * 4 + W_TILE * d * 2 + 2 * tm * W_TILE * 2 + tm * W_TILE * 4
    return pl.pallas_call(
        functools.partial(_in_proj_kernel, chunk=64),
        grid_spec=pltpu.PrefetchScalarGridSpec(
            num_scalar_prefetch=1,
            grid=(t // tm, len(srcs)),
            in_specs=[
                pl.BlockSpec((tm, d), lambda i, j, src: (i, 0)),
                pl.BlockSpec((1, d), lambda i, j, src: (0, 0)),
                pl.BlockSpec((pl.Squeezed(), pl.Element(W_TILE), pl.Element(d)),
                             lambda i, j, src: (layer, src[j] * unit, 0)),
                pl.BlockSpec((1, W_TILE), lambda i, j, src: (0, j)),
            ],
            out_specs=pl.BlockSpec((tm, W_TILE), lambda i, j, src: (i, j)),
            scratch_shapes=[pltpu.VMEM((tm, d), BF16)],
        ),
        out_shape=jax.ShapeDtypeStruct((t, IN_WIDTH_P), BF16),
        compiler_params=pltpu.CompilerParams(
            dimension_semantics=("arbitrary", "arbitrary"), vmem_limit_bytes=_vmem_limit(est)),
        name="in_proj",
    )(src_units, x, g.reshape(1, d), wt_all, col_scale)


def _norm_matmul(x, g, w, col_scale, *, tm, tn):
    t, d = x.shape
    n = w.shape[1]
    est = 2 * tm * d * 4 + tm * d * 2 + 2 * d * tn * 2 + 2 * tm * tn * 2 + tm * tn * 4
    return pl.pallas_call(
        functools.partial(_norm_matmul_kernel, chunk=64),
        grid=(t // tm, n // tn),
        in_specs=[
            pl.BlockSpec((tm, d), lambda i, j: (i, 0)),
            pl.BlockSpec((1, d), lambda i, j: (0, 0)),
            pl.BlockSpec((d, tn), lambda i, j: (0, j)),
            pl.BlockSpec((1, tn), lambda i, j: (0, j)),
        ],
        out_specs=pl.BlockSpec((tm, tn), lambda i, j: (i, j)),
        out_shape=jax.ShapeDtypeStruct((t, n), BF16),
        scratch_shapes=[pltpu.VMEM((tm, d), BF16)],
        compiler_params=pltpu.CompilerParams(
            dimension_semantics=("arbitrary", "arbitrary"), vmem_limit_bytes=_vmem_limit(est)),
        name="norm_matmul",
    )(x, g.reshape(1, d), w, col_scale)


def _bias_tiles_kernel(rb_ref, own_ref, prev_ref):
    h = pl.program_id(0)
    key = lax.broadcasted_iota(jnp.int32, (TQ, TQ), 0)
    qry = lax.broadcasted_iota(jnp.int32, (TQ, TQ), 1)
    d_own = qry - key
    d_prev = d_own + MOBA_BLOCK

    def lookup(dist):
        val = jnp.zeros(dist.shape, F32) + rb_ref[0, h]
        for b in range(1, N_BUCKETS):
            val = jnp.where(dist >= T5_THRESHOLDS[b - 1], rb_ref[b, h], val)
        return val * LOG2E

    own_ref[0] = jnp.where(d_own >= 0, lookup(d_own), -jnp.inf)
    prev_ref[0] = lookup(d_prev)


def _bias_tiles(rel_bias):
    heads = rel_bias.shape[1]
    tile = jax.ShapeDtypeStruct((heads, TQ, TQ), F32)
    spec = pl.BlockSpec((1, TQ, TQ), lambda h: (h, 0, 0))
    return pl.pallas_call(
        _bias_tiles_kernel,
        grid=(heads,),
        in_specs=[pl.BlockSpec(memory_space=pltpu.SMEM)],
        out_specs=[spec, spec],
        out_shape=[tile, tile],
        name="bias_tiles",
    )(rel_bias)


def _build_vt_block(v_ref, vt_ref, j):
    dv = v_ref.shape[-1]
    blk = slice(j * TQ, (j + 1) * TQ)
    vt_ref[0:dv, blk] = v_ref[blk, :].astype(F32).T.astype(BF16)
    row = lax.broadcasted_iota(jnp.int32, (ONES_ROWS, TQ), 0)
    vt_ref[dv:dv + ONES_ROWS, blk] = jnp.where(row == 0, 1.0, 0.0).astype(BF16)


def _attention(n_tiles, q_tile_of, k_block_of, terms_of, prepare, vt_ref, t_bufs, emit):
    nt = (((1,), (1,)), ((), ()))
    dv = vt_ref.shape[0] - ONES_ROWS
    state = {}

    def stage1(i):
        prepare(i)
        q = q_tile_of(i)
        adds, consts, sels = terms_of(i, q)
        st = state[i] = dict(consts=consts, sels=sels, m=None, acc=None)
        t_buf = t_bufs[i % 2]

        def item(j):
            blk = slice(j * TQ, (j + 1) * TQ)
            t = lax.dot_general(k_block_of(j), q, nt, preferred_element_type=F32)
            if adds[j] is not None:
                t = t + adds[j][...]
            t_buf[blk, :] = t
            mj = jnp.max(t, axis=0, keepdims=True) + consts[j]
            if sels[j] is not None:
                mj = jnp.where(sels[j], mj, -jnp.inf)
            st["m"] = mj if st["m"] is None else jnp.maximum(st["m"], mj)
        return [functools.partial(item, j) for j in range(len(consts))]

    def stage2(i):
        st = state.pop(i)
        t_buf = t_bufs[i % 2]

        def item(j):
            blk = slice(j * TQ, (j + 1) * TQ)
            off = st["m"] - st["consts"][j]
            if st["sels"][j] is not None:
                off = jnp.where(st["sels"][j], off, jnp.inf)
            p = jnp.exp2(t_buf[blk, :] - off).astype(BF16)
            part = jnp.dot(vt_ref[:, blk], p, preferred_element_type=F32)
            st["acc"] = part if st["acc"] is None else st["acc"] + part

        def finish():
            acc = st["acc"]
            emit(i, (acc[0:dv, :] / acc[dv:dv + 1, :]).T)
        return [functools.partial(item, j) for j in range(len(st["consts"]))], finish

    for item in stage1(0):
        item()
        yield
    for i in range(n_tiles):
        ahead = stage1(i + 1) if i + 1 < n_tiles else []
        behind, finish = stage2(i)
        for k in range(max(len(ahead), len(behind))):
            if k < len(ahead):
                ahead[k]()
            if k < len(behind):
                behind[k]()
            yield
        finish()


def _run_streams(streams):
    active = list(streams)
    while active:
        for stream in list(active):
            if next(stream, StopIteration) is StopIteration:
                active.remove(stream)


def _silu_gate(o, z):
    zf = z.astype(F32)
    return (o * (zf / (1.0 + jnp.exp(-zf)))).astype(BF16)


def _write_causal_tile(mask_ref):
    key = lax.broadcasted_iota(jnp.int32, (TQ, TQ), 0)
    qry = lax.broadcasted_iota(jnp.int32, (TQ, TQ), 1)
    mask_ref[...] = jnp.where(key <= qry, 0.0, -jnp.inf).astype(F32)


def _head_view(ref, head, width):
    return ref.at[0, :, head * width:(head + 1) * width]


def _moba_kernel(rb_ref, q_ref, k_ref, v_ref, z_ref, own_ref, prev_ref, o_ref, *scratch, heads):
    per_head = len(scratch) // heads
    _run_streams([
        _moba_head(rb_ref, pl.program_id(1) * heads + s, _head_view(q_ref, s, HEAD_DIM),
                   _head_view(k_ref, s, HEAD_DIM), _head_view(v_ref, s, HEAD_DIM),
                   _head_view(z_ref, s, HEAD_DIM), own_ref.at[s], prev_ref.at[s],
                   _head_view(o_ref, s, HEAD_DIM), *scratch[s * per_head:(s + 1) * per_head])
        for s in range(heads)])


def _moba_head(rb_ref, head, q_ref, k_ref, v_ref, z_ref, own_ref, prev_ref, o_ref, vt_ref, t0, t1, km_ref):
    seq = q_ref.shape[0]
    n_tiles = seq // TQ
    far_const = rb_ref[N_BUCKETS - 1, head] * LOG2E
    km_ref[...] = jnp.zeros(km_ref.shape, F32)

    def prepare(i):
        _build_vt_block(v_ref, vt_ref, i)
        k_blk = k_ref[i * TQ:(i + 1) * TQ, :].astype(F32)
        km_ref[i:i + 1, :] = jnp.sum(k_blk, axis=0, keepdims=True) * (1.0 / MOBA_BLOCK)

    nt = (((1,), (1,)), ((), ()))

    def terms_of(i, q_tile):
        if i > MOBA_TOPK:
            k_mean = km_ref[...]
            km1 = k_mean.astype(BF16)
            rem = k_mean - km1.astype(F32)
            km2 = rem.astype(BF16)
            km3 = (rem - km2.astype(F32)).astype(BF16)
            gate = (lax.dot_general(km1, q_tile, nt, preferred_element_type=F32)
                    + lax.dot_general(km2, q_tile, nt, preferred_element_type=F32)
                    + lax.dot_general(km3, q_tile, nt, preferred_element_type=F32))
            sels = []
            for j in range(i):
                gj = gate[j:j + 1, :]
                cnt = jnp.zeros(gj.shape, F32)
                for jp in range(i):
                    if jp == j:
                        continue
                    gp = gate[jp:jp + 1, :]
                    beats = (gp >= gj) if jp < j else (gp > gj)
                    cnt = cnt + jnp.where(beats, 1.0, 0.0)
                sels.append(cnt < MOBA_TOPK)
        else:
            sels = [None] * i
        sels.append(None)
        adds = [None] * (i + 1)
        consts = [far_const] * (i + 1)
        adds[i], consts[i] = own_ref, 0.0
        if i >= 1:
            adds[i - 1], consts[i - 1] = prev_ref, 0.0
        return adds, consts, sels

    def emit(i, o):
        rows = slice(i * TQ, (i + 1) * TQ)
        o_ref[rows, :] = _silu_gate(o, z_ref[rows, :])

    return _attention(n_tiles, lambda i: q_ref[i * TQ:(i + 1) * TQ, :],
                      lambda j: k_ref[j * TQ:(j + 1) * TQ, :], terms_of, prepare, vt_ref, (t0, t1), emit)


HEADS_PER_STEP = 2


def _attn_scratch(n_keys, extra=()):
    per_head = [pltpu.VMEM((VT_ROWS, n_keys), BF16), pltpu.VMEM((n_keys, TQ), F32),
                pltpu.VMEM((n_keys, TQ), F32), *extra]
    return per_head * HEADS_PER_STEP


def _moba_attn(u3, rel_bias, own, prev):
    b, s, _ = u3.shape
    hp = HEADS_PER_STEP
    width = hp * HEAD_DIM
    col = lambda off: (lambda bi, h: (bi, 0, off // width + h))
    blk = (1, s, width)
    tile_spec = pl.BlockSpec((hp, TQ, TQ), lambda bi, h: (h, 0, 0))
    est = 10 * s * width * 2 + 4 * hp * TQ * TQ * 4 + hp * (VT_ROWS * s * 2 + 2 * s * TQ * 4)
    return pl.pallas_call(
        functools.partial(_moba_kernel, heads=hp),
        grid=(b, MOBA_HEADS // hp),
        in_specs=[
            pl.BlockSpec(memory_space=pltpu.SMEM),
            pl.BlockSpec(blk, col(OFF_QA)),
            pl.BlockSpec(blk, col(OFF_KA)),
            pl.BlockSpec(blk, col(OFF_VA)),
            pl.BlockSpec(blk, col(OFF_ZA)),
            tile_spec, tile_spec,
        ],
        out_specs=pl.BlockSpec(blk, lambda bi, h: (bi, 0, h)),
        out_shape=jax.ShapeDtypeStruct((b, s, MOBA_W), BF16),
        scratch_shapes=_attn_scratch(s, extra=(pltpu.VMEM((16, HEAD_DIM), F32),)),
        compiler_params=pltpu.CompilerParams(
            dimension_semantics=("arbitrary", "arbitrary"), vmem_limit_bytes=_vmem_limit(est)),
        name="moba_attn",
    )(rel_bias, u3, u3, u3, u3, own, prev)


def _mla_prep_kernel(cq_ref, ckv_ref, kr_ref, gq_ref, gkv_ref, wuq_ref, wukv_ref, cos_ref, sin_ref,
                     q_out, k_out, v_out, *, q_scale):
    def rms(x_ref, g_ref):
        xf = x_ref[...].astype(F32)
        ms = jnp.mean(xf * xf, axis=-1, keepdims=True)
        return (xf * lax.rsqrt(ms + EPS) * g_ref[...]).astype(BF16)

    cos = cos_ref[...]
    sin = sin_ref[...]
    half = MLA_ROPE // 2
    first_half = lax.broadcasted_iota(jnp.int32, cos.shape, 1) < half

    def rope(xr):
        partner = jnp.where(first_half, pltpu.roll(xr, LANES - half, 1), pltpu.roll(xr, half, 1))
        return xr * cos + partner * sin

    qb = jnp.dot(rms(cq_ref, gq_ref), wuq_ref[...], preferred_element_type=F32) * q_scale
    for h in range(MLA_HEADS):
        base = h * MLA_QK
        q_out[:, base:base + MLA_NOPE] = qb[:, base:base + MLA_NOPE].astype(BF16)
        q_out[:, base + MLA_NOPE:base + MLA_QK] = rope(qb[:, base + MLA_NOPE:base + MLA_QK]).astype(BF16)

    kvb = jnp.dot(rms(ckv_ref, gkv_ref), wukv_ref[...], preferred_element_type=F32)
    in_rope = lax.broadcasted_iota(jnp.int32, cos.shape, 1) < MLA_ROPE
    k_rope = rope(jnp.where(in_rope, kr_ref[:, 0:LANES].astype(F32), 0.0)).astype(BF16)
    for h in range(MLA_HEADS):
        base = h * MLA_QK
        k_out[:, base:base + MLA_NOPE] = kvb[:, h * MLA_NOPE:(h + 1) * MLA_NOPE].astype(BF16)
        k_out[:, base + MLA_NOPE:base + MLA_QK] = k_rope
    v_out[...] = kvb[:, MLA_W:2 * MLA_W].astype(BF16)


def _mla_prep(u, g_cq, g_ckv, wuq_p, wukv_p, cos_t, sin_t, *, seq, tm):
    t = u.shape[0]
    s_tiles = seq // tm
    const = lambda i: (0, 0)
    qk_shape = jax.ShapeDtypeStruct((t, MLA_HEADS * MLA_QK), BF16)
    est = (2 * tm * (MLA_Q_LORA + 2 * MLA_KV_LORA) * 2 + 2 * (wuq_p.size + wukv_p.size) * 2
           + 4 * tm * LANES * 4 + 2 * tm * 5 * MLA_W * 2 + 4 * tm * 2 * MLA_W * 4)
    return pl.pallas_call(
        functools.partial(_mla_prep_kernel, q_scale=(MLA_NOPE + MLA_ROPE) ** -0.5 * LOG2E),
        grid=(t // tm,),
        in_specs=[
            pl.BlockSpec((tm, MLA_Q_LORA), lambda i: (i, OFF_CQ // MLA_Q_LORA)),
            pl.BlockSpec((tm, MLA_KV_LORA), lambda i: (i, OFF_CKV // MLA_KV_LORA)),
            pl.BlockSpec((tm, KR_PAD), lambda i: (i, OFF_KR // KR_PAD)),
            pl.BlockSpec((1, MLA_Q_LORA), const),
            pl.BlockSpec((1, MLA_KV_LORA), const),
            pl.BlockSpec(wuq_p.shape, const),
            pl.BlockSpec(wukv_p.shape, const),
            pl.BlockSpec((tm, LANES), lambda i: (i % s_tiles, 0)),
            pl.BlockSpec((tm, LANES), lambda i: (i % s_tiles, 0)),
        ],
        out_specs=[
            pl.BlockSpec((tm, MLA_HEADS * MLA_QK), lambda i: (i, 0)),
            pl.BlockSpec((tm, MLA_HEADS * MLA_QK), lambda i: (i, 0)),
            pl.BlockSpec((tm, MLA_W), lambda i: (i, 0)),
        ],
        out_shape=[qk_shape, qk_shape, jax.ShapeDtypeStruct((t, MLA_W), BF16)],
        compiler_params=pltpu.CompilerParams(
            dimension_semantics=("arbitrary",), vmem_limit_bytes=_vmem_limit(est)),
        name="mla_prep",
    )(u, u, u, g_cq.reshape(1, -1), g_ckv.reshape(1, -1), wuq_p, wukv_p, cos_t, sin_t)


def _mla_attn_kernel(q_ref, k_ref, v_ref, z_ref, o_ref, mask_ref, *scratch, heads):
    _write_causal_tile(mask_ref)
    per_head = len(scratch) // heads
    _run_streams([
        _mla_head(_head_view(q_ref, s, MLA_QK), _head_view(k_ref, s, MLA_QK), _head_view(v_ref, s, MLA_V),
                  _head_view(z_ref, s, MLA_V), _head_view(o_ref, s, MLA_V), mask_ref,
                  *scratch[s * per_head:(s + 1) * per_head])
        for s in range(heads)])


def _mla_head(q_ref, k_ref, v_ref, z_ref, o_ref, mask_ref, vt_ref, t0, t1):
    seq = q_ref.shape[0]

    def terms_of(i, q_tile):
        return [None] * i + [mask_ref], [0.0] * (i + 1), [None] * (i + 1)

    def emit(i, o):
        rows = slice(i * TQ, (i + 1) * TQ)
        o_ref[rows, :] = _silu_gate(o, z_ref[rows, :])

    return _attention(seq // TQ, lambda i: q_ref[i * TQ:(i + 1) * TQ, :],
                      lambda j: k_ref[j * TQ:(j + 1) * TQ, :], terms_of,
                      functools.partial(_build_vt_block, v_ref, vt_ref), vt_ref, (t0, t1), emit)


def _mla_attn(q3, k3, v3, u3):
    b, s, _ = q3.shape
    hp = HEADS_PER_STEP
    est = (4 * s * hp * MLA_QK * 2 + 6 * s * hp * MLA_V * 2 + hp * (VT_ROWS * s * 2 + 2 * s * TQ * 4)
           + TQ * TQ * 4)
    return pl.pallas_call(
        functools.partial(_mla_attn_kernel, heads=hp),
        grid=(b, MLA_HEADS // hp),
        in_specs=[
            pl.BlockSpec((1, s, hp * MLA_QK), lambda bi, h: (bi, 0, h)),
            pl.BlockSpec((1, s, hp * MLA_QK), lambda bi, h: (bi, 0, h)),
            pl.BlockSpec((1, s, hp * MLA_V), lambda bi, h: (bi, 0, h)),
            pl.BlockSpec((1, s, hp * MLA_V), lambda bi, h: (bi, 0, OFF_ZB // (hp * MLA_V) + h)),
        ],
        out_specs=pl.BlockSpec((1, s, hp * MLA_V), lambda bi, h: (bi, 0, h)),
        out_shape=jax.ShapeDtypeStruct((b, s, MLA_W), BF16),
        scratch_shapes=[pltpu.VMEM((TQ, TQ), F32)] + _attn_scratch(s),
        compiler_params=pltpu.CompilerParams(
            dimension_semantics=("arbitrary", "arbitrary"), vmem_limit_bytes=_vmem_limit(est)),
        name="mla_attn",
    )(q3, k3, v3, u3)


def _mem_attn_kernel(q_ref, k_ref, v_ref, z_ref, o_ref, *scratch, heads):
    assert k_ref.shape[1] == TQ
    per_head = len(scratch) // heads
    d = MEM_HEAD_DIM
    _run_streams([
        _mem_head(_head_view(q_ref, s, d), _head_view(k_ref, s, d), _head_view(v_ref, s, d),
                  _head_view(z_ref, s, d), _head_view(o_ref, s, d), *scratch[s * per_head:(s + 1) * per_head])
        for s in range(heads)])


def _mem_head(q_ref, k_ref, v_ref, z_ref, o_ref, vt_ref, t0, t1):
    seq = q_ref.shape[0]

    def prepare(i):
        if i == 0:
            _build_vt_block(v_ref, vt_ref, 0)

    def emit(i, o):
        rows = slice(i * TQ, (i + 1) * TQ)
        o_ref[rows, :] = _silu_gate(o, z_ref[rows, :])

    return _attention(seq // TQ, lambda i: q_ref[i * TQ:(i + 1) * TQ, :], lambda j: k_ref[...],
                      lambda i, q_tile: ([None], [0.0], [None]), prepare, vt_ref, (t0, t1), emit)


def _mem_attn(u3, kv3):
    b, s, _ = u3.shape
    m = kv3.shape[1]
    hp = HEADS_PER_STEP
    d = hp * MEM_HEAD_DIM
    est = 6 * s * d * 2 + 4 * m * d * 2 + hp * (VT_ROWS * m * 2 + 2 * m * TQ * 4)
    return pl.pallas_call(
        functools.partial(_mem_attn_kernel, heads=hp),
        grid=(b, MEM_HEADS // hp),
        in_specs=[
            pl.BlockSpec((1, s, d), lambda bi, h: (bi, 0, OFF_QM // d + h)),
            pl.BlockSpec((1, m, d), lambda bi, h: (bi, 0, h)),
            pl.BlockSpec((1, m, d), lambda bi, h: (bi, 0, MEM_W // d + h)),
            pl.BlockSpec((1, s, d), lambda bi, h: (bi, 0, OFF_ZM // d + h)),
        ],
        out_specs=pl.BlockSpec((1, s, d), lambda bi, h: (bi, 0, h)),
        out_shape=jax.ShapeDtypeStruct((b, s, MEM_W), BF16),
        scratch_shapes=_attn_scratch(m),
        compiler_params=pltpu.CompilerParams(
            dimension_semantics=("arbitrary", "arbitrary"), vmem_limit_bytes=_vmem_limit(est)),
        name="mem_attn",
    )(u3, kv3, kv3, u3)


def _merge_kernel(ga_ref, gb_ref, gm_ref, gl_ref, x_ref, wpa_ref, wpb_ref, wpm_ref, wout_ref, gf_ref,
                  o_ref, *, final_norm):
    d = x_ref.shape[1]

    def gated(idx, g_ref, w_ref):
        logit = gl_ref[:, idx * d:(idx + 1) * d].astype(F32)
        return jnp.dot(g_ref[...], w_ref[...], preferred_element_type=F32) / (1.0 + jnp.exp(-logit))

    y = gated(0, ga_ref, wpa_ref) + gated(1, gb_ref, wpb_ref) + gated(2, gm_ref, wpm_ref)
    r = x_ref[...] + jnp.dot(y.astype(BF16), wout_ref[...], preferred_element_type=F32)
    if final_norm:
        ms = jnp.mean(r * r, axis=-1, keepdims=True)
        r = r * lax.rsqrt(ms + EPS) * gf_ref[...]
    o_ref[...] = r


def _merge(ga, gb, gm, u, x, wpa, wpb, wpm, wout, g_final, *, tm, final_norm):
    t, d = x.shape
    const = lambda i: (0, 0)
    resident = lambda w: pl.BlockSpec(w.shape, const, pipeline_mode=pl.Buffered(1))
    rows = lambda width: pl.BlockSpec((tm, width), lambda i: (i, 0))
    w_bytes = (wpa.size + wpb.size + wpm.size + wout.size) * 2
    est = w_bytes + 2 * tm * (2 * MOBA_W + MEM_W + 3 * d) * 2 + 4 * tm * d * 4 + 6 * tm * d * 4
    return pl.pallas_call(
        functools.partial(_merge_kernel, final_norm=final_norm),
        grid=(t // tm,),
        in_specs=[
            rows(MOBA_W), rows(MLA_W), rows(MEM_W),
            pl.BlockSpec((tm, 3 * d), lambda i: (i, OFF_GL // (3 * d))),
            rows(d),
            resident(wpa), resident(wpb), resident(wpm), resident(wout),
            pl.BlockSpec((1, d), const),
        ],
        out_specs=rows(d),
        out_shape=jax.ShapeDtypeStruct((t, d), F32),
        compiler_params=pltpu.CompilerParams(
            dimension_semantics=("arbitrary",), vmem_limit_bytes=_vmem_limit(est)),
        name="merge",
    )(ga, gb, gm, u, x, wpa, wpb, wpm, wout, g_final.reshape(1, d))


def _w_in_tile_sources(n_cols):
    o_cq = 4 * MOBA_W
    o_zb = o_cq + MLA_Q_LORA + MLA_KV_LORA + MLA_ROPE
    o_qm = o_zb + MLA_W
    o_gl = o_qm + 2 * MEM_W
    srcs = ([o_gl + W_TILE * k for k in range(3 * D_MODEL // W_TILE)]
            + [W_TILE * k for k in range(4 * MOBA_W // W_TILE)] + [o_zb, o_cq, o_qm])
    assert len(srcs) * W_TILE == IN_WIDTH_P and o_gl + 3 * D_MODEL == n_cols
    assert all(src + W_TILE <= n_cols for src in srcs)
    return srcs


def _in_col_scale():
    cs = jnp.ones((1, IN_WIDTH_P), F32)
    cs = cs.at[:, OFF_QA:OFF_QA + MOBA_W].set(HEAD_DIM ** -0.5 * LOG2E)
    return cs.at[:, OFF_QM:OFF_QM + MEM_W].set(MEM_HEAD_DIM ** -0.5 * LOG2E)


def _regroup_w_uq(w):
    r = w.shape[0]
    w3 = w.reshape(r, MLA_HEADS, MLA_NOPE + MLA_ROPE)
    pad = jnp.zeros((r, MLA_HEADS, MLA_QK - MLA_NOPE - MLA_ROPE), w.dtype)
    return jnp.concatenate([w3, pad], axis=-1).reshape(r, MLA_HEADS * MLA_QK).astype(BF16)


def _regroup_w_ukv(w):
    r = w.shape[0]
    w3 = w.reshape(r, MLA_HEADS, MLA_NOPE + MLA_V)
    return jnp.concatenate([w3[:, :, :MLA_NOPE].reshape(r, MLA_W),
                            w3[:, :, MLA_NOPE:].reshape(r, MLA_W)], axis=1).astype(BF16)


def _rope_tables(seq):
    half = MLA_ROPE // 2
    inv = ROPE_THETA ** (-jnp.arange(half, dtype=F32) / half)
    ang = jnp.arange(seq, dtype=jnp.int32).astype(F32)[:, None] * inv[None, :]
    cos, sin = jnp.cos(ang), jnp.sin(ang)
    pad = LANES - MLA_ROPE
    cos_t = jnp.concatenate([cos, cos, jnp.ones((seq, pad), F32)], axis=1)
    sin_t = jnp.concatenate([-sin, sin, jnp.zeros((seq, pad), F32)], axis=1)
    return cos_t, sin_t


def kernel(x, mem, g_norm, w_in, g_cq, w_uq, g_ckv, w_ukv, g_mem, w_mem_kv, rel_bias,
           w_p_moba, w_p_mla, w_p_mem, w_out, g_final):
    b, s, d = x.shape
    m = mem.shape[1]
    depth = w_in.shape[0]
    t = b * s
    assert d == D_MODEL and s % TQ == 0 and m == TQ

    own, prev = _bias_tiles(rel_bias)
    cos_t, sin_t = _rope_tables(s)
    mem2 = mem.reshape(b * m, d)
    xs = x.reshape(t, d)
    in_scale = _in_col_scale()
    w_in_t = jnp.swapaxes(w_in, 1, 2)
    kv_scale = jnp.ones((1, 2 * MEM_W), F32)
    for l in range(depth):
        u = _in_proj(xs, g_norm[l], w_in_t, l, in_scale, tm=1024)
        u3 = u.reshape(b, s, IN_WIDTH_P)
        ga = _moba_attn(u3, rel_bias, own, prev)
        q2, k2, v2 = _mla_prep(u, g_cq[l], g_ckv[l], _regroup_w_uq(w_uq[l]), _regroup_w_ukv(w_ukv[l]),
                               cos_t, sin_t, seq=s, tm=512)
        gb = _mla_attn(q2.reshape(b, s, -1), k2.reshape(b, s, -1), v2.reshape(b, s, -1), u3)
        kvm = _norm_matmul(mem2, g_mem[l], w_mem_kv[l].astype(BF16), kv_scale, tm=b * m, tn=2 * MEM_W)
        gm = _mem_attn(u3, kvm.reshape(b, m, 2 * MEM_W))
        xs = _merge(ga.reshape(t, MOBA_W), gb.reshape(t, MLA_W), gm.reshape(t, MEM_W), u, xs,
                    w_p_moba[l].astype(BF16), w_p_mla[l].astype(BF16), w_p_mem[l].astype(BF16),
                    w_out[l].astype(BF16), g_final, tm=256, final_norm=(l == depth - 1))
    return xs.reshape(b, s, d)
```

```python
import functools
import math

import jax
import jax.numpy as jnp
from jax import lax
from jax.experimental import pallas as pl
from jax.experimental.pallas import tpu as pltpu

D_MODEL = 2048
MOBA_HEADS = 8
HEAD_DIM = 128
MOBA_BLOCK = 256
MOBA_TOPK = 3
MLA_HEADS = 8
MLA_Q_LORA = 512
MLA_KV_LORA = 256
MLA_NOPE = 128
MLA_ROPE = 64
MLA_V = 128
ROPE_THETA = 10000.0
MEM_HEADS = 4
MEM_HEAD_DIM = 128
N_BUCKETS = 32
MAX_DISTANCE = 128
EPS = 1e-6

MOBA_W = MOBA_HEADS * HEAD_DIM
MLA_W = MLA_HEADS * MLA_V
MEM_W = MEM_HEADS * MEM_HEAD_DIM

LANES = 128
MXU_DIM = 256
V7X_VMEM_BYTES = 64 * 1024 * 1024

BF16 = jnp.bfloat16
F32 = jnp.float32
LOG2E = 1.4426950408889634

KR_PAD = MXU_DIM
OFF_GL = 0
OFF_QA = OFF_GL + 3 * D_MODEL
OFF_KA = OFF_QA + MOBA_W
OFF_VA = OFF_KA + MOBA_W
OFF_ZA = OFF_VA + MOBA_W
OFF_ZB = OFF_ZA + MOBA_W
OFF_CQ = OFF_ZB + MLA_W
OFF_CKV = OFF_CQ + MLA_Q_LORA
OFF_KR = OFF_CKV + MLA_KV_LORA
OFF_QM = OFF_KR + KR_PAD
OFF_ZM = OFF_QM + MEM_W
IN_WIDTH_P = OFF_ZM + MEM_W

TQ = MOBA_BLOCK
ONES_ROWS = 16
VT_ROWS = MLA_V + ONES_ROWS
MLA_QK = MXU_DIM

W_TILE = 1024


def _vmem_limit(nbytes):
    return int(min(nbytes + (8 << 20), V7X_VMEM_BYTES - (4 << 20)))


def _t5_thresholds():
    max_exact = N_BUCKETS // 2

    def bucket(d):
        if d < max_exact:
            return d
        large = max_exact + int(math.log(d / max_exact) / math.log(MAX_DISTANCE / max_exact)
                                * (N_BUCKETS - max_exact))
        return min(large, N_BUCKETS - 1)

    thr, d = [], 0
    for b in range(1, N_BUCKETS):
        while bucket(d) < b:
            d += 1
        thr.append(d)
    return tuple(thr)


T5_THRESHOLDS = _t5_thresholds()
assert T5_THRESHOLDS[-1] <= MOBA_BLOCK + 1


def _norm_rows(x_ref, g_ref, h_ref, chunk):
    def body(r, carry):
        rows = pl.ds(pl.multiple_of(r * chunk, chunk), chunk)
        xv = x_ref[rows, :]
        ms = jnp.mean(xv * xv, axis=-1, keepdims=True)
        h_ref[rows, :] = (xv * lax.rsqrt(ms + EPS) * g_ref[...]).astype(BF16)
        return carry
    lax.fori_loop(0, x_ref.shape[0] // chunk, body, 0)


def _norm_matmul_kernel(x_ref, g_ref, w_ref, cs_ref, o_ref, h_ref, *, chunk):
    pl.when(pl.program_id(1) == 0)(functools.partial(_norm_rows, x_ref, g_ref, h_ref, chunk))
    acc = jnp.dot(h_ref[...], w_ref[...], preferred_element_type=F32)
    o_ref[...] = (acc * cs_ref[...]).astype(o_ref.dtype)


def _in_proj_kernel(src_ref, x_ref, g_ref, wt_ref, cs_ref, o_ref, h_ref, *, chunk):
    del src_ref
    pl.when(pl.program_id(1) == 0)(functools.partial(_norm_rows, x_ref, g_ref, h_ref, chunk))
    acc = lax.dot_general(h_ref[...], wt_ref[...].astype(BF16), (((1,), (1,)), ((), ())),
                          preferred_element_type=F32)
    o_ref[...] = (acc * cs_ref[...]).astype(o_ref.dtype)


def _in_proj(x, g, wt_all, layer, col_scale, *, tm):
    t, d = x.shape
    srcs = _w_in_tile_sources(wt_all.shape[1])
    unit = 64
    assert all(src % unit == 0 for src in srcs)
    src_units = jnp.asarray([src // unit for src in srcs], jnp.int32)
    est = 2 * tm * d * 4 + tm * d * 2 + 2 * W_TILE * d * 4 + W_TILE * d * 2 + 2 * tm * W_TILE * 2 + tm * W_TILE * 4
    return pl.pallas_call(
        functools.partial(_in_proj_kernel, chunk=64),
        grid_spec=pltpu.PrefetchScalarGridSpec(
            num_scalar_prefetch=1,
            grid=(t // tm, len(srcs)),
            in_specs=[
                pl.BlockSpec((tm, d), lambda i, j, src: (i, 0)),
                pl.BlockSpec((1, d), lambda i, j, src: (0, 0)),
                pl.BlockSpec((pl.Squeezed(), pl.Element(W_TILE), pl.Element(d)),
                             lambda i, j, src: (layer, src[j] * unit, 0)),
                pl.BlockSpec((1, W_TILE), lambda i, j, src: (0, j)),
            ],
            out_specs=pl.BlockSpec((tm, W_TILE), lambda i, j, src: (i, j)),
            scratch_shapes=[pltpu.VMEM((tm, d), BF16)],
        ),
        out_shape=jax.ShapeDtypeStruct((t, IN_WIDTH_P), BF16),
        compiler_params=pltpu.CompilerParams(
            dimension_semantics=("arbitrary", "arbitrary"), vmem_limit_bytes=_vmem_limit(est)),
        name="in_proj",
    )(src_units, x, g.reshape(1, d), wt_all, col_scale)


def _norm_matmul(x, g, w, col_scale, *, tm, tn):
    t, d = x.shape
    n = w.shape[1]
    est = 2 * tm * d * 4 + tm * d * 2 + 2 * d * tn * 2 + 2 * tm * tn * 2 + tm * tn * 4
    return pl.pallas_call(
        functools.partial(_norm_matmul_kernel, chunk=64),
        grid=(t // tm, n // tn),
        in_specs=[
            pl.BlockSpec((tm, d), lambda i, j: (i, 0)),
            pl.BlockSpec((1, d), lambda i, j: (0, 0)),
            pl.BlockSpec((d, tn), lambda i, j: (0, j)),
            pl.BlockSpec((1, tn), lambda i, j: (0, j)),
        ],
        out_specs=pl.BlockSpec((tm, tn), lambda i, j: (i, j)),
        out_shape=jax.ShapeDtypeStruct((t, n), BF16),
        scratch_shapes=[pltpu.VMEM((tm, d), BF16)],
        compiler_params=pltpu.CompilerParams(
            dimension_semantics=("arbitrary", "arbitrary"), vmem_limit_bytes=_vmem_limit(est)),
        name="norm_matmul",
    )(x, g.reshape(1, d), w, col_scale)


def _bias_tiles_kernel(rb_ref, own_ref, prev_ref):
    h = pl.program_id(0)
    key = lax.broadcasted_iota(jnp.int32, (TQ, TQ), 0)
    qry = lax.broadcasted_iota(jnp.int32, (TQ, TQ), 1)
    d_own = qry - key
    d_prev = d_own + MOBA_BLOCK

    def lookup(dist):
        val = jnp.zeros(dist.shape, F32) + rb_ref[0, h]
        for b in range(1, N_BUCKETS):
            val = jnp.where(dist >= T5_THRESHOLDS[b - 1], rb_ref[b, h], val)
        return val * LOG2E

    own_ref[0] = jnp.where(d_own >= 0, lookup(d_own), -jnp.inf)
    prev_ref[0] = lookup(d_prev)


def _bias_tiles(rel_bias):
    heads = rel_bias.shape[1]
    tile = jax.ShapeDtypeStruct((heads, TQ, TQ), F32)
    spec = pl.BlockSpec((1, TQ, TQ), lambda h: (h, 0, 0))
    return pl.pallas_call(
        _bias_tiles_kernel,
        grid=(heads,),
        in_specs=[pl.BlockSpec(memory_space=pltpu.SMEM)],
        out_specs=[spec, spec],
        out_shape=[tile, tile],
        name="bias_tiles",
    )(rel_bias)


def _build_vt_block(v_ref, vt_ref, j):
    dv = v_ref.shape[-1]
    blk = slice(j * TQ, (j + 1) * TQ)
    vt_ref[0:dv, blk] = v_ref[blk, :].astype(F32).T.astype(BF16)
    row = lax.broadcasted_iota(jnp.int32, (ONES_ROWS, TQ), 0)
    vt_ref[dv:dv + ONES_ROWS, blk] = jnp.where(row == 0, 1.0, 0.0).astype(BF16)


def _attention(n_tiles, q_tile_of, k_block_of, terms_of, prepare, vt_ref, t_bufs, emit):
    nt = (((1,), (1,)), ((), ()))
    dv = vt_ref.shape[0] - ONES_ROWS
    state = {}

    def stage1(i):
        prepare(i)
        q = q_tile_of(i)
        adds, consts, sels = terms_of(i, q)
        st = state[i] = dict(consts=consts, sels=sels, m=None, acc=None)
        t_buf = t_bufs[i % 2]

        def item(j):
            blk = slice(j * TQ, (j + 1) * TQ)
            t = lax.dot_general(k_block_of(j), q, nt, preferred_element_type=F32)
            if adds[j] is not None:
                t = t + adds[j][...]
            t_buf[blk, :] = t
            mj = jnp.max(t, axis=0, keepdims=True) + consts[j]
            if sels[j] is not None:
                mj = jnp.where(sels[j], mj, -jnp.inf)
            st["m"] = mj if st["m"] is None else jnp.maximum(st["m"], mj)
        return [functools.partial(item, j) for j in range(len(consts))]

    def stage2(i):
        st = state.pop(i)
        t_buf = t_bufs[i % 2]

        def item(j):
            blk = slice(j * TQ, (j + 1) * TQ)
            off = st["m"] - st["consts"][j]
            if st["sels"][j] is not None:
                off = jnp.where(st["sels"][j], off, jnp.inf)
            p = jnp.exp2(t_buf[blk, :] - off).astype(BF16)
            part = jnp.dot(vt_ref[:, blk], p, preferred_element_type=F32)
            st["acc"] = part if st["acc"] is None else st["acc"] + part

        def finish():
            acc = st["acc"]
            emit(i, (acc[0:dv, :] / acc[dv:dv + 1, :]).T)
        return [functools.partial(item, j) for j in range(len(st["consts"]))], finish

    for item in stage1(0):
        item()
        yield
    for i in range(n_tiles):
        ahead = stage1(i + 1) if i + 1 < n_tiles else []
        behind, finish = stage2(i)
        for k in range(max(len(ahead), len(behind))):
            if k < len(ahead):
                ahead[k]()
            if k < len(behind):
                behind[k]()
            yield
        finish()


def _run_streams(streams):
    active = list(streams)
    while active:
        for stream in list(active):
            if next(stream, StopIteration) is StopIteration:
                active.remove(stream)


def _silu_gate(o, z):
    zf = z.astype(F32)
    return (o * (zf / (1.0 + jnp.exp(-zf)))).astype(BF16)


def _write_causal_tile(mask_ref):
    key = lax.broadcasted_iota(jnp.int32, (TQ, TQ), 0)
    qry = lax.broadcasted_iota(jnp.int32, (TQ, TQ), 1)
    mask_ref[...] = jnp.where(key <= qry, 0.0, -jnp.inf).astype(F32)


def _head_view(ref, head, width):
    return ref.at[0, :, head * width:(head + 1) * width]


def _moba_kernel(rb_ref, q_ref, k_ref, v_ref, z_ref, own_ref, prev_ref, o_ref, *scratch, heads):
    per_head = len(scratch) // heads
    _run_streams([
        _moba_head(rb_ref, pl.program_id(1) * heads + s, _head_view(q_ref, s, HEAD_DIM),
                   _head_view(k_ref, s, HEAD_DIM), _head_view(v_ref, s, HEAD_DIM),
                   _head_view(z_ref, s, HEAD_DIM), own_ref.at[s], prev_ref.at[s],
                   _head_view(o_ref, s, HEAD_DIM), *scratch[s * per_head:(s + 1) * per_head])
        for s in range(heads)])


def _moba_head(rb_ref, head, q_ref, k_ref, v_ref, z_ref, own_ref, prev_ref, o_ref, vt_ref, t0, t1, km_ref):
    seq = q_ref.shape[0]
    n_tiles = seq // TQ
    far_const = rb_ref[N_BUCKETS - 1, head] * LOG2E
    km_ref[...] = jnp.zeros(km_ref.shape, F32)

    def prepare(i):
        _build_vt_block(v_ref, vt_ref, i)
        k_blk = k_ref[i * TQ:(i + 1) * TQ, :].astype(F32)
        km_ref[i:i + 1, :] = jnp.sum(k_blk, axis=0, keepdims=True) * (1.0 / MOBA_BLOCK)

    nt = (((1,), (1,)), ((), ()))

    def terms_of(i, q_tile):
        if i > MOBA_TOPK:
            k_mean = km_ref[...]
            km1 = k_mean.astype(BF16)
            rem = k_mean - km1.astype(F32)
            km2 = rem.astype(BF16)
            km3 = (rem - km2.astype(F32)).astype(BF16)
            gate = (lax.dot_general(km1, q_tile, nt, preferred_element_type=F32)
                    + lax.dot_general(km2, q_tile, nt, preferred_element_type=F32)
                    + lax.dot_general(km3, q_tile, nt, preferred_element_type=F32))
            sels = []
            for j in range(i):
                gj = gate[j:j + 1, :]
                cnt = jnp.zeros(gj.shape, F32)
                for jp in range(i):
                    if jp == j:
                        continue
                    gp = gate[jp:jp + 1, :]
                    beats = (gp >= gj) if jp < j else (gp > gj)
                    cnt = cnt + jnp.where(beats, 1.0, 0.0)
                sels.append(cnt < MOBA_TOPK)
        else:
            sels = [None] * i
        sels.append(None)
        adds = [None] * (i + 1)
        consts = [far_const] * (i + 1)
        adds[i], consts[i] = own_ref, 0.0
        if i >= 1:
            adds[i - 1], consts[i - 1] = prev_ref, 0.0
        return adds, consts, sels

    def emit(i, o):
        rows = slice(i * TQ, (i + 1) * TQ)
        o_ref[rows, :] = _silu_gate(o, z_ref[rows, :])

    return _attention(n_tiles, lambda i: q_ref[i * TQ:(i + 1) * TQ, :],
                      lambda j: k_ref[j * TQ:(j + 1) * TQ, :], terms_of, prepare, vt_ref, (t0, t1), emit)


HEADS_PER_STEP = 4


def _attn_scratch(n_keys, extra=()):
    per_head = [pltpu.VMEM((VT_ROWS, n_keys), BF16), pltpu.VMEM((n_keys, TQ), F32),
                pltpu.VMEM((n_keys, TQ), F32), *extra]
    return per_head * HEADS_PER_STEP


def _moba_attn(u3, rel_bias, own, prev):
    b, s, _ = u3.shape
    hp = HEADS_PER_STEP
    width = hp * HEAD_DIM
    col = lambda off: (lambda bi, h: (bi, 0, off // width + h))
    blk = (1, s, width)
    tile_spec = pl.BlockSpec((hp, TQ, TQ), lambda bi, h: (h, 0, 0))
    est = 10 * s * width * 2 + 4 * hp * TQ * TQ * 4 + hp * (VT_ROWS * s * 2 + 2 * s * TQ * 4)
    return pl.pallas_call(
        functools.partial(_moba_kernel, heads=hp),
        grid=(b, MOBA_HEADS // hp),
        in_specs=[
            pl.BlockSpec(memory_space=pltpu.SMEM),
            pl.BlockSpec(blk, col(OFF_QA)),
            pl.BlockSpec(blk, col(OFF_KA)),
            pl.BlockSpec(blk, col(OFF_VA)),
            pl.BlockSpec(blk, col(OFF_ZA)),
            tile_spec, tile_spec,
        ],
        out_specs=pl.BlockSpec(blk, lambda bi, h: (bi, 0, h)),
        out_shape=jax.ShapeDtypeStruct((b, s, MOBA_W), BF16),
        scratch_shapes=_attn_scratch(s, extra=(pltpu.VMEM((16, HEAD_DIM), F32),)),
        compiler_params=pltpu.CompilerParams(
            dimension_semantics=("arbitrary", "arbitrary"), vmem_limit_bytes=_vmem_limit(est)),
        name="moba_attn",
    )(rel_bias, u3, u3, u3, u3, own, prev)


def _mla_prep_kernel(cq_ref, ckv_ref, kr_ref, gq_ref, gkv_ref, wuq_ref, wukv_ref, cos_ref, sin_ref,
                     q_out, k_out, v_out, *, q_scale):
    def rms(x_ref, g_ref):
        xf = x_ref[...].astype(F32)
        ms = jnp.mean(xf * xf, axis=-1, keepdims=True)
        return (xf * lax.rsqrt(ms + EPS) * g_ref[...]).astype(BF16)

    cos = cos_ref[...]
    sin = sin_ref[...]
    half = MLA_ROPE // 2
    first_half = lax.broadcasted_iota(jnp.int32, cos.shape, 1) < half

    def rope(xr):
        partner = jnp.where(first_half, pltpu.roll(xr, LANES - half, 1), pltpu.roll(xr, half, 1))
        return xr * cos + partner * sin

    qb = jnp.dot(rms(cq_ref, gq_ref), wuq_ref[...], preferred_element_type=F32) * q_scale
    for h in range(MLA_HEADS):
        base = h * MLA_QK
        q_out[:, base:base + MLA_NOPE] = qb[:, base:base + MLA_NOPE].astype(BF16)
        q_out[:, base + MLA_NOPE:base + MLA_QK] = rope(qb[:, base + MLA_NOPE:base + MLA_QK]).astype(BF16)

    kvb = jnp.dot(rms(ckv_ref, gkv_ref), wukv_ref[...], preferred_element_type=F32)
    in_rope = lax.broadcasted_iota(jnp.int32, cos.shape, 1) < MLA_ROPE
    k_rope = rope(jnp.where(in_rope, kr_ref[:, 0:LANES].astype(F32), 0.0)).astype(BF16)
    for h in range(MLA_HEADS):
        base = h * MLA_QK
        k_out[:, base:base + MLA_NOPE] = kvb[:, h * MLA_NOPE:(h + 1) * MLA_NOPE].astype(BF16)
        k_out[:, base + MLA_NOPE:base + MLA_QK] = k_rope
    v_out[...] = kvb[:, MLA_W:2 * MLA_W].astype(BF16)


def _mla_prep(u, g_cq, g_ckv, wuq_p, wukv_p, cos_t, sin_t, *, seq, tm):
    t = u.shape[0]
    s_tiles = seq // tm
    const = lambda i: (0, 0)
    qk_shape = jax.ShapeDtypeStruct((t, MLA_HEADS * MLA_QK), BF16)
    est = (2 * tm * (MLA_Q_LORA + 2 * MLA_KV_LORA) * 2 + 2 * (wuq_p.size + wukv_p.size) * 2
           + 4 * tm * LANES * 4 + 2 * tm * 5 * MLA_W * 2 + 4 * tm * 2 * MLA_W * 4)
    return pl.pallas_call(
        functools.partial(_mla_prep_kernel, q_scale=(MLA_NOPE + MLA_ROPE) ** -0.5 * LOG2E),
        grid=(t // tm,),
        in_specs=[
            pl.BlockSpec((tm, MLA_Q_LORA), lambda i: (i, OFF_CQ // MLA_Q_LORA)),
            pl.BlockSpec((tm, MLA_KV_LORA), lambda i: (i, OFF_CKV // MLA_KV_LORA)),
            pl.BlockSpec((tm, KR_PAD), lambda i: (i, OFF_KR // KR_PAD)),
            pl.BlockSpec((1, MLA_Q_LORA), const),
            pl.BlockSpec((1, MLA_KV_LORA), const),
            pl.BlockSpec(wuq_p.shape, const),
            pl.BlockSpec(wukv_p.shape, const),
            pl.BlockSpec((tm, LANES), lambda i: (i % s_tiles, 0)),
            pl.BlockSpec((tm, LANES), lambda i: (i % s_tiles, 0)),
        ],
        out_specs=[
            pl.BlockSpec((tm, MLA_HEADS * MLA_QK), lambda i: (i, 0)),
            pl.BlockSpec((tm, MLA_HEADS * MLA_QK), lambda i: (i, 0)),
            pl.BlockSpec((tm, MLA_W), lambda i: (i, 0)),
        ],
        out_shape=[qk_shape, qk_shape, jax.ShapeDtypeStruct((t, MLA_W), BF16)],
        compiler_params=pltpu.CompilerParams(
            dimension_semantics=("arbitrary",), vmem_limit_bytes=_vmem_limit(est)),
        name="mla_prep",
    )(u, u, u, g_cq.reshape(1, -1), g_ckv.reshape(1, -1), wuq_p, wukv_p, cos_t, sin_t)


def _mla_attn_kernel(q_ref, k_ref, v_ref, z_ref, o_ref, mask_ref, *scratch, heads):
    _write_causal_tile(mask_ref)
    per_head = len(scratch) // heads
    _run_streams([
        _mla_head(_head_view(q_ref, s, MLA_QK), _head_view(k_ref, s, MLA_QK), _head_view(v_ref, s, MLA_V),
                  _head_view(z_ref, s, MLA_V), _head_view(o_ref, s, MLA_V), mask_ref,
                  *scratch[s * per_head:(s + 1) * per_head])
        for s in range(heads)])


def _mla_head(q_ref, k_ref, v_ref, z_ref, o_ref, mask_ref, vt_ref, t0, t1):
    seq = q_ref.shape[0]

    def terms_of(i, q_tile):
        return [None] * i + [mask_ref], [0.0] * (i + 1), [None] * (i + 1)

    def emit(i, o):
        rows = slice(i * TQ, (i + 1) * TQ)
        o_ref[rows, :] = _silu_gate(o, z_ref[rows, :])

    return _attention(seq // TQ, lambda i: q_ref[i * TQ:(i + 1) * TQ, :],
                      lambda j: k_ref[j * TQ:(j + 1) * TQ, :], terms_of,
                      functools.partial(_build_vt_block, v_ref, vt_ref), vt_ref, (t0, t1), emit)


def _mla_attn(q3, k3, v3, u3):
    b, s, _ = q3.shape
    hp = HEADS_PER_STEP
    est = (4 * s * hp * MLA_QK * 2 + 6 * s * hp * MLA_V * 2 + hp * (VT_ROWS * s * 2 + 2 * s * TQ * 4)
           + TQ * TQ * 4)
    return pl.pallas_call(
        functools.partial(_mla_attn_kernel, heads=hp),
        grid=(b, MLA_HEADS // hp),
        in_specs=[
            pl.BlockSpec((1, s, hp * MLA_QK), lambda bi, h: (bi, 0, h)),
            pl.BlockSpec((1, s, hp * MLA_QK), lambda bi, h: (bi, 0, h)),
            pl.BlockSpec((1, s, hp * MLA_V), lambda bi, h: (bi, 0, h)),
            pl.BlockSpec((1, s, hp * MLA_V), lambda bi, h: (bi, 0, OFF_ZB // (hp * MLA_V) + h)),
        ],
        out_specs=pl.BlockSpec((1, s, hp * MLA_V), lambda bi, h: (bi, 0, h)),
        out_shape=jax.ShapeDtypeStruct((b, s, MLA_W), BF16),
        scratch_shapes=[pltpu.VMEM((TQ, TQ), F32)] + _attn_scratch(s),
        compiler_params=pltpu.CompilerParams(
            dimension_semantics=("arbitrary", "arbitrary"), vmem_limit_bytes=_vmem_limit(est)),
        name="mla_attn",
    )(q3, k3, v3, u3)


def _mem_attn_kernel(q_ref, k_ref, v_ref, z_ref, o_ref, *scratch, heads):
    assert k_ref.shape[1] == TQ
    per_head = len(scratch) // heads
    d = MEM_HEAD_DIM
    _run_streams([
        _mem_head(_head_view(q_ref, s, d), _head_view(k_ref, s, d), _head_view(v_ref, s, d),
                  _head_view(z_ref, s, d), _head_view(o_ref, s, d), *scratch[s * per_head:(s + 1) * per_head])
        for s in range(heads)])


def _mem_head(q_ref, k_ref, v_ref, z_ref, o_ref, vt_ref, t0, t1):
    seq = q_ref.shape[0]

    def prepare(i):
        if i == 0:
            _build_vt_block(v_ref, vt_ref, 0)

    def emit(i, o):
        rows = slice(i * TQ, (i + 1) * TQ)
        o_ref[rows, :] = _silu_gate(o, z_ref[rows, :])

    return _attention(seq // TQ, lambda i: q_ref[i * TQ:(i + 1) * TQ, :], lambda j: k_ref[...],
                      lambda i, q_tile: ([None], [0.0], [None]), prepare, vt_ref, (t0, t1), emit)


def _mem_attn(u3, kv3):
    b, s, _ = u3.shape
    m = kv3.shape[1]
    hp = HEADS_PER_STEP
    d = hp * MEM_HEAD_DIM
    est = 6 * s * d * 2 + 4 * m * d * 2 + hp * (VT_ROWS * m * 2 + 2 * m * TQ * 4)
    return pl.pallas_call(
        functools.partial(_mem_attn_kernel, heads=hp),
        grid=(b, MEM_HEADS // hp),
        in_specs=[
            pl.BlockSpec((1, s, d), lambda bi, h: (bi, 0, OFF_QM // d + h)),
            pl.BlockSpec((1, m, d), lambda bi, h: (bi, 0, h)),
            pl.BlockSpec((1, m, d), lambda bi, h: (bi, 0, MEM_W // d + h)),
            pl.BlockSpec((1, s, d), lambda bi, h: (bi, 0, OFF_ZM // d + h)),
        ],
        out_specs=pl.BlockSpec((1, s, d), lambda bi, h: (bi, 0, h)),
        out_shape=jax.ShapeDtypeStruct((b, s, MEM_W), BF16),
        scratch_shapes=_attn_scratch(m),
        compiler_params=pltpu.CompilerParams(
            dimension_semantics=("arbitrary", "arbitrary"), vmem_limit_bytes=_vmem_limit(est)),
        name="mem_attn",
    )(u3, kv3, kv3, u3)


def _merge_kernel(ga_ref, gb_ref, gm_ref, gl_ref, x_ref, wpa_ref, wpb_ref, wpm_ref, wout_ref, gf_ref,
                  o_ref, *, final_norm):
    d = x_ref.shape[1]

    def gated(idx, g_ref, w_ref):
        logit = gl_ref[:, idx * d:(idx + 1) * d].astype(F32)
        return jnp.dot(g_ref[...], w_ref[...], preferred_element_type=F32) / (1.0 + jnp.exp(-logit))

    y = gated(0, ga_ref, wpa_ref) + gated(1, gb_ref, wpb_ref) + gated(2, gm_ref, wpm_ref)
    r = x_ref[...] + jnp.dot(y.astype(BF16), wout_ref[...], preferred_element_type=F32)
    if final_norm:
        ms = jnp.mean(r * r, axis=-1, keepdims=True)
        r = r * lax.rsqrt(ms + EPS) * gf_ref[...]
    o_ref[...] = r


def _merge(ga, gb, gm, u, x, wpa, wpb, wpm, wout, g_final, *, tm, final_norm):
    t, d = x.shape
    const = lambda i: (0, 0)
    resident = lambda w: pl.BlockSpec(w.shape, const, pipeline_mode=pl.Buffered(1))
    rows = lambda width: pl.BlockSpec((tm, width), lambda i: (i, 0))
    w_bytes = (wpa.size + wpb.size + wpm.size + wout.size) * 2
    est = w_bytes + 2 * tm * (2 * MOBA_W + MEM_W + 3 * d) * 2 + 4 * tm * d * 4 + 6 * tm * d * 4
    return pl.pallas_call(
        functools.partial(_merge_kernel, final_norm=final_norm),
        grid=(t // tm,),
        in_specs=[
            rows(MOBA_W), rows(MLA_W), rows(MEM_W),
            pl.BlockSpec((tm, 3 * d), lambda i: (i, OFF_GL // (3 * d))),
            rows(d),
            resident(wpa), resident(wpb), resident(wpm), resident(wout),
            pl.BlockSpec((1, d), const),
        ],
        out_specs=rows(d),
        out_shape=jax.ShapeDtypeStruct((t, d), F32),
        compiler_params=pltpu.CompilerParams(
            dimension_semantics=("arbitrary",), vmem_limit_bytes=_vmem_limit(est)),
        name="merge",
    )(ga, gb, gm, u, x, wpa, wpb, wpm, wout, g_final.reshape(1, d))


def _w_in_tile_sources(n_cols):
    o_cq = 4 * MOBA_W
    o_zb = o_cq + MLA_Q_LORA + MLA_KV_LORA + MLA_ROPE
    o_qm = o_zb + MLA_W
    o_gl = o_qm + 2 * MEM_W
    srcs = ([o_gl + W_TILE * k for k in range(3 * D_MODEL // W_TILE)]
            + [W_TILE * k for k in range(4 * MOBA_W // W_TILE)] + [o_zb, o_cq, o_qm])
    assert len(srcs) * W_TILE == IN_WIDTH_P and o_gl + 3 * D_MODEL == n_cols
    assert all(src + W_TILE <= n_cols for src in srcs)
    return srcs


def _in_col_scale():
    cs = jnp.ones((1, IN_WIDTH_P), F32)
    cs = cs.at[:, OFF_QA:OFF_QA + MOBA_W].set(HEAD_DIM ** -0.5 * LOG2E)
    return cs.at[:, OFF_QM:OFF_QM + MEM_W].set(MEM_HEAD_DIM ** -0.5 * LOG2E)


def _regroup_w_uq(w):
    r = w.shape[0]
    w3 = w.reshape(r, MLA_HEADS, MLA_NOPE + MLA_ROPE)
    pad = jnp.zeros((r, MLA_HEADS, MLA_QK - MLA_NOPE - MLA_ROPE), w.dtype)
    return jnp.concatenate([w3, pad], axis=-1).reshape(r, MLA_HEADS * MLA_QK).astype(BF16)


def _regroup_w_ukv(w):
    r = w.shape[0]
    w3 = w.reshape(r, MLA_HEADS, MLA_NOPE + MLA_V)
    return jnp.concatenate([w3[:, :, :MLA_NOPE].reshape(r, MLA_W),
                            w3[:, :, MLA_NOPE:].reshape(r, MLA_W)], axis=1).astype(BF16)


def _rope_tables(seq):
    half = MLA_ROPE // 2
    inv = ROPE_THETA ** (-jnp.arange(half, dtype=F32) / half)
    ang = jnp.arange(seq, dtype=jnp.int32).astype(F32)[:, None] * inv[None, :]
    cos, sin = jnp.cos(ang), jnp.sin(ang)
    pad = LANES - MLA_ROPE
    cos_t = jnp.concatenate([cos, cos, jnp.ones((seq, pad), F32)], axis=1)
    sin_t = jnp.concatenate([-sin, sin, jnp.zeros((seq, pad), F32)], axis=1)
    return cos_t, sin_t


def kernel(x, mem, g_norm, w_in, g_cq, w_uq, g_ckv, w_ukv, g_mem, w_mem_kv, rel_bias,
           w_p_moba, w_p_mla, w_p_mem, w_out, g_final):
    b, s, d = x.shape
    m = mem.shape[1]
    depth = w_in.shape[0]
    t = b * s
    assert d == D_MODEL and s % TQ == 0 and m == TQ

    own, prev = _bias_tiles(rel_bias)
    cos_t, sin_t = _rope_tables(s)
    mem2 = mem.reshape(b * m, d)
    xs = x.reshape(t, d)
    in_scale = _in_col_scale()
    w_in_t = jnp.swapaxes(w_in, 1, 2)
    kv_scale = jnp.ones((1, 2 * MEM_W), F32)
    for l in range(depth):
        u = _in_proj(xs, g_norm[l], w_in_t, l, in_scale, tm=1024)
        u3 = u.reshape(b, s, IN_WIDTH_P)
        ga = _moba_attn(u3, rel_bias, own, prev)
        q2, k2, v2 = _mla_prep(u, g_cq[l], g_ckv[l], _regroup_w_uq(w_uq[l]), _regroup_w_ukv(w_ukv[l]),
                               cos_t, sin_t, seq=s, tm=512)
        gb = _mla_attn(q2.reshape(b, s, -1), k2.reshape(b, s, -1), v2.reshape(b, s, -1), u3)
        kvm = _norm_matmul(mem2, g_mem[l], w_mem_kv[l].astype(BF16), kv_scale, tm=b * m, tn=2 * MEM_W)
        gm = _mem_attn(u3, kvm.reshape(b, m, 2 * MEM_W))
        xs = _merge(ga.reshape(t, MOBA_W), gb.reshape(t, MLA_W), gm.reshape(t, MEM_W), u, xs,
                    w_p_moba[l].astype(BF16), w_p_mla[l].astype(BF16), w_p_mem[l].astype(BF16),
                    w_out[l].astype(BF16), g_final, tm=256, final_norm=(l == depth - 1))
    return xs.reshape(b, s, d)
```

```python
import functools
import math

import jax
import jax.numpy as jnp
from jax import lax
from jax.experimental import pallas as pl
from jax.experimental.pallas import tpu as pltpu

D_MODEL = 2048
MOBA_HEADS = 8
HEAD_DIM = 128
MOBA_BLOCK = 256
MOBA_TOPK = 3
MLA_HEADS = 8
MLA_Q_LORA = 512
MLA_KV_LORA = 256
MLA_NOPE = 128
MLA_ROPE = 64
MLA_V = 128
ROPE_THETA = 10000.0
MEM_HEADS = 4
MEM_HEAD_DIM = 128
N_BUCKETS = 32
MAX_DISTANCE = 128
EPS = 1e-6

MOBA_W = MOBA_HEADS * HEAD_DIM
MLA_W = MLA_HEADS * MLA_V
MEM_W = MEM_HEADS * MEM_HEAD_DIM

LANES = 128
MXU_DIM = 256
V7X_VMEM_BYTES = 64 * 1024 * 1024

BF16 = jnp.bfloat16
F32 = jnp.float32
LOG2E = 1.4426950408889634

KR_PAD = MXU_DIM
OFF_GL = 0
OFF_QA = OFF_GL + 3 * D_MODEL
OFF_KA = OFF_QA + MOBA_W
OFF_VA = OFF_KA + MOBA_W
OFF_ZA = OFF_VA + MOBA_W
OFF_ZB = OFF_ZA + MOBA_W
OFF_CQ = OFF_ZB + MLA_W
OFF_CKV = OFF_CQ + MLA_Q_LORA
OFF_KR = OFF_CKV + MLA_KV_LORA
OFF_QM = OFF_KR + KR_PAD
OFF_ZM = OFF_QM + MEM_W
IN_WIDTH_P = OFF_ZM + MEM_W

TQ = MOBA_BLOCK
ONES_ROWS = 16
VT_ROWS = MLA_V + ONES_ROWS
MLA_QK = MXU_DIM

W_TILE = 1024


def _vmem_limit(nbytes):
    return int(min(nbytes + (8 << 20), V7X_VMEM_BYTES - (4 << 20)))


def _t5_thresholds():
    max_exact = N_BUCKETS // 2

    def bucket(d):
        if d < max_exact:
            return d
        large = max_exact + int(math.log(d / max_exact) / math.log(MAX_DISTANCE / max_exact)
                                * (N_BUCKETS - max_exact))
        return min(large, N_BUCKETS - 1)

    thr, d = [], 0
    for b in range(1, N_BUCKETS):
        while bucket(d) < b:
            d += 1
        thr.append(d)
    return tuple(thr)


T5_THRESHOLDS = _t5_thresholds()
assert T5_THRESHOLDS[-1] <= MOBA_BLOCK + 1


def _norm_rows(x_ref, g_ref, h_ref, chunk):
    def body(r, carry):
        rows = pl.ds(pl.multiple_of(r * chunk, chunk), chunk)
        xv = x_ref[rows, :]
        ms = jnp.mean(xv * xv, axis=-1, keepdims=True)
        h_ref[rows, :] = (xv * lax.rsqrt(ms + EPS) * g_ref[...]).astype(BF16)
        return carry
    lax.fori_loop(0, x_ref.shape[0] // chunk, body, 0, unroll=4)


def _norm_matmul_kernel(x_ref, g_ref, w_ref, cs_ref, o_ref, h_ref, *, chunk):
    pl.when(pl.program_id(1) == 0)(functools.partial(_norm_rows, x_ref, g_ref, h_ref, chunk))
    acc = jnp.dot(h_ref[...], w_ref[...], preferred_element_type=F32)
    o_ref[...] = (acc * cs_ref[...]).astype(o_ref.dtype)


def _in_proj_kernel(src_ref, x_ref, g_ref, wt_ref, cs_ref, o_ref, h_ref, *, chunk):
    del src_ref
    pl.when(pl.program_id(1) == 0)(functools.partial(_norm_rows, x_ref, g_ref, h_ref, chunk))
    acc = lax.dot_general(h_ref[...], wt_ref[...].astype(BF16), (((1,), (1,)), ((), ())),
                          preferred_element_type=F32)
    o_ref[...] = (acc * cs_ref[...]).astype(o_ref.dtype)


def _in_proj(x, g, wt_all, layer, col_scale, *, tm):
    t, d = x.shape
    srcs = _w_in_tile_sources(wt_all.shape[1])
    unit = 64
    assert all(src % unit == 0 for src in srcs)
    src_units = jnp.asarray([src // unit for src in srcs], jnp.int32)
    est = 2 * tm * d * 4 + tm * d * 2 + 2 * W_TILE * d * 4 + W_TILE * d * 2 + 2 * tm * W_TILE * 2 + tm * W_TILE * 4
    return pl.pallas_call(
        functools.partial(_in_proj_kernel, chunk=64),
        grid_spec=pltpu.PrefetchScalarGridSpec(
            num_scalar_prefetch=1,
            grid=(t // tm, len(srcs)),
            in_specs=[
                pl.BlockSpec((tm, d), lambda i, j, src: (i, 0)),
                pl.BlockSpec((1, d), lambda i, j, src: (0, 0)),
                pl.BlockSpec((pl.Squeezed(), pl.Element(W_TILE), pl.Element(d)),
                             lambda i, j, src: (layer, src[j] * unit, 0)),
                pl.BlockSpec((1, W_TILE), lambda i, j, src: (0, j)),
            ],
            out_specs=pl.BlockSpec((tm, W_TILE), lambda i, j, src: (i, j)),
            scratch_shapes=[pltpu.VMEM((tm, d), BF16)],
        ),
        out_shape=jax.ShapeDtypeStruct((t, IN_WIDTH_P), BF16),
        compiler_params=pltpu.CompilerParams(
            dimension_semantics=("arbitrary", "arbitrary"), vmem_limit_bytes=_vmem_limit(est)),
        name="in_proj",
    )(src_units, x, g.reshape(1, d), wt_all, col_scale)


def _norm_matmul(x, g, w, col_scale, *, tm, tn):
    t, d = x.shape
    n = w.shape[1]
    est = 2 * tm * d * 4 + tm * d * 2 + 2 * d * tn * 2 + 2 * tm * tn * 2 + tm * tn * 4
    return pl.pallas_call(
        functools.partial(_norm_matmul_kernel, chunk=64),
        grid=(t // tm, n // tn),
        in_specs=[
            pl.BlockSpec((tm, d), lambda i, j: (i, 0)),
            pl.BlockSpec((1, d), lambda i, j: (0, 0)),
            pl.BlockSpec((d, tn), lambda i, j: (0, j)),
            pl.BlockSpec((1, tn), lambda i, j: (0, j)),
        ],
        out_specs=pl.BlockSpec((tm, tn), lambda i, j: (i, j)),
        out_shape=jax.ShapeDtypeStruct((t, n), BF16),
        scratch_shapes=[pltpu.VMEM((tm, d), BF16)],
        compiler_params=pltpu.CompilerParams(
            dimension_semantics=("arbitrary", "arbitrary"), vmem_limit_bytes=_vmem_limit(est)),
        name="norm_matmul",
    )(x, g.reshape(1, d), w, col_scale)


def _bias_tiles_kernel(rb_ref, own_ref, prev_ref):
    h = pl.program_id(0)
    key = lax.broadcasted_iota(jnp.int32, (TQ, TQ), 0)
    qry = lax.broadcasted_iota(jnp.int32, (TQ, TQ), 1)
    d_own = qry - key
    d_prev = d_own + MOBA_BLOCK

    def lookup(dist):
        val = jnp.zeros(dist.shape, F32) + rb_ref[0, h]
        for b in range(1, N_BUCKETS):
            val = jnp.where(dist >= T5_THRESHOLDS[b - 1], rb_ref[b, h], val)
        return val * LOG2E

    own_ref[0] = jnp.where(d_own >= 0, lookup(d_own), -jnp.inf)
    prev_ref[0] = lookup(d_prev)


def _bias_tiles(rel_bias):
    heads = rel_bias.shape[1]
    tile = jax.ShapeDtypeStruct((heads, TQ, TQ), F32)
    spec = pl.BlockSpec((1, TQ, TQ), lambda h: (h, 0, 0))
    return pl.pallas_call(
        _bias_tiles_kernel,
        grid=(heads,),
        in_specs=[pl.BlockSpec(memory_space=pltpu.SMEM)],
        out_specs=[spec, spec],
        out_shape=[tile, tile],
        name="bias_tiles",
    )(rel_bias)


def _build_vt_block(v_ref, vt_ref, j):
    dv = v_ref.shape[-1]
    blk = slice(j * TQ, (j + 1) * TQ)
    vt_ref[0:dv, blk] = v_ref[blk, :].astype(F32).T.astype(BF16)
    row = lax.broadcasted_iota(jnp.int32, (ONES_ROWS, TQ), 0)
    vt_ref[dv:dv + ONES_ROWS, blk] = jnp.where(row == 0, 1.0, 0.0).astype(BF16)


def _attention(n_tiles, q_tile_of, k_block_of, terms_of, prepare, vt_ref, t_bufs, emit):
    nt = (((1,), (1,)), ((), ()))
    dv = vt_ref.shape[0] - ONES_ROWS
    state = {}

    def stage1(i):
        prepare(i)
        q = q_tile_of(i)
        adds, consts, sels = terms_of(i, q)
        st = state[i] = dict(consts=consts, sels=sels, m=None, acc=None)
        t_buf = t_bufs[i % 2]

        def item(j):
            blk = slice(j * TQ, (j + 1) * TQ)
            t = lax.dot_general(k_block_of(j), q, nt, preferred_element_type=F32)
            if adds[j] is not None:
                t = t + adds[j][...]
            t_buf[blk, :] = t
            mj = jnp.max(t, axis=0, keepdims=True) + consts[j]
            if sels[j] is not None:
                mj = jnp.where(sels[j], mj, -jnp.inf)
            st["m"] = mj if st["m"] is None else jnp.maximum(st["m"], mj)
        return [functools.partial(item, j) for j in range(len(consts))]

    def stage2(i):
        st = state.pop(i)
        t_buf = t_bufs[i % 2]

        def item(j):
            blk = slice(j * TQ, (j + 1) * TQ)
            off = st["m"] - st["consts"][j]
            if st["sels"][j] is not None:
                off = jnp.where(st["sels"][j], off, jnp.inf)
            p = jnp.exp2(t_buf[blk, :] - off).astype(BF16)
            part = jnp.dot(vt_ref[:, blk], p, preferred_element_type=F32)
            st["acc"] = part if st["acc"] is None else st["acc"] + part

        def finish():
            acc = st["acc"]
            emit(i, (acc[0:dv, :] / acc[dv:dv + 1, :]).T)
        return [functools.partial(item, j) for j in range(len(st["consts"]))], finish

    for item in stage1(0):
        item()
        yield
    for i in range(n_tiles):
        ahead = stage1(i + 1) if i + 1 < n_tiles else []
        behind, finish = stage2(i)
        for k in range(max(len(ahead), len(behind))):
            if k < len(ahead):
                ahead[k]()
            if k < len(behind):
                behind[k]()
            yield
        finish()


def _run_streams(streams):
    active = list(streams)
    while active:
        for stream in list(active):
            if next(stream, StopIteration) is StopIteration:
                active.remove(stream)


def _silu_gate(o, z):
    zf = z.astype(F32)
    return (o * (zf / (1.0 + jnp.exp(-zf)))).astype(BF16)


def _write_causal_tile(mask_ref):
    key = lax.broadcasted_iota(jnp.int32, (TQ, TQ), 0)
    qry = lax.broadcasted_iota(jnp.int32, (TQ, TQ), 1)
    mask_ref[...] = jnp.where(key <= qry, 0.0, -jnp.inf).astype(F32)


def _head_view(ref, head, width):
    return ref.at[0, :, head * width:(head + 1) * width]


def _moba_kernel(rb_ref, q_ref, k_ref, v_ref, z_ref, own_ref, prev_ref, o_ref, *scratch, heads):
    per_head = len(scratch) // heads
    _run_streams([
        _moba_head(rb_ref, pl.program_id(1) * heads + s, _head_view(q_ref, s, HEAD_DIM),
                   _head_view(k_ref, s, HEAD_DIM), _head_view(v_ref, s, HEAD_DIM),
                   _head_view(z_ref, s, HEAD_DIM), own_ref.at[s], prev_ref.at[s],
                   _head_view(o_ref, s, HEAD_DIM), *scratch[s * per_head:(s + 1) * per_head])
        for s in range(heads)])


def _moba_head(rb_ref, head, q_ref, k_ref, v_ref, z_ref, own_ref, prev_ref, o_ref, vt_ref, t0, t1, km_ref):
    seq = q_ref.shape[0]
    n_tiles = seq // TQ
    far_const = rb_ref[N_BUCKETS - 1, head] * LOG2E
    km_ref[...] = jnp.zeros(km_ref.shape, F32)

    def prepare(i):
        _build_vt_block(v_ref, vt_ref, i)
        k_blk = k_ref[i * TQ:(i + 1) * TQ, :].astype(F32)
        km_ref[i:i + 1, :] = jnp.sum(k_blk, axis=0, keepdims=True) * (1.0 / MOBA_BLOCK)

    nt = (((1,), (1,)), ((), ()))

    def terms_of(i, q_tile):
        if i > MOBA_TOPK:
            k_mean = km_ref[...]
            km1 = k_mean.astype(BF16)
            rem = k_mean - km1.astype(F32)
            km2 = rem.astype(BF16)
            km3 = (rem - km2.astype(F32)).astype(BF16)
            gate = (lax.dot_general(km1, q_tile, nt, preferred_element_type=F32)
                    + lax.dot_general(km2, q_tile, nt, preferred_element_type=F32)
                    + lax.dot_general(km3, q_tile, nt, preferred_element_type=F32))
            sels = []
            for j in range(i):
                gj = gate[j:j + 1, :]
                cnt = jnp.zeros(gj.shape, F32)
                for jp in range(i):
                    if jp == j:
                        continue
                    gp = gate[jp:jp + 1, :]
                    beats = (gp >= gj) if jp < j else (gp > gj)
                    cnt = cnt + jnp.where(beats, 1.0, 0.0)
                sels.append(cnt < MOBA_TOPK)
        else:
            sels = [None] * i
        sels.append(None)
        adds = [None] * (i + 1)
        consts = [far_const] * (i + 1)
        adds[i], consts[i] = own_ref, 0.0
        if i >= 1:
            adds[i - 1], consts[i - 1] = prev_ref, 0.0
        return adds, consts, sels

    def emit(i, o):
        rows = slice(i * TQ, (i + 1) * TQ)
        o_ref[rows, :] = _silu_gate(o, z_ref[rows, :])

    return _attention(n_tiles, lambda i: q_ref[i * TQ:(i + 1) * TQ, :],
                      lambda j: k_ref[j * TQ:(j + 1) * TQ, :], terms_of, prepare, vt_ref, (t0, t1), emit)


HEADS_PER_STEP = 4


def _attn_scratch(n_keys, extra=()):
    per_head = [pltpu.VMEM((VT_ROWS, n_keys), BF16), pltpu.VMEM((n_keys, TQ), F32),
                pltpu.VMEM((n_keys, TQ), F32), *extra]
    return per_head * HEADS_PER_STEP


def _moba_attn(u3, rel_bias, own, prev):
    b, s, _ = u3.shape
    hp = HEADS_PER_STEP
    width = hp * HEAD_DIM
    col = lambda off: (lambda bi, h: (bi, 0, off // width + h))
    blk = (1, s, width)
    tile_spec = pl.BlockSpec((hp, TQ, TQ), lambda bi, h: (h, 0, 0))
    est = 10 * s * width * 2 + 4 * hp * TQ * TQ * 4 + hp * (VT_ROWS * s * 2 + 2 * s * TQ * 4)
    return pl.pallas_call(
        functools.partial(_moba_kernel, heads=hp),
        grid=(b, MOBA_HEADS // hp),
        in_specs=[
            pl.BlockSpec(memory_space=pltpu.SMEM),
            pl.BlockSpec(blk, col(OFF_QA)),
            pl.BlockSpec(blk, col(OFF_KA)),
            pl.BlockSpec(blk, col(OFF_VA)),
            pl.BlockSpec(blk, col(OFF_ZA)),
            tile_spec, tile_spec,
        ],
        out_specs=pl.BlockSpec(blk, lambda bi, h: (bi, 0, h)),
        out_shape=jax.ShapeDtypeStruct((b, s, MOBA_W), BF16),
        scratch_shapes=_attn_scratch(s, extra=(pltpu.VMEM((16, HEAD_DIM), F32),)),
        compiler_params=pltpu.CompilerParams(
            dimension_semantics=("arbitrary", "arbitrary"), vmem_limit_bytes=_vmem_limit(est)),
        name="moba_attn",
    )(rel_bias, u3, u3, u3, u3, own, prev)


def _mla_prep_kernel(cq_ref, ckv_ref, kr_ref, gq_ref, gkv_ref, wuq_ref, wukv_ref, cos_ref, sin_ref,
                     q_out, k_out, v_out, *, q_scale):
    def rms(x_ref, g_ref):
        xf = x_ref[...].astype(F32)
        ms = jnp.mean(xf * xf, axis=-1, keepdims=True)
        return (xf * lax.rsqrt(ms + EPS) * g_ref[...]).astype(BF16)

    cos = cos_ref[...]
    sin = sin_ref[...]
    half = MLA_ROPE // 2
    first_half = lax.broadcasted_iota(jnp.int32, cos.shape, 1) < half

    def rope(xr):
        partner = jnp.where(first_half, pltpu.roll(xr, LANES - half, 1), pltpu.roll(xr, half, 1))
        return xr * cos + partner * sin

    qb = jnp.dot(rms(cq_ref, gq_ref), wuq_ref[...], preferred_element_type=F32) * q_scale
    for h in range(MLA_HEADS):
        base = h * MLA_QK
        q_out[:, base:base + MLA_NOPE] = qb[:, base:base + MLA_NOPE].astype(BF16)
        q_out[:, base + MLA_NOPE:base + MLA_QK] = rope(qb[:, base + MLA_NOPE:base + MLA_QK]).astype(BF16)

    kvb = jnp.dot(rms(ckv_ref, gkv_ref), wukv_ref[...], preferred_element_type=F32)
    in_rope = lax.broadcasted_iota(jnp.int32, cos.shape, 1) < MLA_ROPE
    k_rope = rope(jnp.where(in_rope, kr_ref[:, 0:LANES].astype(F32), 0.0)).astype(BF16)
    for h in range(MLA_HEADS):
        base = h * MLA_QK
        k_out[:, base:base + MLA_NOPE] = kvb[:, h * MLA_NOPE:(h + 1) * MLA_NOPE].astype(BF16)
        k_out[:, base + MLA_NOPE:base + MLA_QK] = k_rope
    v_out[...] = kvb[:, MLA_W:2 * MLA_W].astype(BF16)


def _mla_prep(u, g_cq, g_ckv, wuq_p, wukv_p, cos_t, sin_t, *, seq, tm):
    t = u.shape[0]
    s_tiles = seq // tm
    const = lambda i: (0, 0)
    qk_shape = jax.ShapeDtypeStruct((t, MLA_HEADS * MLA_QK), BF16)
    est = (2 * tm * (MLA_Q_LORA + 2 * MLA_KV_LORA) * 2 + 2 * (wuq_p.size + wukv_p.size) * 2
           + 4 * tm * LANES * 4 + 2 * tm * 5 * MLA_W * 2 + 4 * tm * 2 * MLA_W * 4)
    return pl.pallas_call(
        functools.partial(_mla_prep_kernel, q_scale=(MLA_NOPE + MLA_ROPE) ** -0.5 * LOG2E),
        grid=(t // tm,),
        in_specs=[
            pl.BlockSpec((tm, MLA_Q_LORA), lambda i: (i, OFF_CQ // MLA_Q_LORA)),
            pl.BlockSpec((tm, MLA_KV_LORA), lambda i: (i, OFF_CKV // MLA_KV_LORA)),
            pl.BlockSpec((tm, KR_PAD), lambda i: (i, OFF_KR // KR_PAD)),
            pl.BlockSpec((1, MLA_Q_LORA), const),
            pl.BlockSpec((1, MLA_KV_LORA), const),
            pl.BlockSpec(wuq_p.shape, const),
            pl.BlockSpec(wukv_p.shape, const),
            pl.BlockSpec((tm, LANES), lambda i: (i % s_tiles, 0)),
            pl.BlockSpec((tm, LANES), lambda i: (i % s_tiles, 0)),
        ],
        out_specs=[
            pl.BlockSpec((tm, MLA_HEADS * MLA_QK), lambda i: (i, 0)),
            pl.BlockSpec((tm, MLA_HEADS * MLA_QK), lambda i: (i, 0)),
            pl.BlockSpec((tm, MLA_W), lambda i: (i, 0)),
        ],
        out_shape=[qk_shape, qk_shape, jax.ShapeDtypeStruct((t, MLA_W), BF16)],
        compiler_params=pltpu.CompilerParams(
            dimension_semantics=("arbitrary",), vmem_limit_bytes=_vmem_limit(est)),
        name="mla_prep",
    )(u, u, u, g_cq.reshape(1, -1), g_ckv.reshape(1, -1), wuq_p, wukv_p, cos_t, sin_t)


def _mla_attn_kernel(q_ref, k_ref, v_ref, z_ref, o_ref, mask_ref, *scratch, heads):
    _write_causal_tile(mask_ref)
    per_head = len(scratch) // heads
    _run_streams([
        _mla_head(_head_view(q_ref, s, MLA_QK), _head_view(k_ref, s, MLA_QK), _head_view(v_ref, s, MLA_V),
                  _head_view(z_ref, s, MLA_V), _head_view(o_ref, s, MLA_V), mask_ref,
                  *scratch[s * per_head:(s + 1) * per_head])
        for s in range(heads)])


def _mla_head(q_ref, k_ref, v_ref, z_ref, o_ref, mask_ref, vt_ref, t0, t1):
    seq = q_ref.shape[0]

    def terms_of(i, q_tile):
        return [None] * i + [mask_ref], [0.0] * (i + 1), [None] * (i + 1)

    def emit(i, o):
        rows = slice(i * TQ, (i + 1) * TQ)
        o_ref[rows, :] = _silu_gate(o, z_ref[rows, :])

    return _attention(seq // TQ, lambda i: q_ref[i * TQ:(i + 1) * TQ, :],
                      lambda j: k_ref[j * TQ:(j + 1) * TQ, :], terms_of,
                      functools.partial(_build_vt_block, v_ref, vt_ref), vt_ref, (t0, t1), emit)


def _mla_attn(q3, k3, v3, u3):
    b, s, _ = q3.shape
    hp = HEADS_PER_STEP
    est = (4 * s * hp * MLA_QK * 2 + 6 * s * hp * MLA_V * 2 + hp * (VT_ROWS * s * 2 + 2 * s * TQ * 4)
           + TQ * TQ * 4)
    return pl.pallas_call(
        functools.partial(_mla_attn_kernel, heads=hp),
        grid=(b, MLA_HEADS // hp),
        in_specs=[
            pl.BlockSpec((1, s, hp * MLA_QK), lambda bi, h: (bi, 0, h)),
            pl.BlockSpec((1, s, hp * MLA_QK), lambda bi, h: (bi, 0, h)),
            pl.BlockSpec((1, s, hp * MLA_V), lambda bi, h: (bi, 0, h)),
            pl.BlockSpec((1, s, hp * MLA_V), lambda bi, h: (bi, 0, OFF_ZB // (hp * MLA_V) + h)),
        ],
        out_specs=pl.BlockSpec((1, s, hp * MLA_V), lambda bi, h: (bi, 0, h)),
        out_shape=jax.ShapeDtypeStruct((b, s, MLA_W), BF16),
        scratch_shapes=[pltpu.VMEM((TQ, TQ), F32)] + _attn_scratch(s),
        compiler_params=pltpu.CompilerParams(
            dimension_semantics=("arbitrary", "arbitrary"), vmem_limit_bytes=_vmem_limit(est)),
        name="mla_attn",
    )(q3, k3, v3, u3)


def _mem_attn_kernel(q_ref, k_ref, v_ref, z_ref, o_ref, *scratch, heads):
    assert k_ref.shape[1] == TQ
    per_head = len(scratch) // heads
    d = MEM_HEAD_DIM
    _run_streams([
        _mem_head(_head_view(q_ref, s, d), _head_view(k_ref, s, d), _head_view(v_ref, s, d),
                  _head_view(z_ref, s, d), _head_view(o_ref, s, d), *scratch[s * per_head:(s + 1) * per_head])
        for s in range(heads)])


def _mem_head(q_ref, k_ref, v_ref, z_ref, o_ref, vt_ref, t0, t1):
    seq = q_ref.shape[0]

    def prepare(i):
        if i == 0:
            _build_vt_block(v_ref, vt_ref, 0)

    def emit(i, o):
        rows = slice(i * TQ, (i + 1) * TQ)
        o_ref[rows, :] = _silu_gate(o, z_ref[rows, :])

    return _attention(seq // TQ, lambda i: q_ref[i * TQ:(i + 1) * TQ, :], lambda j: k_ref[...],
                      lambda i, q_tile: ([None], [0.0], [None]), prepare, vt_ref, (t0, t1), emit)


def _mem_attn(u3, kv3):
    b, s, _ = u3.shape
    m = kv3.shape[1]
    hp = HEADS_PER_STEP
    d = hp * MEM_HEAD_DIM
    est = 6 * s * d * 2 + 4 * m * d * 2 + hp * (VT_ROWS * m * 2 + 2 * m * TQ * 4)
    return pl.pallas_call(
        functools.partial(_mem_attn_kernel, heads=hp),
        grid=(b, MEM_HEADS // hp),
        in_specs=[
            pl.BlockSpec((1, s, d), lambda bi, h: (bi, 0, OFF_QM // d + h)),
            pl.BlockSpec((1, m, d), lambda bi, h: (bi, 0, h)),
            pl.BlockSpec((1, m, d), lambda bi, h: (bi, 0, MEM_W // d + h)),
            pl.BlockSpec((1, s, d), lambda bi, h: (bi, 0, OFF_ZM // d + h)),
        ],
        out_specs=pl.BlockSpec((1, s, d), lambda bi, h: (bi, 0, h)),
        out_shape=jax.ShapeDtypeStruct((b, s, MEM_W), BF16),
        scratch_shapes=_attn_scratch(m),
        compiler_params=pltpu.CompilerParams(
            dimension_semantics=("arbitrary", "arbitrary"), vmem_limit_bytes=_vmem_limit(est)),
        name="mem_attn",
    )(u3, kv3, kv3, u3)


def _merge_kernel(ga_ref, gb_ref, gm_ref, gl_ref, x_ref, wpa_ref, wpb_ref, wpm_ref, wout_ref, gf_ref,
                  o_ref, *, final_norm):
    d = x_ref.shape[1]

    def gated(idx, g_ref, w_ref):
        logit = gl_ref[:, idx * d:(idx + 1) * d].astype(F32)
        return jnp.dot(g_ref[...], w_ref[...], preferred_element_type=F32) / (1.0 + jnp.exp(-logit))

    y = gated(0, ga_ref, wpa_ref) + gated(1, gb_ref, wpb_ref) + gated(2, gm_ref, wpm_ref)
    r = x_ref[...] + jnp.dot(y.astype(BF16), wout_ref[...], preferred_element_type=F32)
    if final_norm:
        ms = jnp.mean(r * r, axis=-1, keepdims=True)
        r = r * lax.rsqrt(ms + EPS) * gf_ref[...]
    o_ref[...] = r


def _merge(ga, gb, gm, u, x, wpa, wpb, wpm, wout, g_final, *, tm, final_norm):
    t, d = x.shape
    const = lambda i: (0, 0)
    resident = lambda w: pl.BlockSpec(w.shape, const, pipeline_mode=pl.Buffered(1))
    rows = lambda width: pl.BlockSpec((tm, width), lambda i: (i, 0))
    w_bytes = (wpa.size + wpb.size + wpm.size + wout.size) * 2
    est = w_bytes + 2 * tm * (2 * MOBA_W + MEM_W + 3 * d) * 2 + 4 * tm * d * 4 + 6 * tm * d * 4
    return pl.pallas_call(
        functools.partial(_merge_kernel, final_norm=final_norm),
        grid=(t // tm,),
        in_specs=[
            rows(MOBA_W), rows(MLA_W), rows(MEM_W),
            pl.BlockSpec((tm, 3 * d), lambda i: (i, OFF_GL // (3 * d))),
            rows(d),
            resident(wpa), resident(wpb), resident(wpm), resident(wout),
            pl.BlockSpec((1, d), const),
        ],
        out_specs=rows(d),
        out_shape=jax.ShapeDtypeStruct((t, d), F32),
        compiler_params=pltpu.CompilerParams(
            dimension_semantics=("arbitrary",), vmem_limit_bytes=_vmem_limit(est)),
        name="merge",
    )(ga, gb, gm, u, x, wpa, wpb, wpm, wout, g_final.reshape(1, d))


def _w_in_tile_sources(n_cols):
    o_cq = 4 * MOBA_W
    o_zb = o_cq + MLA_Q_LORA + MLA_KV_LORA + MLA_ROPE
    o_qm = o_zb + MLA_W
    o_gl = o_qm + 2 * MEM_W
    srcs = ([o_gl + W_TILE * k for k in range(3 * D_MODEL // W_TILE)]
            + [W_TILE * k for k in range(4 * MOBA_W // W_TILE)] + [o_zb, o_cq, o_qm])
    assert len(srcs) * W_TILE == IN_WIDTH_P and o_gl + 3 * D_MODEL == n_cols
    assert all(src + W_TILE <= n_cols for src in srcs)
    return srcs


def _in_col_scale():
    cs = jnp.ones((1, IN_WIDTH_P), F32)
    cs = cs.at[:, OFF_QA:OFF_QA + MOBA_W].set(HEAD_DIM ** -0.5 * LOG2E)
    return cs.at[:, OFF_QM:OFF_QM + MEM_W].set(MEM_HEAD_DIM ** -0.5 * LOG2E)


def _regroup_w_uq(w):
    r = w.shape[0]
    w3 = w.reshape(r, MLA_HEADS, MLA_NOPE + MLA_ROPE)
    pad = jnp.zeros((r, MLA_HEADS, MLA_QK - MLA_NOPE - MLA_ROPE), w.dtype)
    return jnp.concatenate([w3, pad], axis=-1).reshape(r, MLA_HEADS * MLA_QK).astype(BF16)


def _regroup_w_ukv(w):
    r = w.shape[0]
    w3 = w.reshape(r, MLA_HEADS, MLA_NOPE + MLA_V)
    return jnp.concatenate([w3[:, :, :MLA_NOPE].reshape(r, MLA_W),
                            w3[:, :, MLA_NOPE:].reshape(r, MLA_W)], axis=1).astype(BF16)


def _rope_tables(seq):
    half = MLA_ROPE // 2
    inv = ROPE_THETA ** (-jnp.arange(half, dtype=F32) / half)
    ang = jnp.arange(seq, dtype=jnp.int32).astype(F32)[:, None] * inv[None, :]
    cos, sin = jnp.cos(ang), jnp.sin(ang)
    pad = LANES - MLA_ROPE
    cos_t = jnp.concatenate([cos, cos, jnp.ones((seq, pad), F32)], axis=1)
    sin_t = jnp.concatenate([-sin, sin, jnp.zeros((seq, pad), F32)], axis=1)
    return cos_t, sin_t


def kernel(x, mem, g_norm, w_in, g_cq, w_uq, g_ckv, w_ukv, g_mem, w_mem_kv, rel_bias,
           w_p_moba, w_p_mla, w_p_mem, w_out, g_final):
    b, s, d = x.shape
    m = mem.shape[1]
    depth = w_in.shape[0]
    t = b * s
    assert d == D_MODEL and s % TQ == 0 and m == TQ

    own, prev = _bias_tiles(rel_bias)
    cos_t, sin_t = _rope_tables(s)
    mem2 = mem.reshape(b * m, d)
    xs = x.reshape(t, d)
    in_scale = _in_col_scale()
    w_in_t = jnp.swapaxes(w_in, 1, 2)
    kv_scale = jnp.ones((1, 2 * MEM_W), F32)
    for l in range(depth):
        u = _in_proj(xs, g_norm[l], w_in_t, l, in_scale, tm=1024)
        u3 = u.reshape(b, s, IN_WIDTH_P)
        ga = _moba_attn(u3, rel_bias, own, prev)
        q2, k2, v2 = _mla_prep(u, g_cq[l], g_ckv[l], _regroup_w_uq(w_uq[l]), _regroup_w_ukv(w_ukv[l]),
                               cos_t, sin_t, seq=s, tm=512)
        gb = _mla_attn(q2.reshape(b, s, -1), k2.reshape(b, s, -1), v2.reshape(b, s, -1), u3)
        kvm = _norm_matmul(mem2, g_mem[l], w_mem_kv[l].astype(BF16), kv_scale, tm=b * m, tn=2 * MEM_W)
        gm = _mem_attn(u3, kvm.reshape(b, m, 2 * MEM_W))
        xs = _merge(ga.reshape(t, MOBA_W), gb.reshape(t, MLA_W), gm.reshape(t, MEM_W), u, xs,
                    w_p_moba[l].astype(BF16), w_p_mla[l].astype(BF16), w_p_mem[l].astype(BF16),
                    w_out[l].astype(BF16), g_final, tm=256, final_norm=(l == depth - 1))
    return xs.reshape(b, s, d)
```

```python
import functools
import math

import jax
import jax.numpy as jnp
from jax import lax
from jax.experimental import pallas as pl
from jax.experimental.pallas import tpu as pltpu

D_MODEL = 2048
MOBA_HEADS = 8
HEAD_DIM = 128
MOBA_BLOCK = 256
MOBA_TOPK = 3
MLA_HEADS = 8
MLA_Q_LORA = 512
MLA_KV_LORA = 256
MLA_NOPE = 128
MLA_ROPE = 64
MLA_V = 128
ROPE_THETA = 10000.0
MEM_HEADS = 4
MEM_HEAD_DIM = 128
N_BUCKETS = 32
MAX_DISTANCE = 128
EPS = 1e-6

MOBA_W = MOBA_HEADS * HEAD_DIM
MLA_W = MLA_HEADS * MLA_V
MEM_W = MEM_HEADS * MEM_HEAD_DIM

LANES = 128
MXU_DIM = 256
V7X_VMEM_BYTES = 64 * 1024 * 1024

BF16 = jnp.bfloat16
F32 = jnp.float32
LOG2E = 1.4426950408889634

KR_PAD = MXU_DIM
OFF_GL = 0
OFF_QA = OFF_GL + 3 * D_MODEL
OFF_KA = OFF_QA + MOBA_W
OFF_VA = OFF_KA + MOBA_W
OFF_ZA = OFF_VA + MOBA_W
OFF_ZB = OFF_ZA + MOBA_W
OFF_CQ = OFF_ZB + MLA_W
OFF_CKV = OFF_CQ + MLA_Q_LORA
OFF_KR = OFF_CKV + MLA_KV_LORA
OFF_QM = OFF_KR + KR_PAD
OFF_ZM = OFF_QM + MEM_W
IN_WIDTH_P = OFF_ZM + MEM_W

TQ = MOBA_BLOCK
ONES_ROWS = 16
VT_ROWS = MLA_V + ONES_ROWS
MLA_QK = MXU_DIM

W_TILE = 1024


def _vmem_limit(nbytes):
    return int(min(nbytes + (8 << 20), V7X_VMEM_BYTES - (4 << 20)))


def _t5_thresholds():
    max_exact = N_BUCKETS // 2

    def bucket(d):
        if d < max_exact:
            return d
        large = max_exact + int(math.log(d / max_exact) / math.log(MAX_DISTANCE / max_exact)
                                * (N_BUCKETS - max_exact))
        return min(large, N_BUCKETS - 1)

    thr, d = [], 0
    for b in range(1, N_BUCKETS):
        while bucket(d) < b:
            d += 1
        thr.append(d)
    return tuple(thr)


T5_THRESHOLDS = _t5_thresholds()
assert T5_THRESHOLDS[-1] <= MOBA_BLOCK + 1


def _norm_rows(x_ref, g_ref, h_ref, chunk):
    def body(r, carry):
        rows = pl.ds(pl.multiple_of(r * chunk, chunk), chunk)
        xv = x_ref[rows, :]
        ms = jnp.mean(xv * xv, axis=-1, keepdims=True)
        h_ref[rows, :] = (xv * lax.rsqrt(ms + EPS) * g_ref[...]).astype(BF16)
        return carry
    lax.fori_loop(0, x_ref.shape[0] // chunk, body, 0, unroll=4)


def _norm_matmul_kernel(x_ref, g_ref, w_ref, o_ref, h_ref, *, chunk):
    pl.when(pl.program_id(1) == 0)(functools.partial(_norm_rows, x_ref, g_ref, h_ref, chunk))
    o_ref[...] = jnp.dot(h_ref[...], w_ref[...].astype(BF16), preferred_element_type=F32).astype(o_ref.dtype)


def _in_proj_kernel(src_ref, x_ref, g_ref, wt_ref, cs_ref, o_ref, h_ref, *, chunk):
    del src_ref
    pl.when(pl.program_id(1) == 0)(functools.partial(_norm_rows, x_ref, g_ref, h_ref, chunk))
    acc = lax.dot_general(h_ref[...], wt_ref[...].astype(BF16), (((1,), (1,)), ((), ())),
                          preferred_element_type=F32)
    o_ref[...] = (acc * cs_ref[...]).astype(o_ref.dtype)


def _in_proj(x, g, wt_all, layer, col_scale, *, tm):
    t, d = x.shape
    srcs = _w_in_tile_sources(wt_all.shape[1])
    unit = 64
    assert all(src % unit == 0 for src in srcs)
    src_units = jnp.asarray([src // unit for src in srcs], jnp.int32)
    est = 2 * tm * d * 4 + tm * d * 2 + 2 * W_TILE * d * 4 + W_TILE * d * 2 + 2 * tm * W_TILE * 2 + tm * W_TILE * 4
    return pl.pallas_call(
        functools.partial(_in_proj_kernel, chunk=64),
        grid_spec=pltpu.PrefetchScalarGridSpec(
            num_scalar_prefetch=1,
            grid=(t // tm, len(srcs)),
            in_specs=[
                pl.BlockSpec((tm, d), lambda i, j, src: (i, 0)),
                pl.BlockSpec((1, d), lambda i, j, src: (0, 0)),
                pl.BlockSpec((pl.Squeezed(), pl.Element(W_TILE), pl.Element(d)),
                             lambda i, j, src: (layer, src[j] * unit, 0)),
                pl.BlockSpec((1, W_TILE), lambda i, j, src: (0, j)),
            ],
            out_specs=pl.BlockSpec((tm, W_TILE), lambda i, j, src: (i, j)),
            scratch_shapes=[pltpu.VMEM((tm, d), BF16)],
        ),
        out_shape=jax.ShapeDtypeStruct((t, IN_WIDTH_P), BF16),
        compiler_params=pltpu.CompilerParams(
            dimension_semantics=("arbitrary", "arbitrary"), vmem_limit_bytes=_vmem_limit(est)),
        name="in_proj",
    )(src_units, x, g.reshape(1, d), wt_all, col_scale)


def _norm_matmul(x, g, w_all, layer, *, tm, tn):
    t, d = x.shape
    n = w_all.shape[2]
    est = 2 * tm * d * 4 + tm * d * 2 + 2 * d * tn * 4 + d * tn * 2 + 2 * tm * tn * 2 + tm * tn * 4
    return pl.pallas_call(
        functools.partial(_norm_matmul_kernel, chunk=64),
        grid=(t // tm, n // tn),
        in_specs=[
            pl.BlockSpec((tm, d), lambda i, j: (i, 0)),
            pl.BlockSpec((1, d), lambda i, j: (0, 0)),
            pl.BlockSpec((pl.Squeezed(), d, tn), lambda i, j: (layer, 0, j)),
        ],
        out_specs=pl.BlockSpec((tm, tn), lambda i, j: (i, j)),
        out_shape=jax.ShapeDtypeStruct((t, n), BF16),
        scratch_shapes=[pltpu.VMEM((tm, d), BF16)],
        compiler_params=pltpu.CompilerParams(
            dimension_semantics=("arbitrary", "arbitrary"), vmem_limit_bytes=_vmem_limit(est)),
        name="norm_matmul",
    )(x, g.reshape(1, d), w_all)


def _bias_tiles_kernel(rb_ref, own_ref, prev_ref):
    h = pl.program_id(0)
    key = lax.broadcasted_iota(jnp.int32, (TQ, TQ), 0)
    qry = lax.broadcasted_iota(jnp.int32, (TQ, TQ), 1)
    d_own = qry - key
    d_prev = d_own + MOBA_BLOCK

    def lookup(dist):
        val = jnp.zeros(dist.shape, F32) + rb_ref[0, h]
        for b in range(1, N_BUCKETS):
            val = jnp.where(dist >= T5_THRESHOLDS[b - 1], rb_ref[b, h], val)
        return val * LOG2E

    own_ref[0] = jnp.where(d_own >= 0, lookup(d_own), -jnp.inf)
    prev_ref[0] = lookup(d_prev)


def _bias_tiles(rel_bias):
    heads = rel_bias.shape[1]
    tile = jax.ShapeDtypeStruct((heads, TQ, TQ), F32)
    spec = pl.BlockSpec((1, TQ, TQ), lambda h: (h, 0, 0))
    return pl.pallas_call(
        _bias_tiles_kernel,
        grid=(heads,),
        in_specs=[pl.BlockSpec(memory_space=pltpu.SMEM)],
        out_specs=[spec, spec],
        out_shape=[tile, tile],
        name="bias_tiles",
    )(rel_bias)


def _build_vt_block(v_ref, vt_ref, j):
    dv = v_ref.shape[-1]
    blk = slice(j * TQ, (j + 1) * TQ)
    vt_ref[0:dv, blk] = v_ref[blk, :].astype(F32).T.astype(BF16)
    row = lax.broadcasted_iota(jnp.int32, (ONES_ROWS, TQ), 0)
    vt_ref[dv:dv + ONES_ROWS, blk] = jnp.where(row == 0, 1.0, 0.0).astype(BF16)


def _attention(n_tiles, q_tile_of, k_block_of, terms_of, prepare, vt_ref, bufs, emit, reduce_from_buffer):
    nt = (((1,), (1,)), ((), ()))
    dv = vt_ref.shape[0] - ONES_ROWS
    t_bufs, p_buf = bufs[:2], bufs[2]
    state = {}

    def stage1(i):
        prepare(i)
        q = q_tile_of(i)
        adds, consts, sels = terms_of(i, q)
        st = state[i] = dict(consts=consts, sels=sels, m=None)
        t_buf = t_bufs[i % 2]

        def item(j):
            blk = slice(j * TQ, (j + 1) * TQ)
            t = lax.dot_general(k_block_of(j), q, nt, preferred_element_type=F32)
            if adds[j] is not None:
                t = t + adds[j][...]
            t_buf[blk, :] = t
            if reduce_from_buffer:
                zero = jnp.minimum(pl.program_id(0), 0)
                t = t_buf[pl.ds(pl.multiple_of(j * TQ + zero * TQ, TQ), TQ), :]
            mj = jnp.max(t, axis=0, keepdims=True) + consts[j]
            if sels[j] is not None:
                mj = jnp.where(sels[j], mj, -jnp.inf)
            st["m"] = mj if st["m"] is None else jnp.maximum(st["m"], mj)
        return [functools.partial(item, j) for j in range(len(consts))]

    def stage2(i):
        st = state.pop(i)
        t_buf = t_bufs[i % 2]

        def item(j):
            blk = slice(j * TQ, (j + 1) * TQ)
            off = st["m"] - st["consts"][j]
            if st["sels"][j] is not None:
                off = jnp.where(st["sels"][j], off, jnp.inf)
            p_buf[blk, :] = jnp.exp2(t_buf[blk, :] - off).astype(BF16)

        def finish():
            n_keys = len(st["consts"]) * TQ
            acc = jnp.dot(vt_ref[:, 0:n_keys], p_buf[0:n_keys, :], preferred_element_type=F32)
            emit(i, (acc[0:dv, :] / acc[dv:dv + 1, :]).T)
        return [functools.partial(item, j) for j in range(len(st["consts"]))], finish

    for item in stage1(0):
        item()
        yield
    for i in range(n_tiles):
        ahead = stage1(i + 1) if i + 1 < n_tiles else []
        behind, finish = stage2(i)
        for k in range(max(len(ahead), len(behind))):
            if k < len(ahead):
                ahead[k]()
            if k < len(behind):
                behind[k]()
            yield
        finish()


def _run_streams(streams):
    active = list(streams)
    while active:
        for stream in list(active):
            if next(stream, StopIteration) is StopIteration:
                active.remove(stream)


def _cast_plan(weights, layer, n_b, n_h):
    n_steps = n_b * n_h
    ins, outs, shapes, nbytes = [], [], [], 0
    for w in weights:
        _, rows, cols = w.shape
        assert rows % (n_steps * 16) == 0
        blk_rows = rows // n_steps
        ins.append(pl.BlockSpec((pl.Squeezed(), blk_rows, cols), lambda bi, h: (layer, bi * n_h + h, 0)))
        outs.append(pl.BlockSpec((blk_rows, cols), lambda bi, h: (bi * n_h + h, 0)))
        shapes.append(jax.ShapeDtypeStruct((rows, cols), BF16))
        nbytes += 2 * blk_rows * cols * (4 + 2)
    return ins, outs, shapes, nbytes


def _cast_row_blocks(rest, n_cast):
    cast_in, (o_ref, *cast_out), scratch = rest[:n_cast], rest[n_cast:2 * n_cast + 1], rest[2 * n_cast + 1:]
    for w_ref, wb_ref in zip(cast_in, cast_out):
        wb_ref[...] = w_ref[...].astype(BF16)
    return o_ref, scratch


def _silu_gate(o, z):
    zf = z.astype(F32)
    return (o * (zf / (1.0 + jnp.exp(-zf)))).astype(BF16)


def _write_causal_tile(mask_ref):
    key = lax.broadcasted_iota(jnp.int32, (TQ, TQ), 0)
    qry = lax.broadcasted_iota(jnp.int32, (TQ, TQ), 1)
    mask_ref[...] = jnp.where(key <= qry, 0.0, -jnp.inf).astype(F32)


def _head_view(ref, head, width):
    return ref.at[0, :, head * width:(head + 1) * width]


def _moba_kernel(rb_ref, q_ref, k_ref, v_ref, z_ref, own_ref, prev_ref, *rest, heads, n_cast):
    o_ref, scratch = _cast_row_blocks(rest, n_cast)
    per_head = len(scratch) // heads
    _run_streams([
        _moba_head(rb_ref, pl.program_id(1) * heads + s, _head_view(q_ref, s, HEAD_DIM),
                   _head_view(k_ref, s, HEAD_DIM), _head_view(v_ref, s, HEAD_DIM),
                   _head_view(z_ref, s, HEAD_DIM), own_ref.at[s], prev_ref.at[s],
                   _head_view(o_ref, s, HEAD_DIM), *scratch[s * per_head:(s + 1) * per_head])
        for s in range(heads)])


def _moba_head(rb_ref, head, q_ref, k_ref, v_ref, z_ref, own_ref, prev_ref, o_ref, vt_ref, t0, t1, pb, km_ref):
    seq = q_ref.shape[0]
    n_tiles = seq // TQ
    far_const = rb_ref[N_BUCKETS - 1, head] * LOG2E
    km_ref[...] = jnp.zeros(km_ref.shape, F32)

    def prepare(i):
        _build_vt_block(v_ref, vt_ref, i)
        k_blk = k_ref[i * TQ:(i + 1) * TQ, :].astype(F32)
        km_ref[i:i + 1, :] = jnp.sum(k_blk, axis=0, keepdims=True) * (1.0 / MOBA_BLOCK)

    nt = (((1,), (1,)), ((), ()))

    def terms_of(i, q_tile):
        if i > MOBA_TOPK:
            k_mean = km_ref[...]
            km1 = k_mean.astype(BF16)
            rem = k_mean - km1.astype(F32)
            km2 = rem.astype(BF16)
            km3 = (rem - km2.astype(F32)).astype(BF16)
            gate = (lax.dot_general(km1, q_tile, nt, preferred_element_type=F32)
                    + lax.dot_general(km2, q_tile, nt, preferred_element_type=F32)
                    + lax.dot_general(km3, q_tile, nt, preferred_element_type=F32))
            sels = []
            for j in range(i):
                gj = gate[j:j + 1, :]
                cnt = jnp.zeros(gj.shape, F32)
                for jp in range(i):
                    if jp == j:
                        continue
                    gp = gate[jp:jp + 1, :]
                    beats = (gp >= gj) if jp < j else (gp > gj)
                    cnt = cnt + jnp.where(beats, 1.0, 0.0)
                sels.append(cnt < MOBA_TOPK)
        else:
            sels = [None] * i
        sels.append(None)
        adds = [None] * (i + 1)
        consts = [far_const] * (i + 1)
        adds[i], consts[i] = own_ref, 0.0
        if i >= 1:
            adds[i - 1], consts[i - 1] = prev_ref, 0.0
        return adds, consts, sels

    def emit(i, o):
        rows = slice(i * TQ, (i + 1) * TQ)
        o_ref[rows, :] = _silu_gate(o, z_ref[rows, :])

    return _attention(n_tiles, lambda i: q_ref[i * TQ:(i + 1) * TQ, :],
                      lambda j: k_ref[j * TQ:(j + 1) * TQ, :], terms_of, prepare, vt_ref, (t0, t1, pb), emit,
                      reduce_from_buffer=True)


HEADS_PER_STEP = 4


def _attn_scratch(n_keys, extra=()):
    per_head = [pltpu.VMEM((VT_ROWS, n_keys), BF16), pltpu.VMEM((n_keys, TQ), F32),
                pltpu.VMEM((n_keys, TQ), F32), pltpu.VMEM((n_keys, TQ), BF16), *extra]
    return per_head * HEADS_PER_STEP


def _moba_attn(u3, rel_bias, own, prev, cast_weights, layer):
    b, s, _ = u3.shape
    hp = HEADS_PER_STEP
    width = hp * HEAD_DIM
    n_h = MOBA_HEADS // hp
    col = lambda off: (lambda bi, h: (bi, 0, off // width + h))
    blk = (1, s, width)
    tile_spec = pl.BlockSpec((hp, TQ, TQ), lambda bi, h: (h, 0, 0))
    cast_in, cast_out, cast_shapes, cast_bytes = _cast_plan(cast_weights, layer, b, n_h)
    est = (10 * s * width * 2 + 4 * hp * TQ * TQ * 4 + hp * (VT_ROWS * s * 2 + 2 * s * TQ * 4 + s * TQ * 2)
           + cast_bytes)
    outs = pl.pallas_call(
        functools.partial(_moba_kernel, heads=hp, n_cast=len(cast_weights)),
        grid=(b, n_h),
        in_specs=[
            pl.BlockSpec(memory_space=pltpu.SMEM),
            pl.BlockSpec(blk, col(OFF_QA)),
            pl.BlockSpec(blk, col(OFF_KA)),
            pl.BlockSpec(blk, col(OFF_VA)),
            pl.BlockSpec(blk, col(OFF_ZA)),
            tile_spec, tile_spec,
            *cast_in,
        ],
        out_specs=[pl.BlockSpec(blk, lambda bi, h: (bi, 0, h)), *cast_out],
        out_shape=[jax.ShapeDtypeStruct((b, s, MOBA_W), BF16), *cast_shapes],
        scratch_shapes=_attn_scratch(s, extra=(pltpu.VMEM((16, HEAD_DIM), F32),)),
        compiler_params=pltpu.CompilerParams(
            dimension_semantics=("arbitrary", "arbitrary"), vmem_limit_bytes=_vmem_limit(est)),
        name="moba_attn",
    )(rel_bias, u3, u3, u3, u3, own, prev, *cast_weights)
    return outs[0], outs[1:]


def _mla_prep_kernel(cq_ref, ckv_ref, kr_ref, gq_ref, gkv_ref, wuq_ref, wukv_ref, cos_ref, sin_ref,
                     q_out, k_out, v_out, *, q_scale):
    def rms(x_ref, g_ref):
        xf = x_ref[...].astype(F32)
        ms = jnp.mean(xf * xf, axis=-1, keepdims=True)
        return (xf * lax.rsqrt(ms + EPS) * g_ref[...]).astype(BF16)

    cos = cos_ref[...]
    sin = sin_ref[...]
    half = MLA_ROPE // 2
    first_half = lax.broadcasted_iota(jnp.int32, cos.shape, 1) < half

    def rope(xr):
        partner = jnp.where(first_half, pltpu.roll(xr, LANES - half, 1), pltpu.roll(xr, half, 1))
        return xr * cos + partner * sin

    qb = jnp.dot(rms(cq_ref, gq_ref), wuq_ref[...], preferred_element_type=F32) * q_scale
    for h in range(MLA_HEADS):
        base = h * MLA_QK
        q_out[:, base:base + MLA_NOPE] = qb[:, base:base + MLA_NOPE].astype(BF16)
        q_out[:, base + MLA_NOPE:base + MLA_QK] = rope(qb[:, base + MLA_NOPE:base + MLA_QK]).astype(BF16)

    kvb = jnp.dot(rms(ckv_ref, gkv_ref), wukv_ref[...], preferred_element_type=F32)
    in_rope = lax.broadcasted_iota(jnp.int32, cos.shape, 1) < MLA_ROPE
    k_rope = rope(jnp.where(in_rope, kr_ref[:, 0:LANES].astype(F32), 0.0)).astype(BF16)
    for h in range(MLA_HEADS):
        base = h * MLA_QK
        k_out[:, base:base + MLA_NOPE] = kvb[:, h * MLA_NOPE:(h + 1) * MLA_NOPE].astype(BF16)
        k_out[:, base + MLA_NOPE:base + MLA_QK] = k_rope
    v_out[...] = kvb[:, MLA_W:2 * MLA_W].astype(BF16)


def _mla_prep(u, g_cq, g_ckv, wuq_p, wukv_p, cos_t, sin_t, *, seq, tm):
    t = u.shape[0]
    s_tiles = seq // tm
    const = lambda i: (0, 0)
    qk_shape = jax.ShapeDtypeStruct((t, MLA_HEADS * MLA_QK), BF16)
    est = (2 * tm * (MLA_Q_LORA + 2 * MLA_KV_LORA) * 2 + 2 * (wuq_p.size + wukv_p.size) * 2
           + 4 * tm * LANES * 4 + 2 * tm * 5 * MLA_W * 2 + 4 * tm * 2 * MLA_W * 4)
    return pl.pallas_call(
        functools.partial(_mla_prep_kernel, q_scale=(MLA_NOPE + MLA_ROPE) ** -0.5 * LOG2E),
        grid=(t // tm,),
        in_specs=[
            pl.BlockSpec((tm, MLA_Q_LORA), lambda i: (i, OFF_CQ // MLA_Q_LORA)),
            pl.BlockSpec((tm, MLA_KV_LORA), lambda i: (i, OFF_CKV // MLA_KV_LORA)),
            pl.BlockSpec((tm, KR_PAD), lambda i: (i, OFF_KR // KR_PAD)),
            pl.BlockSpec((1, MLA_Q_LORA), const),
            pl.BlockSpec((1, MLA_KV_LORA), const),
            pl.BlockSpec(wuq_p.shape, const),
            pl.BlockSpec(wukv_p.shape, const),
            pl.BlockSpec((tm, LANES), lambda i: (i % s_tiles, 0)),
            pl.BlockSpec((tm, LANES), lambda i: (i % s_tiles, 0)),
        ],
        out_specs=[
            pl.BlockSpec((tm, MLA_HEADS * MLA_QK), lambda i: (i, 0)),
            pl.BlockSpec((tm, MLA_HEADS * MLA_QK), lambda i: (i, 0)),
            pl.BlockSpec((tm, MLA_W), lambda i: (i, 0)),
        ],
        out_shape=[qk_shape, qk_shape, jax.ShapeDtypeStruct((t, MLA_W), BF16)],
        compiler_params=pltpu.CompilerParams(
            dimension_semantics=("arbitrary",), vmem_limit_bytes=_vmem_limit(est)),
        name="mla_prep",
    )(u, u, u, g_cq.reshape(1, -1), g_ckv.reshape(1, -1), wuq_p, wukv_p, cos_t, sin_t)


def _mla_attn_kernel(q_ref, k_ref, v_ref, z_ref, o_ref, mask_ref, *scratch, heads):
    _write_causal_tile(mask_ref)
    per_head = len(scratch) // heads
    _run_streams([
        _mla_head(_head_view(q_ref, s, MLA_QK), _head_view(k_ref, s, MLA_QK), _head_view(v_ref, s, MLA_V),
                  _head_view(z_ref, s, MLA_V), _head_view(o_ref, s, MLA_V), mask_ref,
                  *scratch[s * per_head:(s + 1) * per_head])
        for s in range(heads)])


def _mla_head(q_ref, k_ref, v_ref, z_ref, o_ref, mask_ref, vt_ref, t0, t1, pb):
    seq = q_ref.shape[0]

    def terms_of(i, q_tile):
        return [None] * i + [mask_ref], [0.0] * (i + 1), [None] * (i + 1)

    def emit(i, o):
        rows = slice(i * TQ, (i + 1) * TQ)
        o_ref[rows, :] = _silu_gate(o, z_ref[rows, :])

    return _attention(seq // TQ, lambda i: q_ref[i * TQ:(i + 1) * TQ, :],
                      lambda j: k_ref[j * TQ:(j + 1) * TQ, :], terms_of,
                      functools.partial(_build_vt_block, v_ref, vt_ref), vt_ref, (t0, t1, pb), emit,
                      reduce_from_buffer=False)


def _mla_attn(q3, k3, v3, u3):
    b, s, _ = q3.shape
    hp = HEADS_PER_STEP
    est = (4 * s * hp * MLA_QK * 2 + 6 * s * hp * MLA_V * 2 + hp * (VT_ROWS * s * 2 + 2 * s * TQ * 4 + s * TQ * 2)
           + TQ * TQ * 4)
    return pl.pallas_call(
        functools.partial(_mla_attn_kernel, heads=hp),
        grid=(b, MLA_HEADS // hp),
        in_specs=[
            pl.BlockSpec((1, s, hp * MLA_QK), lambda bi, h: (bi, 0, h)),
            pl.BlockSpec((1, s, hp * MLA_QK), lambda bi, h: (bi, 0, h)),
            pl.BlockSpec((1, s, hp * MLA_V), lambda bi, h: (bi, 0, h)),
            pl.BlockSpec((1, s, hp * MLA_V), lambda bi, h: (bi, 0, OFF_ZB // (hp * MLA_V) + h)),
        ],
        out_specs=pl.BlockSpec((1, s, hp * MLA_V), lambda bi, h: (bi, 0, h)),
        out_shape=jax.ShapeDtypeStruct((b, s, MLA_W), BF16),
        scratch_shapes=[pltpu.VMEM((TQ, TQ), F32)] + _attn_scratch(s),
        compiler_params=pltpu.CompilerParams(
            dimension_semantics=("arbitrary", "arbitrary"), vmem_limit_bytes=_vmem_limit(est)),
        name="mla_attn",
    )(q3, k3, v3, u3)


def _mem_attn_kernel(q_ref, k_ref, v_ref, z_ref, *rest, heads, n_cast):
    o_ref, scratch = _cast_row_blocks(rest, n_cast)
    assert k_ref.shape[1] == TQ
    per_head = len(scratch) // heads
    d = MEM_HEAD_DIM
    _run_streams([
        _mem_head(_head_view(q_ref, s, d), _head_view(k_ref, s, d), _head_view(v_ref, s, d),
                  _head_view(z_ref, s, d), _head_view(o_ref, s, d), *scratch[s * per_head:(s + 1) * per_head])
        for s in range(heads)])


def _mem_head(q_ref, k_ref, v_ref, z_ref, o_ref, vt_ref, t0, t1, pb):
    seq = q_ref.shape[0]

    def prepare(i):
        if i == 0:
            _build_vt_block(v_ref, vt_ref, 0)

    def emit(i, o):
        rows = slice(i * TQ, (i + 1) * TQ)
        o_ref[rows, :] = _silu_gate(o, z_ref[rows, :])

    return _attention(seq // TQ, lambda i: q_ref[i * TQ:(i + 1) * TQ, :], lambda j: k_ref[...],
                      lambda i, q_tile: ([None], [0.0], [None]), prepare, vt_ref, (t0, t1, pb), emit,
                      reduce_from_buffer=True)


def _mem_attn(u3, kv3, cast_weights, layer):
    b, s, _ = u3.shape
    m = kv3.shape[1]
    hp = HEADS_PER_STEP
    d = hp * MEM_HEAD_DIM
    n_h = MEM_HEADS // hp
    cast_in, cast_out, cast_shapes, cast_bytes = _cast_plan(cast_weights, layer, b, n_h)
    est = 6 * s * d * 2 + 4 * m * d * 2 + hp * (VT_ROWS * m * 2 + 2 * m * TQ * 4 + m * TQ * 2) + cast_bytes
    outs = pl.pallas_call(
        functools.partial(_mem_attn_kernel, heads=hp, n_cast=len(cast_weights)),
        grid=(b, n_h),
        in_specs=[
            pl.BlockSpec((1, s, d), lambda bi, h: (bi, 0, OFF_QM // d + h)),
            pl.BlockSpec((1, m, d), lambda bi, h: (bi, 0, h)),
            pl.BlockSpec((1, m, d), lambda bi, h: (bi, 0, MEM_W // d + h)),
            pl.BlockSpec((1, s, d), lambda bi, h: (bi, 0, OFF_ZM // d + h)),
            *cast_in,
        ],
        out_specs=[pl.BlockSpec((1, s, d), lambda bi, h: (bi, 0, h)), *cast_out],
        out_shape=[jax.ShapeDtypeStruct((b, s, MEM_W), BF16), *cast_shapes],
        scratch_shapes=_attn_scratch(m),
        compiler_params=pltpu.CompilerParams(
            dimension_semantics=("arbitrary", "arbitrary"), vmem_limit_bytes=_vmem_limit(est)),
        name="mem_attn",
    )(u3, kv3, kv3, u3, *cast_weights)
    return outs[0], outs[1:]


def _merge_kernel(ga_ref, gb_ref, gm_ref, gl_ref, x_ref, wpa_ref, wpb_ref, wpm_ref, wout_ref, gf_ref,
                  o_ref, *, final_norm):
    d = x_ref.shape[1]

    def gated(idx, g_ref, w_ref):
        logit = gl_ref[:, idx * d:(idx + 1) * d].astype(F32)
        return jnp.dot(g_ref[...], w_ref[...], preferred_element_type=F32) / (1.0 + jnp.exp(-logit))

    y = gated(0, ga_ref, wpa_ref) + gated(1, gb_ref, wpb_ref) + gated(2, gm_ref, wpm_ref)
    r = x_ref[...] + jnp.dot(y.astype(BF16), wout_ref[...], preferred_element_type=F32)
    if final_norm:
        ms = jnp.mean(r * r, axis=-1, keepdims=True)
        r = r * lax.rsqrt(ms + EPS) * gf_ref[...]
    o_ref[...] = r


def _merge(ga, gb, gm, u, x, wpa, wpb, wpm, wout, g_final, *, tm, final_norm):
    t, d = x.shape
    const = lambda i: (0, 0)
    resident = lambda w: pl.BlockSpec(w.shape, const, pipeline_mode=pl.Buffered(1))
    rows = lambda width: pl.BlockSpec((tm, width), lambda i: (i, 0))
    w_bytes = (wpa.size + wpb.size + wpm.size + wout.size) * 2
    est = w_bytes + 2 * tm * (2 * MOBA_W + MEM_W + 3 * d) * 2 + 4 * tm * d * 4 + 6 * tm * d * 4
    return pl.pallas_call(
        functools.partial(_merge_kernel, final_norm=final_norm),
        grid=(t // tm,),
        in_specs=[
            rows(MOBA_W), rows(MLA_W), rows(MEM_W),
            pl.BlockSpec((tm, 3 * d), lambda i: (i, OFF_GL // (3 * d))),
            rows(d),
            resident(wpa), resident(wpb), resident(wpm), resident(wout),
            pl.BlockSpec((1, d), const),
        ],
        out_specs=rows(d),
        out_shape=jax.ShapeDtypeStruct((t, d), F32),
        compiler_params=pltpu.CompilerParams(
            dimension_semantics=("arbitrary",), vmem_limit_bytes=_vmem_limit(est)),
        name="merge",
    )(ga, gb, gm, u, x, wpa, wpb, wpm, wout, g_final.reshape(1, d))


def _w_in_tile_sources(n_cols):
    o_cq = 4 * MOBA_W
    o_zb = o_cq + MLA_Q_LORA + MLA_KV_LORA + MLA_ROPE
    o_qm = o_zb + MLA_W
    o_gl = o_qm + 2 * MEM_W
    srcs = ([o_gl + W_TILE * k for k in range(3 * D_MODEL // W_TILE)]
            + [W_TILE * k for k in range(4 * MOBA_W // W_TILE)] + [o_zb, o_cq, o_qm])
    assert len(srcs) * W_TILE == IN_WIDTH_P and o_gl + 3 * D_MODEL == n_cols
    assert all(src + W_TILE <= n_cols for src in srcs)
    return srcs


def _in_col_scale():
    cs = jnp.ones((1, IN_WIDTH_P), F32)
    cs = cs.at[:, OFF_QA:OFF_QA + MOBA_W].set(HEAD_DIM ** -0.5 * LOG2E)
    return cs.at[:, OFF_QM:OFF_QM + MEM_W].set(MEM_HEAD_DIM ** -0.5 * LOG2E)


def _regroup_w_uq(w):
    r = w.shape[0]
    w3 = w.reshape(r, MLA_HEADS, MLA_NOPE + MLA_ROPE)
    pad = jnp.zeros((r, MLA_HEADS, MLA_QK - MLA_NOPE - MLA_ROPE), w.dtype)
    return jnp.concatenate([w3, pad], axis=-1).reshape(r, MLA_HEADS * MLA_QK).astype(BF16)


def _regroup_w_ukv(w):
    r = w.shape[0]
    w3 = w.reshape(r, MLA_HEADS, MLA_NOPE + MLA_V)
    return jnp.concatenate([w3[:, :, :MLA_NOPE].reshape(r, MLA_W),
                            w3[:, :, MLA_NOPE:].reshape(r, MLA_W)], axis=1).astype(BF16)


def _rope_tables(seq):
    half = MLA_ROPE // 2
    inv = ROPE_THETA ** (-jnp.arange(half, dtype=F32) / half)
    ang = jnp.arange(seq, dtype=jnp.int32).astype(F32)[:, None] * inv[None, :]
    cos, sin = jnp.cos(ang), jnp.sin(ang)
    pad = LANES - MLA_ROPE
    cos_t = jnp.concatenate([cos, cos, jnp.ones((seq, pad), F32)], axis=1)
    sin_t = jnp.concatenate([-sin, sin, jnp.zeros((seq, pad), F32)], axis=1)
    return cos_t, sin_t


def kernel(x, mem, g_norm, w_in, g_cq, w_uq, g_ckv, w_ukv, g_mem, w_mem_kv, rel_bias,
           w_p_moba, w_p_mla, w_p_mem, w_out, g_final):
    b, s, d = x.shape
    m = mem.shape[1]
    depth = w_in.shape[0]
    t = b * s
    assert d == D_MODEL and s % TQ == 0 and m == TQ

    own, prev = _bias_tiles(rel_bias)
    cos_t, sin_t = _rope_tables(s)
    mem2 = mem.reshape(b * m, d)
    xs = x.reshape(t, d)
    in_scale = _in_col_scale()
    w_in_t = jnp.swapaxes(w_in, 1, 2)
    for l in range(depth):
        u = _in_proj(xs, g_norm[l], w_in_t, l, in_scale, tm=1024)
        u3 = u.reshape(b, s, IN_WIDTH_P)
        ga, (wout,) = _moba_attn(u3, rel_bias, own, prev, (w_out,), l)
        q2, k2, v2 = _mla_prep(u, g_cq[l], g_ckv[l], _regroup_w_uq(w_uq[l]), _regroup_w_ukv(w_ukv[l]),
                               cos_t, sin_t, seq=s, tm=512)
        gb = _mla_attn(q2.reshape(b, s, -1), k2.reshape(b, s, -1), v2.reshape(b, s, -1), u3)
        kvm = _norm_matmul(mem2, g_mem[l], w_mem_kv, l, tm=b * m, tn=2 * MEM_W)
        gm, (wpa, wpb, wpm) = _mem_attn(u3, kvm.reshape(b, m, 2 * MEM_W), (w_p_moba, w_p_mla, w_p_mem), l)
        xs = _merge(ga.reshape(t, MOBA_W), gb.reshape(t, MLA_W), gm.reshape(t, MEM_W), u, xs,
                    wpa, wpb, wpm, wout, g_final, tm=256, final_norm=(l == depth - 1))
    return xs.reshape(b, s, d)
```

```python
import functools
import math

import jax
import jax.numpy as jnp
from jax import lax
from jax.experimental import pallas as pl
from jax.experimental.pallas import tpu as pltpu

D_MODEL = 2048
MOBA_HEADS = 8
HEAD_DIM = 128
MOBA_BLOCK = 256
MOBA_TOPK = 3
MLA_HEADS = 8
MLA_Q_LORA = 512
MLA_KV_LORA = 256
MLA_NOPE = 128
MLA_ROPE = 64
MLA_V = 128
ROPE_THETA = 10000.0
MEM_HEADS = 4
MEM_HEAD_DIM = 128
N_BUCKETS = 32
MAX_DISTANCE = 128
EPS = 1e-6

MOBA_W = MOBA_HEADS * HEAD_DIM
MLA_W = MLA_HEADS * MLA_V
MEM_W = MEM_HEADS * MEM_HEAD_DIM

LANES = 128
MXU_DIM = 256
V7X_VMEM_BYTES = 64 * 1024 * 1024
VMEM_HEADROOM_BYTES = 4 * 1024 * 1024
KERNEL_TEMP_BYTES = 8 * 1024 * 1024

BF16 = jnp.bfloat16
F32 = jnp.float32
LOG2E = 1.4426950408889634

KR_PAD = MXU_DIM
OFF_GL = 0
OFF_QA = OFF_GL + 3 * D_MODEL
OFF_KA = OFF_QA + MOBA_W
OFF_VA = OFF_KA + MOBA_W
OFF_ZA = OFF_VA + MOBA_W
OFF_ZB = OFF_ZA + MOBA_W
OFF_CQ = OFF_ZB + MLA_W
OFF_CKV = OFF_CQ + MLA_Q_LORA
OFF_KR = OFF_CKV + MLA_KV_LORA
OFF_QM = OFF_KR + KR_PAD
OFF_ZM = OFF_QM + MEM_W
IN_WIDTH_P = OFF_ZM + MEM_W

TQ = MOBA_BLOCK
BF16_TILE_ROWS = 16
ONES_ROWS = BF16_TILE_ROWS
VT_ROWS = MLA_V + ONES_ROWS
MLA_QK = MXU_DIM

W_TILE = 1024
W_SRC_UNIT = 64
NORM_CHUNK_ROWS = 64


def _vmem_limit(block_bytes):
    return int(min(block_bytes + KERNEL_TEMP_BYTES, V7X_VMEM_BYTES - VMEM_HEADROOM_BYTES))


def _t5_thresholds():
    max_exact = N_BUCKETS // 2

    def bucket(d):
        if d < max_exact:
            return d
        large = max_exact + int(math.log(d / max_exact) / math.log(MAX_DISTANCE / max_exact)
                                * (N_BUCKETS - max_exact))
        return min(large, N_BUCKETS - 1)

    thr, d = [], 0
    for b in range(1, N_BUCKETS):
        while bucket(d) < b:
            d += 1
        thr.append(d)
    return tuple(thr)


T5_THRESHOLDS = _t5_thresholds()
assert T5_THRESHOLDS[-1] <= MOBA_BLOCK + 1


def _norm_rows(x_ref, g_ref, h_ref, chunk):
    def body(r, carry):
        rows = pl.ds(pl.multiple_of(r * chunk, chunk), chunk)
        xv = x_ref[rows, :]
        ms = jnp.mean(xv * xv, axis=-1, keepdims=True)
        h_ref[rows, :] = (xv * lax.rsqrt(ms + EPS) * g_ref[...]).astype(BF16)
        return carry
    lax.fori_loop(0, x_ref.shape[0] // chunk, body, 0, unroll=4)


def _norm_matmul_kernel(x_ref, g_ref, w_ref, o_ref, h_ref, *, chunk):
    pl.when(pl.program_id(1) == 0)(functools.partial(_norm_rows, x_ref, g_ref, h_ref, chunk))
    o_ref[...] = jnp.dot(h_ref[...], w_ref[...].astype(BF16), preferred_element_type=F32).astype(o_ref.dtype)


def _in_proj_kernel(src_ref, x_ref, g_ref, wt_ref, cs_ref, o_ref, h_ref, *, chunk):
    del src_ref
    pl.when(pl.program_id(1) == 0)(functools.partial(_norm_rows, x_ref, g_ref, h_ref, chunk))
    acc = lax.dot_general(h_ref[...], wt_ref[...].astype(BF16), (((1,), (1,)), ((), ())),
                          preferred_element_type=F32)
    o_ref[...] = (acc * cs_ref[...]).astype(o_ref.dtype)


def _in_proj(x, g, wt_all, layer, col_scale, *, tm):
    t, d = x.shape
    srcs = _w_in_tile_sources(wt_all.shape[1])
    unit = W_SRC_UNIT
    assert all(src % unit == 0 for src in srcs)
    src_units = jnp.asarray([src // unit for src in srcs], jnp.int32)
    est = 2 * tm * d * 4 + tm * d * 2 + 2 * W_TILE * d * 4 + W_TILE * d * 2 + 2 * tm * W_TILE * 2 + tm * W_TILE * 4
    return pl.pallas_call(
        functools.partial(_in_proj_kernel, chunk=NORM_CHUNK_ROWS),
        grid_spec=pltpu.PrefetchScalarGridSpec(
            num_scalar_prefetch=1,
            grid=(t // tm, len(srcs)),
            in_specs=[
                pl.BlockSpec((tm, d), lambda i, j, src: (i, 0)),
                pl.BlockSpec((1, d), lambda i, j, src: (0, 0)),
                pl.BlockSpec((pl.Squeezed(), pl.Element(W_TILE), pl.Element(d)),
                             lambda i, j, src: (layer, src[j] * unit, 0)),
                pl.BlockSpec((1, W_TILE), lambda i, j, src: (0, j)),
            ],
            out_specs=pl.BlockSpec((tm, W_TILE), lambda i, j, src: (i, j)),
            scratch_shapes=[pltpu.VMEM((tm, d), BF16)],
        ),
        out_shape=jax.ShapeDtypeStruct((t, IN_WIDTH_P), BF16),
        compiler_params=pltpu.CompilerParams(
            dimension_semantics=("arbitrary", "arbitrary"), vmem_limit_bytes=_vmem_limit(est)),
        name="in_proj",
    )(src_units, x, g.reshape(1, d), wt_all, col_scale)


def _norm_matmul(x, g, w_all, layer, *, tm, tn):
    t, d = x.shape
    n = w_all.shape[2]
    est = 2 * tm * d * 4 + tm * d * 2 + 2 * d * tn * 4 + d * tn * 2 + 2 * tm * tn * 2 + tm * tn * 4
    return pl.pallas_call(
        functools.partial(_norm_matmul_kernel, chunk=NORM_CHUNK_ROWS),
        grid=(t // tm, n // tn),
        in_specs=[
            pl.BlockSpec((tm, d), lambda i, j: (i, 0)),
            pl.BlockSpec((1, d), lambda i, j: (0, 0)),
            pl.BlockSpec((pl.Squeezed(), d, tn), lambda i, j: (layer, 0, j)),
        ],
        out_specs=pl.BlockSpec((tm, tn), lambda i, j: (i, j)),
        out_shape=jax.ShapeDtypeStruct((t, n), BF16),
        scratch_shapes=[pltpu.VMEM((tm, d), BF16)],
        compiler_params=pltpu.CompilerParams(
            dimension_semantics=("arbitrary", "arbitrary"), vmem_limit_bytes=_vmem_limit(est)),
        name="norm_matmul",
    )(x, g.reshape(1, d), w_all)


def _bias_tiles_kernel(rb_ref, own_ref, prev_ref):
    key = lax.broadcasted_iota(jnp.int32, (TQ, TQ), 0)
    qry = lax.broadcasted_iota(jnp.int32, (TQ, TQ), 1)
    d_own = qry - key
    d_prev = d_own + MOBA_BLOCK

    def lookup(dist, h):
        val = jnp.zeros(dist.shape, F32) + rb_ref[0, h]
        for b in range(1, N_BUCKETS):
            val = jnp.where(dist >= T5_THRESHOLDS[b - 1], rb_ref[b, h], val)
        return val * LOG2E

    for h in range(own_ref.shape[0]):
        own_ref[h] = jnp.where(d_own >= 0, lookup(d_own, h), -jnp.inf)
        prev_ref[h] = lookup(d_prev, h)


def _bias_tiles(rel_bias):
    heads = rel_bias.shape[1]
    tile = jax.ShapeDtypeStruct((heads, TQ, TQ), F32)
    spec = pl.BlockSpec((heads, TQ, TQ), lambda i: (0, 0, 0))
    return pl.pallas_call(
        _bias_tiles_kernel,
        grid=(1,),
        in_specs=[pl.BlockSpec(memory_space=pltpu.SMEM)],
        out_specs=[spec, spec],
        out_shape=[tile, tile],
        name="bias_tiles",
    )(rel_bias)


def _build_vt_block(v_ref, vt_ref, j):
    dv = v_ref.shape[-1]
    blk = slice(j * TQ, (j + 1) * TQ)
    vt_ref[0:dv, blk] = v_ref[blk, :].astype(F32).T.astype(BF16)
    row = lax.broadcasted_iota(jnp.int32, (ONES_ROWS, TQ), 0)
    vt_ref[dv:dv + ONES_ROWS, blk] = jnp.where(row == 0, 1.0, 0.0).astype(BF16)


def _attention(n_tiles, q_tile_of, k_block_of, terms_of, prepare, vt_ref, bufs, emit):
    nt = (((1,), (1,)), ((), ()))
    dv = vt_ref.shape[0] - ONES_ROWS
    t_bufs, p_buf = bufs[:2], bufs[2]
    state = {}

    def stage1(i):
        prepare(i)
        q = q_tile_of(i)
        adds, consts, sels = terms_of(i, q)
        st = state[i] = dict(consts=consts, sels=sels, m=None)
        t_buf = t_bufs[i % 2]

        def item(j):
            blk = slice(j * TQ, (j + 1) * TQ)
            t = lax.dot_general(k_block_of(j), q, nt, preferred_element_type=F32)
            if adds[j] is not None:
                t = t + adds[j][...]
            t_buf[blk, :] = t
            mj = jnp.max(t, axis=0, keepdims=True) + consts[j]
            if sels[j] is not None:
                mj = jnp.where(sels[j], mj, -jnp.inf)
            st["m"] = mj if st["m"] is None else jnp.maximum(st["m"], mj)
        return [functools.partial(item, j) for j in range(len(consts))]

    def stage2(i):
        st = state.pop(i)
        t_buf = t_bufs[i % 2]

        def item(j):
            blk = slice(j * TQ, (j + 1) * TQ)
            off = st["m"] - st["consts"][j]
            if st["sels"][j] is not None:
                off = jnp.where(st["sels"][j], off, jnp.inf)
            p_buf[blk, :] = jnp.exp2(t_buf[blk, :] - off).astype(BF16)

        def finish():
            n_keys = len(st["consts"]) * TQ
            acc = jnp.dot(vt_ref[:, 0:n_keys], p_buf[0:n_keys, :], preferred_element_type=F32)
            emit(i, (acc[0:dv, :] / acc[dv:dv + 1, :]).T)
        return [functools.partial(item, j) for j in range(len(st["consts"]))], finish

    for item in stage1(0):
        item()
        yield
    for i in range(n_tiles):
        ahead = stage1(i + 1) if i + 1 < n_tiles else []
        behind, finish = stage2(i)
        for k in range(max(len(ahead), len(behind))):
            if k < len(ahead):
                ahead[k]()
            if k < len(behind):
                behind[k]()
            yield
        finish()


def _run_streams(streams):
    active = list(streams)
    while active:
        for stream in list(active):
            if next(stream, StopIteration) is StopIteration:
                active.remove(stream)


def _cast_plan(weights, layer, n_b, n_h):
    n_steps = n_b * n_h
    ins, outs, shapes, nbytes = [], [], [], 0
    for w in weights:
        _, rows, cols = w.shape
        assert rows % (n_steps * BF16_TILE_ROWS) == 0
        blk_rows = rows // n_steps
        ins.append(pl.BlockSpec((pl.Squeezed(), blk_rows, cols), lambda bi, h: (layer, bi * n_h + h, 0)))
        outs.append(pl.BlockSpec((blk_rows, cols), lambda bi, h: (bi * n_h + h, 0)))
        shapes.append(jax.ShapeDtypeStruct((rows, cols), BF16))
        nbytes += 2 * blk_rows * cols * (4 + 2)
    return ins, outs, shapes, nbytes


def _cast_row_blocks(rest, n_cast):
    cast_in, (o_ref, *cast_out), scratch = rest[:n_cast], rest[n_cast:2 * n_cast + 1], rest[2 * n_cast + 1:]
    for w_ref, wb_ref in zip(cast_in, cast_out):
        wb_ref[...] = w_ref[...].astype(BF16)
    return o_ref, scratch


def _silu_gate(o, z):
    zf = z.astype(F32)
    return (o * (zf / (1.0 + jnp.exp(-zf)))).astype(BF16)


def _write_causal_tile(mask_ref):
    key = lax.broadcasted_iota(jnp.int32, (TQ, TQ), 0)
    qry = lax.broadcasted_iota(jnp.int32, (TQ, TQ), 1)
    mask_ref[...] = jnp.where(key <= qry, 0.0, -jnp.inf).astype(F32)


def _head_view(ref, head, width):
    return ref.at[0, :, head * width:(head + 1) * width]


def _moba_kernel(rb_ref, q_ref, k_ref, v_ref, z_ref, own_ref, prev_ref, *rest, heads, n_cast):
    o_ref, scratch = _cast_row_blocks(rest, n_cast)
    per_head = len(scratch) // heads
    _run_streams([
        _moba_head(rb_ref, pl.program_id(1) * heads + s, _head_view(q_ref, s, HEAD_DIM),
                   _head_view(k_ref, s, HEAD_DIM), _head_view(v_ref, s, HEAD_DIM),
                   _head_view(z_ref, s, HEAD_DIM), own_ref.at[s], prev_ref.at[s],
                   _head_view(o_ref, s, HEAD_DIM), *scratch[s * per_head:(s + 1) * per_head])
        for s in range(heads)])


def _moba_head(rb_ref, head, q_ref, k_ref, v_ref, z_ref, own_ref, prev_ref, o_ref, vt_ref, t0, t1, pb, km_ref):
    seq = q_ref.shape[0]
    n_tiles = seq // TQ
    far_const = rb_ref[N_BUCKETS - 1, head] * LOG2E
    km_ref[...] = jnp.zeros(km_ref.shape, F32)

    def prepare(i):
        _build_vt_block(v_ref, vt_ref, i)
        k_blk = k_ref[i * TQ:(i + 1) * TQ, :].astype(F32)
        km_ref[i:i + 1, :] = jnp.sum(k_blk, axis=0, keepdims=True) * (1.0 / MOBA_BLOCK)

    nt = (((1,), (1,)), ((), ()))

    def terms_of(i, q_tile):
        if i > MOBA_TOPK:
            k_mean = km_ref[...]
            km1 = k_mean.astype(BF16)
            rem = k_mean - km1.astype(F32)
            km2 = rem.astype(BF16)
            km3 = (rem - km2.astype(F32)).astype(BF16)
            gate = (lax.dot_general(km1, q_tile, nt, preferred_element_type=F32)
                    + lax.dot_general(km2, q_tile, nt, preferred_element_type=F32)
                    + lax.dot_general(km3, q_tile, nt, preferred_element_type=F32))
            sels = []
            for j in range(i):
                gj = gate[j:j + 1, :]
                cnt = jnp.zeros(gj.shape, F32)
                for jp in range(i):
                    if jp == j:
                        continue
                    gp = gate[jp:jp + 1, :]
                    beats = (gp >= gj) if jp < j else (gp > gj)
                    cnt = cnt + jnp.where(beats, 1.0, 0.0)
                sels.append(cnt < MOBA_TOPK)
        else:
            sels = [None] * i
        sels.append(None)
        adds = [None] * (i + 1)
        consts = [far_const] * (i + 1)
        adds[i], consts[i] = own_ref, 0.0
        if i >= 1:
            adds[i - 1], consts[i - 1] = prev_ref, 0.0
        return adds, consts, sels

    def emit(i, o):
        rows = slice(i * TQ, (i + 1) * TQ)
        o_ref[rows, :] = _silu_gate(o, z_ref[rows, :])

    return _attention(n_tiles, lambda i: q_ref[i * TQ:(i + 1) * TQ, :],
                      lambda j: k_ref[j * TQ:(j + 1) * TQ, :], terms_of, prepare, vt_ref, (t0, t1, pb), emit)


HEADS_PER_STEP = 4


def _attn_scratch(n_keys, extra=()):
    per_head = [pltpu.VMEM((VT_ROWS, n_keys), BF16), pltpu.VMEM((n_keys, TQ), F32),
                pltpu.VMEM((n_keys, TQ), F32), pltpu.VMEM((n_keys, TQ), BF16), *extra]
    return per_head * HEADS_PER_STEP


def _moba_attn(u3, rel_bias, own, prev, cast_weights, layer):
    b, s, _ = u3.shape
    assert s // MOBA_BLOCK <= BF16_TILE_ROWS
    hp = HEADS_PER_STEP
    width = hp * HEAD_DIM
    n_h = MOBA_HEADS // hp
    col = lambda off: (lambda bi, h: (bi, 0, off // width + h))
    blk = (1, s, width)
    tile_spec = pl.BlockSpec((hp, TQ, TQ), lambda bi, h: (h, 0, 0))
    cast_in, cast_out, cast_shapes, cast_bytes = _cast_plan(cast_weights, layer, b, n_h)
    est = (10 * s * width * 2 + 4 * hp * TQ * TQ * 4 + hp * (VT_ROWS * s * 2 + 2 * s * TQ * 4 + s * TQ * 2)
           + cast_bytes)
    outs = pl.pallas_call(
        functools.partial(_moba_kernel, heads=hp, n_cast=len(cast_weights)),
        grid=(b, n_h),
        in_specs=[
            pl.BlockSpec(memory_space=pltpu.SMEM),
            pl.BlockSpec(blk, col(OFF_QA)),
            pl.BlockSpec(blk, col(OFF_KA)),
            pl.BlockSpec(blk, col(OFF_VA)),
            pl.BlockSpec(blk, col(OFF_ZA)),
            tile_spec, tile_spec,
            *cast_in,
        ],
        out_specs=[pl.BlockSpec(blk, lambda bi, h: (bi, 0, h)), *cast_out],
        out_shape=[jax.ShapeDtypeStruct((b, s, MOBA_W), BF16), *cast_shapes],
        scratch_shapes=_attn_scratch(s, extra=(pltpu.VMEM((BF16_TILE_ROWS, HEAD_DIM), F32),)),
        compiler_params=pltpu.CompilerParams(
            dimension_semantics=("arbitrary", "arbitrary"), vmem_limit_bytes=_vmem_limit(est)),
        name="moba_attn",
    )(rel_bias, u3, u3, u3, u3, own, prev, *cast_weights)
    return outs[0], outs[1:]


def _mla_prep_kernel(cq_ref, ckv_ref, kr_ref, gq_ref, gkv_ref, wuq_ref, wukv_ref, cos_ref, sin_ref,
                     q_out, k_out, v_out, *, q_scale):
    def rms(x_ref, g_ref):
        xf = x_ref[...].astype(F32)
        ms = jnp.mean(xf * xf, axis=-1, keepdims=True)
        return (xf * lax.rsqrt(ms + EPS) * g_ref[...]).astype(BF16)

    cos = cos_ref[...]
    sin = sin_ref[...]
    half = MLA_ROPE // 2
    first_half = lax.broadcasted_iota(jnp.int32, cos.shape, 1) < half

    def rope(xr):
        partner = jnp.where(first_half, pltpu.roll(xr, LANES - half, 1), pltpu.roll(xr, half, 1))
        return xr * cos + partner * sin

    qb = jnp.dot(rms(cq_ref, gq_ref), wuq_ref[...], preferred_element_type=F32) * q_scale
    for h in range(MLA_HEADS):
        base = h * MLA_QK
        q_out[:, base:base + MLA_NOPE] = qb[:, base:base + MLA_NOPE].astype(BF16)
        q_out[:, base + MLA_NOPE:base + MLA_QK] = rope(qb[:, base + MLA_NOPE:base + MLA_QK]).astype(BF16)

    kvb = jnp.dot(rms(ckv_ref, gkv_ref), wukv_ref[...], preferred_element_type=F32)
    in_rope = lax.broadcasted_iota(jnp.int32, cos.shape, 1) < MLA_ROPE
    k_rope = rope(jnp.where(in_rope, kr_ref[:, 0:LANES].astype(F32), 0.0)).astype(BF16)
    for h in range(MLA_HEADS):
        base = h * MLA_QK
        k_out[:, base:base + MLA_NOPE] = kvb[:, h * MLA_NOPE:(h + 1) * MLA_NOPE].astype(BF16)
        k_out[:, base + MLA_NOPE:base + MLA_QK] = k_rope
    v_out[...] = kvb[:, MLA_W:2 * MLA_W].astype(BF16)


def _mla_prep(u, g_cq, g_ckv, wuq_p, wukv_p, cos_t, sin_t, *, seq, tm):
    t = u.shape[0]
    s_tiles = seq // tm
    const = lambda i: (0, 0)
    qk_shape = jax.ShapeDtypeStruct((t, MLA_HEADS * MLA_QK), BF16)
    est = (2 * tm * (MLA_Q_LORA + 2 * MLA_KV_LORA) * 2 + 2 * (wuq_p.size + wukv_p.size) * 2
           + 4 * tm * LANES * 4 + 2 * tm * 5 * MLA_W * 2 + 4 * tm * 2 * MLA_W * 4)
    return pl.pallas_call(
        functools.partial(_mla_prep_kernel, q_scale=(MLA_NOPE + MLA_ROPE) ** -0.5 * LOG2E),
        grid=(t // tm,),
        in_specs=[
            pl.BlockSpec((tm, MLA_Q_LORA), lambda i: (i, OFF_CQ // MLA_Q_LORA)),
            pl.BlockSpec((tm, MLA_KV_LORA), lambda i: (i, OFF_CKV // MLA_KV_LORA)),
            pl.BlockSpec((tm, KR_PAD), lambda i: (i, OFF_KR // KR_PAD)),
            pl.BlockSpec((1, MLA_Q_LORA), const),
            pl.BlockSpec((1, MLA_KV_LORA), const),
            pl.BlockSpec(wuq_p.shape, const),
            pl.BlockSpec(wukv_p.shape, const),
            pl.BlockSpec((tm, LANES), lambda i: (i % s_tiles, 0)),
            pl.BlockSpec((tm, LANES), lambda i: (i % s_tiles, 0)),
        ],
        out_specs=[
            pl.BlockSpec((tm, MLA_HEADS * MLA_QK), lambda i: (i, 0)),
            pl.BlockSpec((tm, MLA_HEADS * MLA_QK), lambda i: (i, 0)),
            pl.BlockSpec((tm, MLA_W), lambda i: (i, 0)),
        ],
        out_shape=[qk_shape, qk_shape, jax.ShapeDtypeStruct((t, MLA_W), BF16)],
        compiler_params=pltpu.CompilerParams(
            dimension_semantics=("arbitrary",), vmem_limit_bytes=_vmem_limit(est)),
        name="mla_prep",
    )(u, u, u, g_cq.reshape(1, -1), g_ckv.reshape(1, -1), wuq_p, wukv_p, cos_t, sin_t)


def _mla_attn_kernel(q_ref, k_ref, v_ref, z_ref, o_ref, mask_ref, *scratch, heads):
    _write_causal_tile(mask_ref)
    per_head = len(scratch) // heads
    _run_streams([
        _mla_head(_head_view(q_ref, s, MLA_QK), _head_view(k_ref, s, MLA_QK), _head_view(v_ref, s, MLA_V),
                  _head_view(z_ref, s, MLA_V), _head_view(o_ref, s, MLA_V), mask_ref,
                  *scratch[s * per_head:(s + 1) * per_head])
        for s in range(heads)])


def _mla_head(q_ref, k_ref, v_ref, z_ref, o_ref, mask_ref, vt_ref, t0, t1, pb):
    seq = q_ref.shape[0]

    def terms_of(i, q_tile):
        return [None] * i + [mask_ref], [0.0] * (i + 1), [None] * (i + 1)

    def emit(i, o):
        rows = slice(i * TQ, (i + 1) * TQ)
        o_ref[rows, :] = _silu_gate(o, z_ref[rows, :])

    return _attention(seq // TQ, lambda i: q_ref[i * TQ:(i + 1) * TQ, :],
                      lambda j: k_ref[j * TQ:(j + 1) * TQ, :], terms_of,
                      functools.partial(_build_vt_block, v_ref, vt_ref), vt_ref, (t0, t1, pb), emit)


def _mla_attn(q3, k3, v3, u3):
    b, s, _ = q3.shape
    hp = HEADS_PER_STEP
    est = (4 * s * hp * MLA_QK * 2 + 6 * s * hp * MLA_V * 2 + hp * (VT_ROWS * s * 2 + 2 * s * TQ * 4 + s * TQ * 2)
           + TQ * TQ * 4)
    return pl.pallas_call(
        functools.partial(_mla_attn_kernel, heads=hp),
        grid=(b, MLA_HEADS // hp),
        in_specs=[
            pl.BlockSpec((1, s, hp * MLA_QK), lambda bi, h: (bi, 0, h)),
            pl.BlockSpec((1, s, hp * MLA_QK), lambda bi, h: (bi, 0, h)),
            pl.BlockSpec((1, s, hp * MLA_V), lambda bi, h: (bi, 0, h)),
            pl.BlockSpec((1, s, hp * MLA_V), lambda bi, h: (bi, 0, OFF_ZB // (hp * MLA_V) + h)),
        ],
        out_specs=pl.BlockSpec((1, s, hp * MLA_V), lambda bi, h: (bi, 0, h)),
        out_shape=jax.ShapeDtypeStruct((b, s, MLA_W), BF16),
        scratch_shapes=[pltpu.VMEM((TQ, TQ), F32)] + _attn_scratch(s),
        compiler_params=pltpu.CompilerParams(
            dimension_semantics=("arbitrary", "arbitrary"), vmem_limit_bytes=_vmem_limit(est)),
        name="mla_attn",
    )(q3, k3, v3, u3)


def _mem_attn_kernel(q_ref, k_ref, v_ref, z_ref, *rest, heads, n_cast):
    o_ref, scratch = _cast_row_blocks(rest, n_cast)
    assert k_ref.shape[1] == TQ
    per_head = len(scratch) // heads
    d = MEM_HEAD_DIM
    _run_streams([
        _mem_head(_head_view(q_ref, s, d), _head_view(k_ref, s, d), _head_view(v_ref, s, d),
                  _head_view(z_ref, s, d), _head_view(o_ref, s, d), *scratch[s * per_head:(s + 1) * per_head])
        for s in range(heads)])


def _mem_head(q_ref, k_ref, v_ref, z_ref, o_ref, vt_ref, t0, t1, pb):
    seq = q_ref.shape[0]

    def prepare(i):
        if i == 0:
            _build_vt_block(v_ref, vt_ref, 0)

    def emit(i, o):
        rows = slice(i * TQ, (i + 1) * TQ)
        o_ref[rows, :] = _silu_gate(o, z_ref[rows, :])

    return _attention(seq // TQ, lambda i: q_ref[i * TQ:(i + 1) * TQ, :], lambda j: k_ref[...],
                      lambda i, q_tile: ([None], [0.0], [None]), prepare, vt_ref, (t0, t1, pb), emit)


def _mem_attn(u3, kv3, cast_weights, layer):
    b, s, _ = u3.shape
    m = kv3.shape[1]
    hp = HEADS_PER_STEP
    d = hp * MEM_HEAD_DIM
    n_h = MEM_HEADS // hp
    cast_in, cast_out, cast_shapes, cast_bytes = _cast_plan(cast_weights, layer, b, n_h)
    est = 6 * s * d * 2 + 4 * m * d * 2 + hp * (VT_ROWS * m * 2 + 2 * m * TQ * 4 + m * TQ * 2) + cast_bytes
    outs = pl.pallas_call(
        functools.partial(_mem_attn_kernel, heads=hp, n_cast=len(cast_weights)),
        grid=(b, n_h),
        in_specs=[
            pl.BlockSpec((1, s, d), lambda bi, h: (bi, 0, OFF_QM // d + h)),
            pl.BlockSpec((1, m, d), lambda bi, h: (bi, 0, h)),
            pl.BlockSpec((1, m, d), lambda bi, h: (bi, 0, MEM_W // d + h)),
            pl.BlockSpec((1, s, d), lambda bi, h: (bi, 0, OFF_ZM // d + h)),
            *cast_in,
        ],
        out_specs=[pl.BlockSpec((1, s, d), lambda bi, h: (bi, 0, h)), *cast_out],
        out_shape=[jax.ShapeDtypeStruct((b, s, MEM_W), BF16), *cast_shapes],
        scratch_shapes=_attn_scratch(m),
        compiler_params=pltpu.CompilerParams(
            dimension_semantics=("arbitrary", "arbitrary"), vmem_limit_bytes=_vmem_limit(est)),
        name="mem_attn",
    )(u3, kv3, kv3, u3, *cast_weights)
    return outs[0], outs[1:]


def _merge_kernel(ga_ref, gb_ref, gm_ref, gl_ref, x_ref, wpa_ref, wpb_ref, wpm_ref, wout_ref, gf_ref,
                  o_ref, *, final_norm):
    d = x_ref.shape[1]

    def gated(idx, g_ref, w_ref):
        logit = gl_ref[:, idx * d:(idx + 1) * d].astype(F32)
        return jnp.dot(g_ref[...], w_ref[...], preferred_element_type=F32) / (1.0 + jnp.exp(-logit))

    y = gated(0, ga_ref, wpa_ref) + gated(1, gb_ref, wpb_ref) + gated(2, gm_ref, wpm_ref)
    r = x_ref[...] + jnp.dot(y.astype(BF16), wout_ref[...], preferred_element_type=F32)
    if final_norm:
        ms = jnp.mean(r * r, axis=-1, keepdims=True)
        r = r * lax.rsqrt(ms + EPS) * gf_ref[...]
    o_ref[...] = r


def _merge(ga, gb, gm, u, x, wpa, wpb, wpm, wout, g_final, *, tm, final_norm):
    t, d = x.shape
    const = lambda i: (0, 0)
    resident = lambda w: pl.BlockSpec(w.shape, const, pipeline_mode=pl.Buffered(1))
    rows = lambda width: pl.BlockSpec((tm, width), lambda i: (i, 0))
    w_bytes = (wpa.size + wpb.size + wpm.size + wout.size) * 2
    est = w_bytes + 2 * tm * (2 * MOBA_W + MEM_W + 3 * d) * 2 + 4 * tm * d * 4 + 6 * tm * d * 4
    return pl.pallas_call(
        functools.partial(_merge_kernel, final_norm=final_norm),
        grid=(t // tm,),
        in_specs=[
            rows(MOBA_W), rows(MLA_W), rows(MEM_W),
            pl.BlockSpec((tm, 3 * d), lambda i: (i, OFF_GL // (3 * d))),
            rows(d),
            resident(wpa), resident(wpb), resident(wpm), resident(wout),
            pl.BlockSpec((1, d), const),
        ],
        out_specs=rows(d),
        out_shape=jax.ShapeDtypeStruct((t, d), F32),
        compiler_params=pltpu.CompilerParams(
            dimension_semantics=("arbitrary",), vmem_limit_bytes=_vmem_limit(est)),
        name="merge",
    )(ga, gb, gm, u, x, wpa, wpb, wpm, wout, g_final.reshape(1, d))


def _w_in_tile_sources(n_cols):
    o_cq = 4 * MOBA_W
    o_zb = o_cq + MLA_Q_LORA + MLA_KV_LORA + MLA_ROPE
    o_qm = o_zb + MLA_W
    o_gl = o_qm + 2 * MEM_W
    srcs = ([o_gl + W_TILE * k for k in range(3 * D_MODEL // W_TILE)]
            + [W_TILE * k for k in range(4 * MOBA_W // W_TILE)] + [o_zb, o_cq, o_qm])
    assert len(srcs) * W_TILE == IN_WIDTH_P and o_gl + 3 * D_MODEL == n_cols
    assert all(src + W_TILE <= n_cols for src in srcs)
    return srcs


def _in_col_scale():
    cs = jnp.ones((1, IN_WIDTH_P), F32)
    cs = cs.at[:, OFF_QA:OFF_QA + MOBA_W].set(HEAD_DIM ** -0.5 * LOG2E)
    return cs.at[:, OFF_QM:OFF_QM + MEM_W].set(MEM_HEAD_DIM ** -0.5 * LOG2E)


def _regroup_w_uq(w):
    r = w.shape[0]
    w3 = w.reshape(r, MLA_HEADS, MLA_NOPE + MLA_ROPE)
    pad = jnp.zeros((r, MLA_HEADS, MLA_QK - MLA_NOPE - MLA_ROPE), w.dtype)
    return jnp.concatenate([w3, pad], axis=-1).reshape(r, MLA_HEADS * MLA_QK).astype(BF16)


def _regroup_w_ukv(w):
    r = w.shape[0]
    w3 = w.reshape(r, MLA_HEADS, MLA_NOPE + MLA_V)
    return jnp.concatenate([w3[:, :, :MLA_NOPE].reshape(r, MLA_W),
                            w3[:, :, MLA_NOPE:].reshape(r, MLA_W)], axis=1).astype(BF16)


def _rope_tables(seq):
    half = MLA_ROPE // 2
    inv = ROPE_THETA ** (-jnp.arange(half, dtype=F32) / half)
    ang = jnp.arange(seq, dtype=jnp.int32).astype(F32)[:, None] * inv[None, :]
    cos, sin = jnp.cos(ang), jnp.sin(ang)
    pad = LANES - MLA_ROPE
    cos_t = jnp.concatenate([cos, cos, jnp.ones((seq, pad), F32)], axis=1)
    sin_t = jnp.concatenate([-sin, sin, jnp.zeros((seq, pad), F32)], axis=1)
    return cos_t, sin_t


def kernel(x, mem, g_norm, w_in, g_cq, w_uq, g_ckv, w_ukv, g_mem, w_mem_kv, rel_bias,
           w_p_moba, w_p_mla, w_p_mem, w_out, g_final):
    b, s, d = x.shape
    m = mem.shape[1]
    depth = w_in.shape[0]
    t = b * s
    assert d == D_MODEL and s % TQ == 0 and m == TQ

    own, prev = _bias_tiles(rel_bias)
    cos_t, sin_t = _rope_tables(s)
    mem2 = mem.reshape(b * m, d)
    xs = x.reshape(t, d)
    in_scale = _in_col_scale()
    w_in_t = jnp.swapaxes(w_in, 1, 2)
    for l in range(depth):
        u = _in_proj(xs, g_norm[l], w_in_t, l, in_scale, tm=1024)
        u3 = u.reshape(b, s, IN_WIDTH_P)
        ga, (wout,) = _moba_attn(u3, rel_bias, own, prev, (w_out,), l)
        q2, k2, v2 = _mla_prep(u, g_cq[l], g_ckv[l], _regroup_w_uq(w_uq[l]), _regroup_w_ukv(w_ukv[l]),
                               cos_t, sin_t, seq=s, tm=512)
        gb = _mla_attn(q2.reshape(b, s, -1), k2.reshape(b, s, -1), v2.reshape(b, s, -1), u3)
        kvm = _norm_matmul(mem2, g_mem[l], w_mem_kv, l, tm=b * m, tn=MEM_W)
        gm, (wpa, wpb, wpm) = _mem_attn(u3, kvm.reshape(b, m, 2 * MEM_W), (w_p_moba, w_p_mla, w_p_mem), l)
        xs = _merge(ga.reshape(t, MOBA_W), gb.reshape(t, MLA_W), gm.reshape(t, MEM_W), u, xs,
                    wpa, wpb, wpm, wout, g_final, tm=256, final_norm=(l == depth - 1))
    return xs.reshape(b, s, d)
```

```python
import functools
import math

import jax
import jax.numpy as jnp
from jax import lax
from jax.experimental import pallas as pl
from jax.experimental.pallas import tpu as pltpu

D_MODEL = 2048
MOBA_HEADS = 8
HEAD_DIM = 128
MOBA_BLOCK = 256
MOBA_TOPK = 3
MLA_HEADS = 8
MLA_Q_LORA = 512
MLA_KV_LORA = 256
MLA_NOPE = 128
MLA_ROPE = 64
MLA_V = 128
ROPE_THETA = 10000.0
MEM_HEADS = 4
MEM_HEAD_DIM = 128
N_BUCKETS = 32
MAX_DISTANCE = 128
EPS = 1e-6

MOBA_W = MOBA_HEADS * HEAD_DIM
MLA_W = MLA_HEADS * MLA_V
MEM_W = MEM_HEADS * MEM_HEAD_DIM

LANES = 128
MXU_DIM = 256
V7X_VMEM_BYTES = 64 * 1024 * 1024
VMEM_HEADROOM_BYTES = 4 * 1024 * 1024
KERNEL_TEMP_BYTES = 8 * 1024 * 1024

BF16 = jnp.bfloat16
F32 = jnp.float32
LOG2E = 1.4426950408889634

KR_PAD = MXU_DIM
OFF_GL = 0
OFF_QA = OFF_GL + 3 * D_MODEL
OFF_KA = OFF_QA + MOBA_W
OFF_VA = OFF_KA + MOBA_W
OFF_ZA = OFF_VA + MOBA_W
OFF_ZB = OFF_ZA + MOBA_W
OFF_CQ = OFF_ZB + MLA_W
OFF_CKV = OFF_CQ + MLA_Q_LORA
OFF_KR = OFF_CKV + MLA_KV_LORA
OFF_QM = OFF_KR + KR_PAD
OFF_ZM = OFF_QM + MEM_W
IN_WIDTH_P = OFF_ZM + MEM_W

TQ = MOBA_BLOCK
BF16_TILE_ROWS = 16
ONES_ROWS = BF16_TILE_ROWS
VT_ROWS = MLA_V + ONES_ROWS
MLA_QK = MXU_DIM

W_TILE = 1024
W_SRC_UNIT = 64
NORM_CHUNK_ROWS = 64


def _vmem_limit(block_bytes):
    return int(min(block_bytes + KERNEL_TEMP_BYTES, V7X_VMEM_BYTES - VMEM_HEADROOM_BYTES))


def _t5_thresholds():
    max_exact = N_BUCKETS // 2

    def bucket(d):
        if d < max_exact:
            return d
        large = max_exact + int(math.log(d / max_exact) / math.log(MAX_DISTANCE / max_exact)
                                * (N_BUCKETS - max_exact))
        return min(large, N_BUCKETS - 1)

    thr, d = [], 0
    for b in range(1, N_BUCKETS):
        while bucket(d) < b:
            d += 1
        thr.append(d)
    return tuple(thr)


T5_THRESHOLDS = _t5_thresholds()
assert T5_THRESHOLDS[-1] <= MOBA_BLOCK + 1


def _norm_rows(x_ref, g_ref, h_ref, chunk):
    def body(r, carry):
        rows = pl.ds(pl.multiple_of(r * chunk, chunk), chunk)
        xv = x_ref[rows, :]
        ms = jnp.mean(xv * xv, axis=-1, keepdims=True)
        h_ref[rows, :] = (xv * lax.rsqrt(ms + EPS) * g_ref[...]).astype(BF16)
        return carry
    lax.fori_loop(0, x_ref.shape[0] // chunk, body, 0, unroll=4)


def _norm_matmul_kernel(x_ref, g_ref, w_ref, o_ref, h_ref, *, chunk):
    pl.when(pl.program_id(1) == 0)(functools.partial(_norm_rows, x_ref, g_ref, h_ref, chunk))
    o_ref[...] = jnp.dot(h_ref[...], w_ref[...].astype(BF16), preferred_element_type=F32).astype(o_ref.dtype)


def _in_proj_kernel(src_ref, x_ref, g_ref, wt_ref, cs_ref, o_ref, h_ref, *, chunk):
    del src_ref
    pl.when(pl.program_id(1) == 0)(functools.partial(_norm_rows, x_ref, g_ref, h_ref, chunk))
    acc = lax.dot_general(h_ref[...], wt_ref[...].astype(BF16), (((1,), (1,)), ((), ())),
                          preferred_element_type=F32)
    o_ref[...] = (acc * cs_ref[...]).astype(o_ref.dtype)


def _in_proj(x, g, wt_all, layer, col_scale, *, tm):
    t, d = x.shape
    srcs = _w_in_tile_sources(wt_all.shape[1])
    unit = W_SRC_UNIT
    assert all(src % unit == 0 for src in srcs)
    src_units = jnp.asarray([src // unit for src in srcs], jnp.int32)
    est = 2 * tm * d * 4 + tm * d * 2 + 2 * W_TILE * d * 4 + W_TILE * d * 2 + 2 * tm * W_TILE * 2 + tm * W_TILE * 4
    return pl.pallas_call(
        functools.partial(_in_proj_kernel, chunk=NORM_CHUNK_ROWS),
        grid_spec=pltpu.PrefetchScalarGridSpec(
            num_scalar_prefetch=1,
            grid=(t // tm, len(srcs)),
            in_specs=[
                pl.BlockSpec((tm, d), lambda i, j, src: (i, 0)),
                pl.BlockSpec((1, d), lambda i, j, src: (0, 0)),
                pl.BlockSpec((pl.Squeezed(), pl.Element(W_TILE), pl.Element(d)),
                             lambda i, j, src: (layer, src[j] * unit, 0)),
                pl.BlockSpec((1, W_TILE), lambda i, j, src: (0, j)),
            ],
            out_specs=pl.BlockSpec((tm, W_TILE), lambda i, j, src: (i, j)),
            scratch_shapes=[pltpu.VMEM((tm, d), BF16)],
        ),
        out_shape=jax.ShapeDtypeStruct((t, IN_WIDTH_P), BF16),
        compiler_params=pltpu.CompilerParams(
            dimension_semantics=("arbitrary", "arbitrary"), vmem_limit_bytes=_vmem_limit(est)),
        name="in_proj",
    )(src_units, x, g.reshape(1, d), wt_all, col_scale)


def _norm_matmul(x, g, w_all, layer, *, tm, tn):
    t, d = x.shape
    n = w_all.shape[2]
    est = 2 * tm * d * 4 + tm * d * 2 + 2 * d * tn * 4 + d * tn * 2 + 2 * tm * tn * 2 + tm * tn * 4
    return pl.pallas_call(
        functools.partial(_norm_matmul_kernel, chunk=NORM_CHUNK_ROWS),
        grid=(t // tm, n // tn),
        in_specs=[
            pl.BlockSpec((tm, d), lambda i, j: (i, 0)),
            pl.BlockSpec((1, d), lambda i, j: (0, 0)),
            pl.BlockSpec((pl.Squeezed(), d, tn), lambda i, j: (layer, 0, j)),
        ],
        out_specs=pl.BlockSpec((tm, tn), lambda i, j: (i, j)),
        out_shape=jax.ShapeDtypeStruct((t, n), BF16),
        scratch_shapes=[pltpu.VMEM((tm, d), BF16)],
        compiler_params=pltpu.CompilerParams(
            dimension_semantics=("arbitrary", "arbitrary"), vmem_limit_bytes=_vmem_limit(est)),
        name="norm_matmul",
    )(x, g.reshape(1, d), w_all)


def _bias_tiles_kernel(rb_ref, own_ref, prev_ref):
    key = lax.broadcasted_iota(jnp.int32, (TQ, TQ), 0)
    qry = lax.broadcasted_iota(jnp.int32, (TQ, TQ), 1)
    d_own = qry - key
    d_prev = d_own + MOBA_BLOCK

    def lookup(dist, h):
        val = jnp.zeros(dist.shape, F32) + rb_ref[0, h]
        for b in range(1, N_BUCKETS):
            val = jnp.where(dist >= T5_THRESHOLDS[b - 1], rb_ref[b, h], val)
        return val * LOG2E

    for h in range(own_ref.shape[0]):
        own_ref[h] = jnp.where(d_own >= 0, lookup(d_own, h), -jnp.inf)
        prev_ref[h] = lookup(d_prev, h)


def _bias_tiles(rel_bias):
    heads = rel_bias.shape[1]
    tile = jax.ShapeDtypeStruct((heads, TQ, TQ), F32)
    spec = pl.BlockSpec((heads, TQ, TQ), lambda i: (0, 0, 0))
    return pl.pallas_call(
        _bias_tiles_kernel,
        grid=(1,),
        in_specs=[pl.BlockSpec(memory_space=pltpu.SMEM)],
        out_specs=[spec, spec],
        out_shape=[tile, tile],
        name="bias_tiles",
    )(rel_bias)


def _build_vt_block(v_ref, vt_ref, j):
    dv = v_ref.shape[-1]
    blk = slice(j * TQ, (j + 1) * TQ)
    vt_ref[0:dv, blk] = v_ref[blk, :].astype(F32).T.astype(BF16)
    row = lax.broadcasted_iota(jnp.int32, (ONES_ROWS, TQ), 0)
    vt_ref[dv:dv + ONES_ROWS, blk] = jnp.where(row == 0, 1.0, 0.0).astype(BF16)


def _attention(n_tiles, q_tile_of, k_block_of, terms_of, prepare, vt_ref, bufs, emit):
    nt = (((1,), (1,)), ((), ()))
    dv = vt_ref.shape[0] - ONES_ROWS
    t_bufs, p_buf = bufs[:2], bufs[2]
    state = {}

    def stage1(i):
        prepare(i)
        q = q_tile_of(i)
        adds, consts, sels, rider = terms_of(i, q)
        st = state[i] = dict(consts=consts, sels=sels, m=None)
        t_buf = t_bufs[i % 2]

        def item(j):
            blk = slice(j * TQ, (j + 1) * TQ)
            lhs = k_block_of(j)
            if j == 0 and rider is not None:
                extra_rows, sels_from = rider
                lhs = jnp.concatenate([lhs, extra_rows], axis=0)
            t = lax.dot_general(lhs, q, nt, preferred_element_type=F32)
            if j == 0 and rider is not None:
                st["sels"] = sels_from(t[TQ:, :])
                t = t[0:TQ, :]
            if adds[j] is not None:
                t = t + adds[j][...]
            t_buf[blk, :] = t
            mj = jnp.max(t, axis=0, keepdims=True) + consts[j]
            if st["sels"][j] is not None:
                mj = jnp.where(st["sels"][j], mj, -jnp.inf)
            st["m"] = mj if st["m"] is None else jnp.maximum(st["m"], mj)
        return [functools.partial(item, j) for j in range(len(consts))]

    def stage2(i):
        st = state.pop(i)
        t_buf = t_bufs[i % 2]

        def item(j):
            blk = slice(j * TQ, (j + 1) * TQ)
            off = st["m"] - st["consts"][j]
            if st["sels"][j] is not None:
                off = jnp.where(st["sels"][j], off, jnp.inf)
            p_buf[blk, :] = jnp.exp2(t_buf[blk, :] - off).astype(BF16)

        def finish():
            n_keys = len(st["consts"]) * TQ
            acc = jnp.dot(vt_ref[:, 0:n_keys], p_buf[0:n_keys, :], preferred_element_type=F32)
            emit(i, (acc[0:dv, :] / acc[dv:dv + 1, :]).T)
        return [functools.partial(item, j) for j in range(len(st["consts"]))], finish

    for item in stage1(0):
        item()
        yield
    for i in range(n_tiles):
        ahead = stage1(i + 1) if i + 1 < n_tiles else []
        behind, finish = stage2(i)
        for k in range(max(len(ahead), len(behind))):
            if k < len(ahead):
                ahead[k]()
            if k < len(behind):
                behind[k]()
            yield
        finish()


def _run_streams(streams):
    active = list(streams)
    while active:
        for stream in list(active):
            if next(stream, StopIteration) is StopIteration:
                active.remove(stream)


def _cast_plan(weights, layer, n_b, n_h):
    n_steps = n_b * n_h
    ins, outs, shapes, nbytes = [], [], [], 0
    for w in weights:
        _, rows, cols = w.shape
        assert rows % (n_steps * BF16_TILE_ROWS) == 0
        blk_rows = rows // n_steps
        ins.append(pl.BlockSpec((pl.Squeezed(), blk_rows, cols), lambda bi, h: (layer, bi * n_h + h, 0)))
        outs.append(pl.BlockSpec((blk_rows, cols), lambda bi, h: (bi * n_h + h, 0)))
        shapes.append(jax.ShapeDtypeStruct((rows, cols), BF16))
        nbytes += 2 * blk_rows * cols * (4 + 2)
    return ins, outs, shapes, nbytes


def _cast_row_blocks(rest, n_cast):
    cast_in, (o_ref, *cast_out), scratch = rest[:n_cast], rest[n_cast:2 * n_cast + 1], rest[2 * n_cast + 1:]
    for w_ref, wb_ref in zip(cast_in, cast_out):
        wb_ref[...] = w_ref[...].astype(BF16)
    return o_ref, scratch


def _silu_gate(o, z):
    zf = z.astype(F32)
    return (o * (zf / (1.0 + jnp.exp(-zf)))).astype(BF16)


def _write_causal_tile(mask_ref):
    key = lax.broadcasted_iota(jnp.int32, (TQ, TQ), 0)
    qry = lax.broadcasted_iota(jnp.int32, (TQ, TQ), 1)
    mask_ref[...] = jnp.where(key <= qry, 0.0, -jnp.inf).astype(F32)


def _head_view(ref, head, width):
    return ref.at[0, :, head * width:(head + 1) * width]


def _moba_kernel(rb_ref, q_ref, k_ref, v_ref, z_ref, own_ref, prev_ref, *rest, heads, n_cast):
    o_ref, scratch = _cast_row_blocks(rest, n_cast)
    per_head = len(scratch) // heads
    _run_streams([
        _moba_head(rb_ref, pl.program_id(1) * heads + s, _head_view(q_ref, s, HEAD_DIM),
                   _head_view(k_ref, s, HEAD_DIM), _head_view(v_ref, s, HEAD_DIM),
                   _head_view(z_ref, s, HEAD_DIM), own_ref.at[s], prev_ref.at[s],
                   _head_view(o_ref, s, HEAD_DIM), *scratch[s * per_head:(s + 1) * per_head])
        for s in range(heads)])


def _moba_head(rb_ref, head, q_ref, k_ref, v_ref, z_ref, own_ref, prev_ref, o_ref, vt_ref, t0, t1, pb, km_ref):
    seq = q_ref.shape[0]
    n_tiles = seq // TQ
    far_const = rb_ref[N_BUCKETS - 1, head] * LOG2E
    km_ref[...] = jnp.zeros(km_ref.shape, F32)

    def prepare(i):
        _build_vt_block(v_ref, vt_ref, i)
        k_blk = k_ref[i * TQ:(i + 1) * TQ, :].astype(F32)
        km_ref[i:i + 1, :] = jnp.sum(k_blk, axis=0, keepdims=True) * (1.0 / MOBA_BLOCK)

    def block_mask_from(i, parts):
        rows = km_ref.shape[0]
        gate = parts[0:rows] + parts[rows:2 * rows] + parts[2 * rows:3 * rows]
        sels = []
        for j in range(i):
            gj = gate[j:j + 1, :]
            cnt = jnp.zeros(gj.shape, F32)
            for jp in range(i):
                if jp == j:
                    continue
                gp = gate[jp:jp + 1, :]
                beats = (gp >= gj) if jp < j else (gp > gj)
                cnt = cnt + jnp.where(beats, 1.0, 0.0)
            sels.append(cnt < MOBA_TOPK)
        return sels + [None]

    def terms_of(i, q_tile):
        adds = [None] * (i + 1)
        consts = [far_const] * (i + 1)
        adds[i], consts[i] = own_ref, 0.0
        if i >= 1:
            adds[i - 1], consts[i - 1] = prev_ref, 0.0
        if i <= MOBA_TOPK:
            return adds, consts, [None] * (i + 1), None
        k_mean = km_ref[...]
        km1 = k_mean.astype(BF16)
        rem = k_mean - km1.astype(F32)
        km2 = rem.astype(BF16)
        km3 = (rem - km2.astype(F32)).astype(BF16)
        rider = (jnp.concatenate([km1, km2, km3], axis=0), functools.partial(block_mask_from, i))
        return adds, consts, None, rider

    def emit(i, o):
        rows = slice(i * TQ, (i + 1) * TQ)
        o_ref[rows, :] = _silu_gate(o, z_ref[rows, :])

    return _attention(n_tiles, lambda i: q_ref[i * TQ:(i + 1) * TQ, :],
                      lambda j: k_ref[j * TQ:(j + 1) * TQ, :], terms_of, prepare, vt_ref, (t0, t1, pb), emit)


HEADS_PER_STEP = 4


def _attn_scratch(n_keys, extra=()):
    per_head = [pltpu.VMEM((VT_ROWS, n_keys), BF16), pltpu.VMEM((n_keys, TQ), F32),
                pltpu.VMEM((n_keys, TQ), F32), pltpu.VMEM((n_keys, TQ), BF16), *extra]
    return per_head * HEADS_PER_STEP


def _moba_attn(u3, rel_bias, own, prev, cast_weights, layer):
    b, s, _ = u3.shape
    assert s // MOBA_BLOCK <= BF16_TILE_ROWS
    hp = HEADS_PER_STEP
    width = hp * HEAD_DIM
    n_h = MOBA_HEADS // hp
    col = lambda off: (lambda bi, h: (bi, 0, off // width + h))
    blk = (1, s, width)
    tile_spec = pl.BlockSpec((hp, TQ, TQ), lambda bi, h: (h, 0, 0))
    cast_in, cast_out, cast_shapes, cast_bytes = _cast_plan(cast_weights, layer, b, n_h)
    est = (10 * s * width * 2 + 4 * hp * TQ * TQ * 4 + hp * (VT_ROWS * s * 2 + 2 * s * TQ * 4 + s * TQ * 2)
           + cast_bytes)
    outs = pl.pallas_call(
        functools.partial(_moba_kernel, heads=hp, n_cast=len(cast_weights)),
        grid=(b, n_h),
        in_specs=[
            pl.BlockSpec(memory_space=pltpu.SMEM),
            pl.BlockSpec(blk, col(OFF_QA)),
            pl.BlockSpec(blk, col(OFF_KA)),
            pl.BlockSpec(blk, col(OFF_VA)),
            pl.BlockSpec(blk, col(OFF_ZA)),
            tile_spec, tile_spec,
            *cast_in,
        ],
        out_specs=[pl.BlockSpec(blk, lambda bi, h: (bi, 0, h)), *cast_out],
        out_shape=[jax.ShapeDtypeStruct((b, s, MOBA_W), BF16), *cast_shapes],
        scratch_shapes=_attn_scratch(s, extra=(pltpu.VMEM((BF16_TILE_ROWS, HEAD_DIM), F32),)),
        compiler_params=pltpu.CompilerParams(
            dimension_semantics=("arbitrary", "arbitrary"), vmem_limit_bytes=_vmem_limit(est)),
        name="moba_attn",
    )(rel_bias, u3, u3, u3, u3, own, prev, *cast_weights)
    return outs[0], outs[1:]


def _mla_prep_kernel(cq_ref, ckv_ref, kr_ref, gq_ref, gkv_ref, wuq_ref, wukv_ref, cos_ref, sin_ref,
                     q_out, k_out, v_out, *, q_scale):
    def rms(x_ref, g_ref):
        xf = x_ref[...].astype(F32)
        ms = jnp.mean(xf * xf, axis=-1, keepdims=True)
        return (xf * lax.rsqrt(ms + EPS) * g_ref[...]).astype(BF16)

    cos = cos_ref[...]
    sin = sin_ref[...]
    half = MLA_ROPE // 2
    first_half = lax.broadcasted_iota(jnp.int32, cos.shape, 1) < half

    def rope(xr):
        partner = jnp.where(first_half, pltpu.roll(xr, LANES - half, 1), pltpu.roll(xr, half, 1))
        return xr * cos + partner * sin

    qb = jnp.dot(rms(cq_ref, gq_ref), wuq_ref[...], preferred_element_type=F32) * q_scale
    for h in range(MLA_HEADS):
        base = h * MLA_QK
        q_out[:, base:base + MLA_NOPE] = qb[:, base:base + MLA_NOPE].astype(BF16)
        q_out[:, base + MLA_NOPE:base + MLA_QK] = rope(qb[:, base + MLA_NOPE:base + MLA_QK]).astype(BF16)

    kvb = jnp.dot(rms(ckv_ref, gkv_ref), wukv_ref[...], preferred_element_type=F32)
    in_rope = lax.broadcasted_iota(jnp.int32, cos.shape, 1) < MLA_ROPE
    k_rope = rope(jnp.where(in_rope, kr_ref[:, 0:LANES].astype(F32), 0.0)).astype(BF16)
    for h in range(MLA_HEADS):
        base = h * MLA_QK
        k_out[:, base:base + MLA_NOPE] = kvb[:, h * MLA_NOPE:(h + 1) * MLA_NOPE].astype(BF16)
        k_out[:, base + MLA_NOPE:base + MLA_QK] = k_rope
    v_out[...] = kvb[:, MLA_W:2 * MLA_W].astype(BF16)


def _mla_prep(u, g_cq, g_ckv, wuq_p, wukv_p, cos_t, sin_t, *, seq, tm):
    t = u.shape[0]
    s_tiles = seq // tm
    const = lambda i: (0, 0)
    qk_shape = jax.ShapeDtypeStruct((t, MLA_HEADS * MLA_QK), BF16)
    est = (2 * tm * (MLA_Q_LORA + 2 * MLA_KV_LORA) * 2 + 2 * (wuq_p.size + wukv_p.size) * 2
           + 4 * tm * LANES * 4 + 2 * tm * 5 * MLA_W * 2 + 4 * tm * 2 * MLA_W * 4)
    return pl.pallas_call(
        functools.partial(_mla_prep_kernel, q_scale=(MLA_NOPE + MLA_ROPE) ** -0.5 * LOG2E),
        grid=(t // tm,),
        in_specs=[
            pl.BlockSpec((tm, MLA_Q_LORA), lambda i: (i, OFF_CQ // MLA_Q_LORA)),
            pl.BlockSpec((tm, MLA_KV_LORA), lambda i: (i, OFF_CKV // MLA_KV_LORA)),
            pl.BlockSpec((tm, KR_PAD), lambda i: (i, OFF_KR // KR_PAD)),
            pl.BlockSpec((1, MLA_Q_LORA), const),
            pl.BlockSpec((1, MLA_KV_LORA), const),
            pl.BlockSpec(wuq_p.shape, const),
            pl.BlockSpec(wukv_p.shape, const),
            pl.BlockSpec((tm, LANES), lambda i: (i % s_tiles, 0)),
            pl.BlockSpec((tm, LANES), lambda i: (i % s_tiles, 0)),
        ],
        out_specs=[
            pl.BlockSpec((tm, MLA_HEADS * MLA_QK), lambda i: (i, 0)),
            pl.BlockSpec((tm, MLA_HEADS * MLA_QK), lambda i: (i, 0)),
            pl.BlockSpec((tm, MLA_W), lambda i: (i, 0)),
        ],
        out_shape=[qk_shape, qk_shape, jax.ShapeDtypeStruct((t, MLA_W), BF16)],
        compiler_params=pltpu.CompilerParams(
            dimension_semantics=("arbitrary",), vmem_limit_bytes=_vmem_limit(est)),
        name="mla_prep",
    )(u, u, u, g_cq.reshape(1, -1), g_ckv.reshape(1, -1), wuq_p, wukv_p, cos_t, sin_t)


def _mla_attn_kernel(q_ref, k_ref, v_ref, z_ref, o_ref, mask_ref, *scratch, heads):
    _write_causal_tile(mask_ref)
    per_head = len(scratch) // heads
    _run_streams([
        _mla_head(_head_view(q_ref, s, MLA_QK), _head_view(k_ref, s, MLA_QK), _head_view(v_ref, s, MLA_V),
                  _head_view(z_ref, s, MLA_V), _head_view(o_ref, s, MLA_V), mask_ref,
                  *scratch[s * per_head:(s + 1) * per_head])
        for s in range(heads)])


def _mla_head(q_ref, k_ref, v_ref, z_ref, o_ref, mask_ref, vt_ref, t0, t1, pb):
    seq = q_ref.shape[0]

    def terms_of(i, q_tile):
        return [None] * i + [mask_ref], [0.0] * (i + 1), [None] * (i + 1), None

    def emit(i, o):
        rows = slice(i * TQ, (i + 1) * TQ)
        o_ref[rows, :] = _silu_gate(o, z_ref[rows, :])

    return _attention(seq // TQ, lambda i: q_ref[i * TQ:(i + 1) * TQ, :],
                      lambda j: k_ref[j * TQ:(j + 1) * TQ, :], terms_of,
                      functools.partial(_build_vt_block, v_ref, vt_ref), vt_ref, (t0, t1, pb), emit)


def _mla_attn(q3, k3, v3, u3):
    b, s, _ = q3.shape
    hp = HEADS_PER_STEP
    est = (4 * s * hp * MLA_QK * 2 + 6 * s * hp * MLA_V * 2 + hp * (VT_ROWS * s * 2 + 2 * s * TQ * 4 + s * TQ * 2)
           + TQ * TQ * 4)
    return pl.pallas_call(
        functools.partial(_mla_attn_kernel, heads=hp),
        grid=(b, MLA_HEADS // hp),
        in_specs=[
            pl.BlockSpec((1, s, hp * MLA_QK), lambda bi, h: (bi, 0, h)),
            pl.BlockSpec((1, s, hp * MLA_QK), lambda bi, h: (bi, 0, h)),
            pl.BlockSpec((1, s, hp * MLA_V), lambda bi, h: (bi, 0, h)),
            pl.BlockSpec((1, s, hp * MLA_V), lambda bi, h: (bi, 0, OFF_ZB // (hp * MLA_V) + h)),
        ],
        out_specs=pl.BlockSpec((1, s, hp * MLA_V), lambda bi, h: (bi, 0, h)),
        out_shape=jax.ShapeDtypeStruct((b, s, MLA_W), BF16),
        scratch_shapes=[pltpu.VMEM((TQ, TQ), F32)] + _attn_scratch(s),
        compiler_params=pltpu.CompilerParams(
            dimension_semantics=("arbitrary", "arbitrary"), vmem_limit_bytes=_vmem_limit(est)),
        name="mla_attn",
    )(q3, k3, v3, u3)


def _mem_attn_kernel(q_ref, k_ref, v_ref, z_ref, *rest, heads, n_cast):
    o_ref, scratch = _cast_row_blocks(rest, n_cast)
    assert k_ref.shape[1] == TQ
    per_head = len(scratch) // heads
    d = MEM_HEAD_DIM
    _run_streams([
        _mem_head(_head_view(q_ref, s, d), _head_view(k_ref, s, d), _head_view(v_ref, s, d),
                  _head_view(z_ref, s, d), _head_view(o_ref, s, d), *scratch[s * per_head:(s + 1) * per_head])
        for s in range(heads)])


def _mem_head(q_ref, k_ref, v_ref, z_ref, o_ref, vt_ref, t0, t1, pb):
    seq = q_ref.shape[0]

    def prepare(i):
        if i == 0:
            _build_vt_block(v_ref, vt_ref, 0)

    def emit(i, o):
        rows = slice(i * TQ, (i + 1) * TQ)
        o_ref[rows, :] = _silu_gate(o, z_ref[rows, :])

    return _attention(seq // TQ, lambda i: q_ref[i * TQ:(i + 1) * TQ, :], lambda j: k_ref[...],
                      lambda i, q_tile: ([None], [0.0], [None], None), prepare, vt_ref, (t0, t1, pb), emit)


def _mem_attn(u3, kv3, cast_weights, layer):
    b, s, _ = u3.shape
    m = kv3.shape[1]
    hp = HEADS_PER_STEP
    d = hp * MEM_HEAD_DIM
    n_h = MEM_HEADS // hp
    cast_in, cast_out, cast_shapes, cast_bytes = _cast_plan(cast_weights, layer, b, n_h)
    est = 6 * s * d * 2 + 4 * m * d * 2 + hp * (VT_ROWS * m * 2 + 2 * m * TQ * 4 + m * TQ * 2) + cast_bytes
    outs = pl.pallas_call(
        functools.partial(_mem_attn_kernel, heads=hp, n_cast=len(cast_weights)),
        grid=(b, n_h),
        in_specs=[
            pl.BlockSpec((1, s, d), lambda bi, h: (bi, 0, OFF_QM // d + h)),
            pl.BlockSpec((1, m, d), lambda bi, h: (bi, 0, h)),
            pl.BlockSpec((1, m, d), lambda bi, h: (bi, 0, MEM_W // d + h)),
            pl.BlockSpec((1, s, d), lambda bi, h: (bi, 0, OFF_ZM // d + h)),
            *cast_in,
        ],
        out_specs=[pl.BlockSpec((1, s, d), lambda bi, h: (bi, 0, h)), *cast_out],
        out_shape=[jax.ShapeDtypeStruct((b, s, MEM_W), BF16), *cast_shapes],
        scratch_shapes=_attn_scratch(m),
        compiler_params=pltpu.CompilerParams(
            dimension_semantics=("arbitrary", "arbitrary"), vmem_limit_bytes=_vmem_limit(est)),
        name="mem_attn",
    )(u3, kv3, kv3, u3, *cast_weights)
    return outs[0], outs[1:]


def _merge_kernel(ga_ref, gb_ref, gm_ref, gl_ref, x_ref, wpa_ref, wpb_ref, wpm_ref, wout_ref, gf_ref,
                  o_ref, *, final_norm):
    d = x_ref.shape[1]

    def gated(idx, g_ref, w_ref):
        logit = gl_ref[:, idx * d:(idx + 1) * d].astype(F32)
        return jnp.dot(g_ref[...], w_ref[...], preferred_element_type=F32) / (1.0 + jnp.exp(-logit))

    y = gated(0, ga_ref, wpa_ref) + gated(1, gb_ref, wpb_ref) + gated(2, gm_ref, wpm_ref)
    r = x_ref[...] + jnp.dot(y.astype(BF16), wout_ref[...], preferred_element_type=F32)
    if final_norm:
        ms = jnp.mean(r * r, axis=-1, keepdims=True)
        r = r * lax.rsqrt(ms + EPS) * gf_ref[...]
    o_ref[...] = r


def _merge(ga, gb, gm, u, x, wpa, wpb, wpm, wout, g_final, *, tm, final_norm):
    t, d = x.shape
    const = lambda i: (0, 0)
    resident = lambda w: pl.BlockSpec(w.shape, const, pipeline_mode=pl.Buffered(1))
    rows = lambda width: pl.BlockSpec((tm, width), lambda i: (i, 0))
    w_bytes = (wpa.size + wpb.size + wpm.size + wout.size) * 2
    est = w_bytes + 2 * tm * (2 * MOBA_W + MEM_W + 3 * d) * 2 + 4 * tm * d * 4 + 6 * tm * d * 4
    return pl.pallas_call(
        functools.partial(_merge_kernel, final_norm=final_norm),
        grid=(t // tm,),
        in_specs=[
            rows(MOBA_W), rows(MLA_W), rows(MEM_W),
            pl.BlockSpec((tm, 3 * d), lambda i: (i, OFF_GL // (3 * d))),
            rows(d),
            resident(wpa), resident(wpb), resident(wpm), resident(wout),
            pl.BlockSpec((1, d), const),
        ],
        out_specs=rows(d),
        out_shape=jax.ShapeDtypeStruct((t, d), F32),
        compiler_params=pltpu.CompilerParams(
            dimension_semantics=("arbitrary",), vmem_limit_bytes=_vmem_limit(est)),
        name="merge",
    )(ga, gb, gm, u, x, wpa, wpb, wpm, wout, g_final.reshape(1, d))


def _w_in_tile_sources(n_cols):
    o_cq = 4 * MOBA_W
    o_zb = o_cq + MLA_Q_LORA + MLA_KV_LORA + MLA_ROPE
    o_qm = o_zb + MLA_W
    o_gl = o_qm + 2 * MEM_W
    srcs = ([o_gl + W_TILE * k for k in range(3 * D_MODEL // W_TILE)]
            + [W_TILE * k for k in range(4 * MOBA_W // W_TILE)] + [o_zb, o_cq, o_qm])
    assert len(srcs) * W_TILE == IN_WIDTH_P and o_gl + 3 * D_MODEL == n_cols
    assert all(src + W_TILE <= n_cols for src in srcs)
    return srcs


def _in_col_scale():
    cs = jnp.ones((1, IN_WIDTH_P), F32)
    cs = cs.at[:, OFF_QA:OFF_QA + MOBA_W].set(HEAD_DIM ** -0.5 * LOG2E)
    return cs.at[:, OFF_QM:OFF_QM + MEM_W].set(MEM_HEAD_DIM ** -0.5 * LOG2E)


def _regroup_w_uq(w):
    r = w.shape[0]
    w3 = w.reshape(r, MLA_HEADS, MLA_NOPE + MLA_ROPE)
    pad = jnp.zeros((r, MLA_HEADS, MLA_QK - MLA_NOPE - MLA_ROPE), w.dtype)
    return jnp.concatenate([w3, pad], axis=-1).reshape(r, MLA_HEADS * MLA_QK).astype(BF16)


def _regroup_w_ukv(w):
    r = w.shape[0]
    w3 = w.reshape(r, MLA_HEADS, MLA_NOPE + MLA_V)
    return jnp.concatenate([w3[:, :, :MLA_NOPE].reshape(r, MLA_W),
                            w3[:, :, MLA_NOPE:].reshape(r, MLA_W)], axis=1).astype(BF16)


def _rope_tables(seq):
    half = MLA_ROPE // 2
    inv = ROPE_THETA ** (-jnp.arange(half, dtype=F32) / half)
    ang = jnp.arange(seq, dtype=jnp.int32).astype(F32)[:, None] * inv[None, :]
    cos, sin = jnp.cos(ang), jnp.sin(ang)
    pad = LANES - MLA_ROPE
    cos_t = jnp.concatenate([cos, cos, jnp.ones((seq, pad), F32)], axis=1)
    sin_t = jnp.concatenate([-sin, sin, jnp.zeros((seq, pad), F32)], axis=1)
    return cos_t, sin_t


def kernel(x, mem, g_norm, w_in, g_cq, w_uq, g_ckv, w_ukv, g_mem, w_mem_kv, rel_bias,
           w_p_moba, w_p_mla, w_p_mem, w_out, g_final):
    b, s, d = x.shape
    m = mem.shape[1]
    depth = w_in.shape[0]
    t = b * s
    assert d == D_MODEL and s % TQ == 0 and m == TQ

    own, prev = _bias_tiles(rel_bias)
    cos_t, sin_t = _rope_tables(s)
    mem2 = mem.reshape(b * m, d)
    xs = x.reshape(t, d)
    in_scale = _in_col_scale()
    w_in_t = jnp.swapaxes(w_in, 1, 2)
    for l in range(depth):
        u = _in_proj(xs, g_norm[l], w_in_t, l, in_scale, tm=1024)
        u3 = u.reshape(b, s, IN_WIDTH_P)
        ga, (wout,) = _moba_attn(u3, rel_bias, own, prev, (w_out,), l)
        q2, k2, v2 = _mla_prep(u, g_cq[l], g_ckv[l], _regroup_w_uq(w_uq[l]), _regroup_w_ukv(w_ukv[l]),
                               cos_t, sin_t, seq=s, tm=512)
        gb = _mla_attn(q2.reshape(b, s, -1), k2.reshape(b, s, -1), v2.reshape(b, s, -1), u3)
        kvm = _norm_matmul(mem2, g_mem[l], w_mem_kv, l, tm=b * m, tn=MEM_W)
        gm, (wpa, wpb, wpm) = _mem_attn(u3, kvm.reshape(b, m, 2 * MEM_W), (w_p_moba, w_p_mla, w_p_mem), l)
        xs = _merge(ga.reshape(t, MOBA_W), gb.reshape(t, MLA_W), gm.reshape(t, MEM_W), u, xs,
                    wpa, wpb, wpm, wout, g_final, tm=256, final_norm=(l == depth - 1))
    return xs.reshape(b, s, d)
```

```python
import functools
import math

import jax
import jax.numpy as jnp
from jax import lax
from jax.experimental import pallas as pl
from jax.experimental.pallas import tpu as pltpu

D_MODEL = 2048
MOBA_HEADS = 8
HEAD_DIM = 128
MOBA_BLOCK = 256
MOBA_TOPK = 3
MLA_HEADS = 8
MLA_Q_LORA = 512
MLA_KV_LORA = 256
MLA_NOPE = 128
MLA_ROPE = 64
MLA_V = 128
ROPE_THETA = 10000.0
MEM_HEADS = 4
MEM_HEAD_DIM = 128
N_BUCKETS = 32
MAX_DISTANCE = 128
EPS = 1e-6

MOBA_W = MOBA_HEADS * HEAD_DIM
MLA_W = MLA_HEADS * MLA_V
MEM_W = MEM_HEADS * MEM_HEAD_DIM

LANES = 128
MXU_DIM = 256
V7X_VMEM_BYTES = 64 * 1024 * 1024
VMEM_HEADROOM_BYTES = 4 * 1024 * 1024
KERNEL_TEMP_BYTES = 8 * 1024 * 1024

BF16 = jnp.bfloat16
F32 = jnp.float32
LOG2E = 1.4426950408889634

KR_PAD = MXU_DIM
OFF_GL = 0
OFF_QA = OFF_GL + 3 * D_MODEL
OFF_KA = OFF_QA + MOBA_W
OFF_VA = OFF_KA + MOBA_W
OFF_ZA = OFF_VA + MOBA_W
OFF_ZB = OFF_ZA + MOBA_W
OFF_CQ = OFF_ZB + MLA_W
OFF_CKV = OFF_CQ + MLA_Q_LORA
OFF_KR = OFF_CKV + MLA_KV_LORA
OFF_QM = OFF_KR + KR_PAD
OFF_ZM = OFF_QM + MEM_W
IN_WIDTH_P = OFF_ZM + MEM_W

TQ = MOBA_BLOCK
BF16_TILE_ROWS = 16
ONES_ROWS = BF16_TILE_ROWS
VT_ROWS = MLA_V + ONES_ROWS
MLA_QK = MXU_DIM

W_TILE = 1024
W_SRC_UNIT = 64
NORM_CHUNK_ROWS = 64


def _vmem_limit(block_bytes):
    return int(min(block_bytes + KERNEL_TEMP_BYTES, V7X_VMEM_BYTES - VMEM_HEADROOM_BYTES))


def _t5_thresholds():
    max_exact = N_BUCKETS // 2

    def bucket(d):
        if d < max_exact:
            return d
        large = max_exact + int(math.log(d / max_exact) / math.log(MAX_DISTANCE / max_exact)
                                * (N_BUCKETS - max_exact))
        return min(large, N_BUCKETS - 1)

    thr, d = [], 0
    for b in range(1, N_BUCKETS):
        while bucket(d) < b:
            d += 1
        thr.append(d)
    return tuple(thr)


T5_THRESHOLDS = _t5_thresholds()
assert T5_THRESHOLDS[-1] <= MOBA_BLOCK + 1


def _norm_rows(x_ref, g_ref, h_ref, chunk):
    def body(r, carry):
        rows = pl.ds(pl.multiple_of(r * chunk, chunk), chunk)
        xv = x_ref[rows, :]
        ms = jnp.mean(xv * xv, axis=-1, keepdims=True)
        h_ref[rows, :] = (xv * lax.rsqrt(ms + EPS) * g_ref[...]).astype(BF16)
        return carry
    lax.fori_loop(0, x_ref.shape[0] // chunk, body, 0, unroll=4)


def _norm_matmul_kernel(x_ref, g_ref, w_ref, o_ref, h_ref, *, chunk):
    pl.when(pl.program_id(1) == 0)(functools.partial(_norm_rows, x_ref, g_ref, h_ref, chunk))
    o_ref[...] = jnp.dot(h_ref[...], w_ref[...].astype(BF16), preferred_element_type=F32).astype(o_ref.dtype)


def _in_proj_kernel(src_ref, x_ref, g_ref, wt_ref, cs_ref, o_ref, h_ref, *, chunk):
    del src_ref
    pl.when(pl.program_id(1) == 0)(functools.partial(_norm_rows, x_ref, g_ref, h_ref, chunk))
    acc = lax.dot_general(h_ref[...], wt_ref[...].astype(BF16), (((1,), (1,)), ((), ())),
                          preferred_element_type=F32)
    o_ref[...] = (acc * cs_ref[...]).astype(o_ref.dtype)


def _in_proj(x, g, wt_all, layer, col_scale, *, tm):
    t, d = x.shape
    srcs = _w_in_tile_sources(wt_all.shape[1])
    unit = W_SRC_UNIT
    assert all(src % unit == 0 for src in srcs)
    src_units = jnp.asarray([src // unit for src in srcs], jnp.int32)
    est = 2 * tm * d * 4 + tm * d * 2 + 2 * W_TILE * d * 4 + W_TILE * d * 2 + 2 * tm * W_TILE * 2 + tm * W_TILE * 4
    return pl.pallas_call(
        functools.partial(_in_proj_kernel, chunk=NORM_CHUNK_ROWS),
        grid_spec=pltpu.PrefetchScalarGridSpec(
            num_scalar_prefetch=1,
            grid=(t // tm, len(srcs)),
            in_specs=[
                pl.BlockSpec((tm, d), lambda i, j, src: (i, 0)),
                pl.BlockSpec((1, d), lambda i, j, src: (0, 0)),
                pl.BlockSpec((pl.Squeezed(), pl.Element(W_TILE), pl.Element(d)),
                             lambda i, j, src: (layer, src[j] * unit, 0)),
                pl.BlockSpec((1, W_TILE), lambda i, j, src: (0, j)),
            ],
            out_specs=pl.BlockSpec((tm, W_TILE), lambda i, j, src: (i, j)),
            scratch_shapes=[pltpu.VMEM((tm, d), BF16)],
        ),
        out_shape=jax.ShapeDtypeStruct((t, IN_WIDTH_P), BF16),
        compiler_params=pltpu.CompilerParams(
            dimension_semantics=("arbitrary", "arbitrary"), vmem_limit_bytes=_vmem_limit(est)),
        name="in_proj",
    )(src_units, x, g.reshape(1, d), wt_all, col_scale)


def _norm_matmul(x, g, w_all, layer, *, tm, tn):
    t, d = x.shape
    n = w_all.shape[2]
    est = 2 * tm * d * 4 + tm * d * 2 + 2 * d * tn * 4 + d * tn * 2 + 2 * tm * tn * 2 + tm * tn * 4
    return pl.pallas_call(
        functools.partial(_norm_matmul_kernel, chunk=NORM_CHUNK_ROWS),
        grid=(t // tm, n // tn),
        in_specs=[
            pl.BlockSpec((tm, d), lambda i, j: (i, 0)),
            pl.BlockSpec((1, d), lambda i, j: (0, 0)),
            pl.BlockSpec((pl.Squeezed(), d, tn), lambda i, j: (layer, 0, j)),
        ],
        out_specs=pl.BlockSpec((tm, tn), lambda i, j: (i, j)),
        out_shape=jax.ShapeDtypeStruct((t, n), BF16),
        scratch_shapes=[pltpu.VMEM((tm, d), BF16)],
        compiler_params=pltpu.CompilerParams(
            dimension_semantics=("arbitrary", "arbitrary"), vmem_limit_bytes=_vmem_limit(est)),
        name="norm_matmul",
    )(x, g.reshape(1, d), w_all)


def _bias_tiles_kernel(rb_ref, own_ref, prev_ref):
    key = lax.broadcasted_iota(jnp.int32, (TQ, TQ), 0)
    qry = lax.broadcasted_iota(jnp.int32, (TQ, TQ), 1)
    d_own = qry - key
    d_prev = d_own + MOBA_BLOCK

    def lookup(dist, h):
        val = jnp.zeros(dist.shape, F32) + rb_ref[0, h]
        for b in range(1, N_BUCKETS):
            val = jnp.where(dist >= T5_THRESHOLDS[b - 1], rb_ref[b, h], val)
        return val * LOG2E

    for h in range(own_ref.shape[0]):
        own_ref[h] = jnp.where(d_own >= 0, lookup(d_own, h), -jnp.inf)
        prev_ref[h] = lookup(d_prev, h)


def _bias_tiles(rel_bias):
    heads = rel_bias.shape[1]
    tile = jax.ShapeDtypeStruct((heads, TQ, TQ), F32)
    spec = pl.BlockSpec((heads, TQ, TQ), lambda i: (0, 0, 0))
    return pl.pallas_call(
        _bias_tiles_kernel,
        grid=(1,),
        in_specs=[pl.BlockSpec(memory_space=pltpu.SMEM)],
        out_specs=[spec, spec],
        out_shape=[tile, tile],
        name="bias_tiles",
    )(rel_bias)


def _build_vt_block(v_ref, vt_ref, j):
    dv = v_ref.shape[-1]
    blk = slice(j * TQ, (j + 1) * TQ)
    vt_ref[0:dv, blk] = v_ref[blk, :].astype(F32).T.astype(BF16)
    row = lax.broadcasted_iota(jnp.int32, (ONES_ROWS, TQ), 0)
    vt_ref[dv:dv + ONES_ROWS, blk] = jnp.where(row == 0, 1.0, 0.0).astype(BF16)


def _attention(n_tiles, q_tile_of, k_block_of, terms_of, prepare, vt_ref, bufs, emit):
    nt = (((1,), (1,)), ((), ()))
    dv = vt_ref.shape[0] - ONES_ROWS
    t_bufs, p_buf = bufs[:2], bufs[2]
    state = {}

    def stage1(i):
        prepare(i)
        q = q_tile_of(i)
        adds, consts, sels = terms_of(i, q)
        st = state[i] = dict(consts=consts, sels=sels, m=None)
        t_buf = t_bufs[i % 2]

        def item(j):
            blk = slice(j * TQ, (j + 1) * TQ)
            t = lax.dot_general(k_block_of(j), q, nt, preferred_element_type=F32)
            if adds[j] is not None:
                t = t + adds[j][...]
            t_buf[blk, :] = t
            mj = jnp.max(t, axis=0, keepdims=True) + consts[j]
            if sels[j] is not None:
                mj = jnp.where(sels[j], mj, -jnp.inf)
            st["m"] = mj if st["m"] is None else jnp.maximum(st["m"], mj)
        return [functools.partial(item, j) for j in range(len(consts))]

    def stage2(i):
        st = state.pop(i)
        t_buf = t_bufs[i % 2]

        def item(j):
            blk = slice(j * TQ, (j + 1) * TQ)
            off = st["m"] - st["consts"][j]
            if st["sels"][j] is not None:
                off = jnp.where(st["sels"][j], off, jnp.inf)
            p_buf[blk, :] = jnp.exp2(t_buf[blk, :] - off).astype(BF16)

        def finish():
            n_keys = len(st["consts"]) * TQ
            acc = jnp.dot(vt_ref[:, 0:n_keys], p_buf[0:n_keys, :], preferred_element_type=F32)
            emit(i, (acc[0:dv, :] / acc[dv:dv + 1, :]).T)
        return [functools.partial(item, j) for j in range(len(st["consts"]))], finish

    for item in stage1(0):
        item()
        yield
    for i in range(n_tiles):
        ahead = stage1(i + 1) if i + 1 < n_tiles else []
        behind, finish = stage2(i)
        for k in range(max(len(ahead), len(behind))):
            if k < len(ahead):
                ahead[k]()
            if k < len(behind):
                behind[k]()
            yield
        finish()


def _run_streams(streams):
    active = list(streams)
    while active:
        for stream in list(active):
            if next(stream, StopIteration) is StopIteration:
                active.remove(stream)


def _cast_plan(weights, layer, n_b, n_h):
    n_steps = n_b * n_h
    ins, outs, shapes, nbytes = [], [], [], 0
    for w in weights:
        _, rows, cols = w.shape
        assert rows % (n_steps * BF16_TILE_ROWS) == 0
        blk_rows = rows // n_steps
        ins.append(pl.BlockSpec((pl.Squeezed(), blk_rows, cols), lambda bi, h: (layer, bi * n_h + h, 0)))
        outs.append(pl.BlockSpec((blk_rows, cols), lambda bi, h: (bi * n_h + h, 0)))
        shapes.append(jax.ShapeDtypeStruct((rows, cols), BF16))
        nbytes += 2 * blk_rows * cols * (4 + 2)
    return ins, outs, shapes, nbytes


def _cast_row_blocks(rest, n_cast):
    cast_in, (o_ref, *cast_out), scratch = rest[:n_cast], rest[n_cast:2 * n_cast + 1], rest[2 * n_cast + 1:]
    for w_ref, wb_ref in zip(cast_in, cast_out):
        wb_ref[...] = w_ref[...].astype(BF16)
    return o_ref, scratch


def _silu_gate(o, z):
    zf = z.astype(F32)
    return (o * (zf / (1.0 + jnp.exp(-zf)))).astype(BF16)


def _write_causal_tile(mask_ref):
    key = lax.broadcasted_iota(jnp.int32, (TQ, TQ), 0)
    qry = lax.broadcasted_iota(jnp.int32, (TQ, TQ), 1)
    mask_ref[...] = jnp.where(key <= qry, 0.0, -jnp.inf).astype(F32)


def _head_view(ref, head, width):
    return ref.at[0, :, head * width:(head + 1) * width]


def _moba_kernel(rb_ref, q_ref, k_ref, v_ref, z_ref, own_ref, prev_ref, *rest, heads, n_cast):
    o_ref, scratch = _cast_row_blocks(rest, n_cast)
    per_head = len(scratch) // heads
    _run_streams([
        _moba_head(rb_ref, pl.program_id(1) * heads + s, _head_view(q_ref, s, HEAD_DIM),
                   _head_view(k_ref, s, HEAD_DIM), _head_view(v_ref, s, HEAD_DIM),
                   _head_view(z_ref, s, HEAD_DIM), own_ref.at[s], prev_ref.at[s],
                   _head_view(o_ref, s, HEAD_DIM), *scratch[s * per_head:(s + 1) * per_head])
        for s in range(heads)])


def _moba_head(rb_ref, head, q_ref, k_ref, v_ref, z_ref, own_ref, prev_ref, o_ref, vt_ref, t0, t1, pb, km_ref):
    seq = q_ref.shape[0]
    n_tiles = seq // TQ
    far_const = rb_ref[N_BUCKETS - 1, head] * LOG2E
    km_ref[...] = jnp.zeros(km_ref.shape, F32)

    def prepare(i):
        _build_vt_block(v_ref, vt_ref, i)
        k_blk = k_ref[i * TQ:(i + 1) * TQ, :].astype(F32)
        km_ref[i:i + 1, :] = jnp.sum(k_blk, axis=0, keepdims=True) * (1.0 / MOBA_BLOCK)

    nt = (((1,), (1,)), ((), ()))

    def terms_of(i, q_tile):
        if i > MOBA_TOPK:
            k_mean = km_ref[...]
            km1 = k_mean.astype(BF16)
            rem = k_mean - km1.astype(F32)
            km2 = rem.astype(BF16)
            km3 = (rem - km2.astype(F32)).astype(BF16)
            gate = (lax.dot_general(km1, q_tile, nt, preferred_element_type=F32)
                    + lax.dot_general(km2, q_tile, nt, preferred_element_type=F32)
                    + lax.dot_general(km3, q_tile, nt, preferred_element_type=F32))
            sels = []
            for j in range(i):
                gj = gate[j:j + 1, :]
                cnt = jnp.zeros(gj.shape, F32)
                for jp in range(i):
                    if jp == j:
                        continue
                    gp = gate[jp:jp + 1, :]
                    beats = (gp >= gj) if jp < j else (gp > gj)
                    cnt = cnt + jnp.where(beats, 1.0, 0.0)
                sels.append(cnt < MOBA_TOPK)
        else:
            sels = [None] * i
        sels.append(None)
        adds = [None] * (i + 1)
        consts = [far_const] * (i + 1)
        adds[i], consts[i] = own_ref, 0.0
        if i >= 1:
            adds[i - 1], consts[i - 1] = prev_ref, 0.0
        return adds, consts, sels

    def emit(i, o):
        rows = slice(i * TQ, (i + 1) * TQ)
        o_ref[rows, :] = _silu_gate(o, z_ref[rows, :])

    return _attention(n_tiles, lambda i: q_ref[i * TQ:(i + 1) * TQ, :],
                      lambda j: k_ref[j * TQ:(j + 1) * TQ, :], terms_of, prepare, vt_ref, (t0, t1, pb), emit)


HEADS_PER_STEP = 4


MOBA_HEADS_PER_STEP = 2


def _attn_scratch(n_keys, extra=(), heads=HEADS_PER_STEP):
    per_head = [pltpu.VMEM((VT_ROWS, n_keys), BF16), pltpu.VMEM((n_keys, TQ), F32),
                pltpu.VMEM((n_keys, TQ), F32), pltpu.VMEM((n_keys, TQ), BF16), *extra]
    return per_head * heads


def _moba_attn(u3, rel_bias, own, prev, cast_weights, layer):
    b, s, _ = u3.shape
    assert s // MOBA_BLOCK <= BF16_TILE_ROWS
    hp = MOBA_HEADS_PER_STEP
    width = hp * HEAD_DIM
    n_h = MOBA_HEADS // hp
    col = lambda off: (lambda bi, h: (bi, 0, off // width + h))
    blk = (1, s, width)
    tile_spec = pl.BlockSpec((hp, TQ, TQ), lambda bi, h: (h, 0, 0))
    cast_in, cast_out, cast_shapes, cast_bytes = _cast_plan(cast_weights, layer, b, n_h)
    est = (10 * s * width * 2 + 4 * hp * TQ * TQ * 4 + hp * (VT_ROWS * s * 2 + 2 * s * TQ * 4 + s * TQ * 2)
           + cast_bytes)
    outs = pl.pallas_call(
        functools.partial(_moba_kernel, heads=hp, n_cast=len(cast_weights)),
        grid=(b, n_h),
        in_specs=[
            pl.BlockSpec(memory_space=pltpu.SMEM),
            pl.BlockSpec(blk, col(OFF_QA)),
            pl.BlockSpec(blk, col(OFF_KA)),
            pl.BlockSpec(blk, col(OFF_VA)),
            pl.BlockSpec(blk, col(OFF_ZA)),
            tile_spec, tile_spec,
            *cast_in,
        ],
        out_specs=[pl.BlockSpec(blk, lambda bi, h: (bi, 0, h)), *cast_out],
        out_shape=[jax.ShapeDtypeStruct((b, s, MOBA_W), BF16), *cast_shapes],
        scratch_shapes=_attn_scratch(s, extra=(pltpu.VMEM((BF16_TILE_ROWS, HEAD_DIM), F32),), heads=hp),
        compiler_params=pltpu.CompilerParams(
            dimension_semantics=("arbitrary", "arbitrary"), vmem_limit_bytes=_vmem_limit(est)),
        name="moba_attn",
    )(rel_bias, u3, u3, u3, u3, own, prev, *cast_weights)
    return outs[0], outs[1:]


def _mla_prep_kernel(cq_ref, ckv_ref, kr_ref, gq_ref, gkv_ref, wuq_ref, wukv_ref, cos_ref, sin_ref,
                     q_out, k_out, v_out, *, q_scale):
    def rms(x_ref, g_ref):
        xf = x_ref[...].astype(F32)
        ms = jnp.mean(xf * xf, axis=-1, keepdims=True)
        return (xf * lax.rsqrt(ms + EPS) * g_ref[...]).astype(BF16)

    cos = cos_ref[...]
    sin = sin_ref[...]
    half = MLA_ROPE // 2
    first_half = lax.broadcasted_iota(jnp.int32, cos.shape, 1) < half

    def rope(xr):
        partner = jnp.where(first_half, pltpu.roll(xr, LANES - half, 1), pltpu.roll(xr, half, 1))
        return xr * cos + partner * sin

    qb = jnp.dot(rms(cq_ref, gq_ref), wuq_ref[...], preferred_element_type=F32) * q_scale
    for h in range(MLA_HEADS):
        base = h * MLA_QK
        q_out[:, base:base + MLA_NOPE] = qb[:, base:base + MLA_NOPE].astype(BF16)
        q_out[:, base + MLA_NOPE:base + MLA_QK] = rope(qb[:, base + MLA_NOPE:base + MLA_QK]).astype(BF16)

    kvb = jnp.dot(rms(ckv_ref, gkv_ref), wukv_ref[...], preferred_element_type=F32)
    in_rope = lax.broadcasted_iota(jnp.int32, cos.shape, 1) < MLA_ROPE
    k_rope = rope(jnp.where(in_rope, kr_ref[:, 0:LANES].astype(F32), 0.0)).astype(BF16)
    for h in range(MLA_HEADS):
        base = h * MLA_QK
        k_out[:, base:base + MLA_NOPE] = kvb[:, h * MLA_NOPE:(h + 1) * MLA_NOPE].astype(BF16)
        k_out[:, base + MLA_NOPE:base + MLA_QK] = k_rope
    v_out[...] = kvb[:, MLA_W:2 * MLA_W].astype(BF16)


def _mla_prep(u, g_cq, g_ckv, wuq_p, wukv_p, cos_t, sin_t, *, seq, tm):
    t = u.shape[0]
    s_tiles = seq // tm
    const = lambda i: (0, 0)
    qk_shape = jax.ShapeDtypeStruct((t, MLA_HEADS * MLA_QK), BF16)
    est = (2 * tm * (MLA_Q_LORA + 2 * MLA_KV_LORA) * 2 + 2 * (wuq_p.size + wukv_p.size) * 2
           + 4 * tm * LANES * 4 + 2 * tm * 5 * MLA_W * 2 + 4 * tm * 2 * MLA_W * 4)
    return pl.pallas_call(
        functools.partial(_mla_prep_kernel, q_scale=(MLA_NOPE + MLA_ROPE) ** -0.5 * LOG2E),
        grid=(t // tm,),
        in_specs=[
            pl.BlockSpec((tm, MLA_Q_LORA), lambda i: (i, OFF_CQ // MLA_Q_LORA)),
            pl.BlockSpec((tm, MLA_KV_LORA), lambda i: (i, OFF_CKV // MLA_KV_LORA)),
            pl.BlockSpec((tm, KR_PAD), lambda i: (i, OFF_KR // KR_PAD)),
            pl.BlockSpec((1, MLA_Q_LORA), const),
            pl.BlockSpec((1, MLA_KV_LORA), const),
            pl.BlockSpec(wuq_p.shape, const),
            pl.BlockSpec(wukv_p.shape, const),
            pl.BlockSpec((tm, LANES), lambda i: (i % s_tiles, 0)),
            pl.BlockSpec((tm, LANES), lambda i: (i % s_tiles, 0)),
        ],
        out_specs=[
            pl.BlockSpec((tm, MLA_HEADS * MLA_QK), lambda i: (i, 0)),
            pl.BlockSpec((tm, MLA_HEADS * MLA_QK), lambda i: (i, 0)),
            pl.BlockSpec((tm, MLA_W), lambda i: (i, 0)),
        ],
        out_shape=[qk_shape, qk_shape, jax.ShapeDtypeStruct((t, MLA_W), BF16)],
        compiler_params=pltpu.CompilerParams(
            dimension_semantics=("arbitrary",), vmem_limit_bytes=_vmem_limit(est)),
        name="mla_prep",
    )(u, u, u, g_cq.reshape(1, -1), g_ckv.reshape(1, -1), wuq_p, wukv_p, cos_t, sin_t)


def _mla_attn_kernel(q_ref, k_ref, v_ref, z_ref, o_ref, mask_ref, *scratch, heads):
    _write_causal_tile(mask_ref)
    per_head = len(scratch) // heads
    _run_streams([
        _mla_head(_head_view(q_ref, s, MLA_QK), _head_view(k_ref, s, MLA_QK), _head_view(v_ref, s, MLA_V),
                  _head_view(z_ref, s, MLA_V), _head_view(o_ref, s, MLA_V), mask_ref,
                  *scratch[s * per_head:(s + 1) * per_head])
        for s in range(heads)])


def _mla_head(q_ref, k_ref, v_ref, z_ref, o_ref, mask_ref, vt_ref, t0, t1, pb):
    seq = q_ref.shape[0]

    def terms_of(i, q_tile):
        return [None] * i + [mask_ref], [0.0] * (i + 1), [None] * (i + 1)

    def emit(i, o):
        rows = slice(i * TQ, (i + 1) * TQ)
        o_ref[rows, :] = _silu_gate(o, z_ref[rows, :])

    return _attention(seq // TQ, lambda i: q_ref[i * TQ:(i + 1) * TQ, :],
                      lambda j: k_ref[j * TQ:(j + 1) * TQ, :], terms_of,
                      functools.partial(_build_vt_block, v_ref, vt_ref), vt_ref, (t0, t1, pb), emit)


def _mla_attn(q3, k3, v3, u3):
    b, s, _ = q3.shape
    hp = HEADS_PER_STEP
    est = (4 * s * hp * MLA_QK * 2 + 6 * s * hp * MLA_V * 2 + hp * (VT_ROWS * s * 2 + 2 * s * TQ * 4 + s * TQ * 2)
           + TQ * TQ * 4)
    return pl.pallas_call(
        functools.partial(_mla_attn_kernel, heads=hp),
        grid=(b, MLA_HEADS // hp),
        in_specs=[
            pl.BlockSpec((1, s, hp * MLA_QK), lambda bi, h: (bi, 0, h)),
            pl.BlockSpec((1, s, hp * MLA_QK), lambda bi, h: (bi, 0, h)),
            pl.BlockSpec((1, s, hp * MLA_V), lambda bi, h: (bi, 0, h)),
            pl.BlockSpec((1, s, hp * MLA_V), lambda bi, h: (bi, 0, OFF_ZB // (hp * MLA_V) + h)),
        ],
        out_specs=pl.BlockSpec((1, s, hp * MLA_V), lambda bi, h: (bi, 0, h)),
        out_shape=jax.ShapeDtypeStruct((b, s, MLA_W), BF16),
        scratch_shapes=[pltpu.VMEM((TQ, TQ), F32)] + _attn_scratch(s),
        compiler_params=pltpu.CompilerParams(
            dimension_semantics=("arbitrary", "arbitrary"), vmem_limit_bytes=_vmem_limit(est)),
        name="mla_attn",
    )(q3, k3, v3, u3)


def _mem_attn_kernel(q_ref, k_ref, v_ref, z_ref, *rest, heads, n_cast):
    o_ref, scratch = _cast_row_blocks(rest, n_cast)
    assert k_ref.shape[1] == TQ
    per_head = len(scratch) // heads
    d = MEM_HEAD_DIM
    _run_streams([
        _mem_head(_head_view(q_ref, s, d), _head_view(k_ref, s, d), _head_view(v_ref, s, d),
                  _head_view(z_ref, s, d), _head_view(o_ref, s, d), *scratch[s * per_head:(s + 1) * per_head])
        for s in range(heads)])


def _mem_head(q_ref, k_ref, v_ref, z_ref, o_ref, vt_ref, t0, t1, pb):
    seq = q_ref.shape[0]

    def prepare(i):
        if i == 0:
            _build_vt_block(v_ref, vt_ref, 0)

    def emit(i, o):
        rows = slice(i * TQ, (i + 1) * TQ)
        o_ref[rows, :] = _silu_gate(o, z_ref[rows, :])

    return _attention(seq // TQ, lambda i: q_ref[i * TQ:(i + 1) * TQ, :], lambda j: k_ref[...],
                      lambda i, q_tile: ([None], [0.0], [None]), prepare, vt_ref, (t0, t1, pb), emit)


def _mem_attn(u3, kv3, cast_weights, layer):
    b, s, _ = u3.shape
    m = kv3.shape[1]
    hp = HEADS_PER_STEP
    d = hp * MEM_HEAD_DIM
    n_h = MEM_HEADS // hp
    cast_in, cast_out, cast_shapes, cast_bytes = _cast_plan(cast_weights, layer, b, n_h)
    est = 6 * s * d * 2 + 4 * m * d * 2 + hp * (VT_ROWS * m * 2 + 2 * m * TQ * 4 + m * TQ * 2) + cast_bytes
    outs = pl.pallas_call(
        functools.partial(_mem_attn_kernel, heads=hp, n_cast=len(cast_weights)),
        grid=(b, n_h),
        in_specs=[
            pl.BlockSpec((1, s, d), lambda bi, h: (bi, 0, OFF_QM // d + h)),
            pl.BlockSpec((1, m, d), lambda bi, h: (bi, 0, h)),
            pl.BlockSpec((1, m, d), lambda bi, h: (bi, 0, MEM_W // d + h)),
            pl.BlockSpec((1, s, d), lambda bi, h: (bi, 0, OFF_ZM // d + h)),
            *cast_in,
        ],
        out_specs=[pl.BlockSpec((1, s, d), lambda bi, h: (bi, 0, h)), *cast_out],
        out_shape=[jax.ShapeDtypeStruct((b, s, MEM_W), BF16), *cast_shapes],
        scratch_shapes=_attn_scratch(m),
        compiler_params=pltpu.CompilerParams(
            dimension_semantics=("arbitrary", "arbitrary"), vmem_limit_bytes=_vmem_limit(est)),
        name="mem_attn",
    )(u3, kv3, kv3, u3, *cast_weights)
    return outs[0], outs[1:]


def _merge_kernel(ga_ref, gb_ref, gm_ref, gl_ref, x_ref, wpa_ref, wpb_ref, wpm_ref, wout_ref, gf_ref,
                  o_ref, *, final_norm):
    d = x_ref.shape[1]

    def gated(idx, g_ref, w_ref):
        logit = gl_ref[:, idx * d:(idx + 1) * d].astype(F32)
        return jnp.dot(g_ref[...], w_ref[...], preferred_element_type=F32) / (1.0 + jnp.exp(-logit))

    y = gated(0, ga_ref, wpa_ref) + gated(1, gb_ref, wpb_ref) + gated(2, gm_ref, wpm_ref)
    r = x_ref[...] + jnp.dot(y.astype(BF16), wout_ref[...], preferred_element_type=F32)
    if final_norm:
        ms = jnp.mean(r * r, axis=-1, keepdims=True)
        r = r * lax.rsqrt(ms + EPS) * gf_ref[...]
    o_ref[...] = r


def _merge(ga, gb, gm, u, x, wpa, wpb, wpm, wout, g_final, *, tm, final_norm):
    t, d = x.shape
    const = lambda i: (0, 0)
    resident = lambda w: pl.BlockSpec(w.shape, const, pipeline_mode=pl.Buffered(1))
    rows = lambda width: pl.BlockSpec((tm, width), lambda i: (i, 0))
    w_bytes = (wpa.size + wpb.size + wpm.size + wout.size) * 2
    est = w_bytes + 2 * tm * (2 * MOBA_W + MEM_W + 3 * d) * 2 + 4 * tm * d * 4 + 6 * tm * d * 4
    return pl.pallas_call(
        functools.partial(_merge_kernel, final_norm=final_norm),
        grid=(t // tm,),
        in_specs=[
            rows(MOBA_W), rows(MLA_W), rows(MEM_W),
            pl.BlockSpec((tm, 3 * d), lambda i: (i, OFF_GL // (3 * d))),
            rows(d),
            resident(wpa), resident(wpb), resident(wpm), resident(wout),
            pl.BlockSpec((1, d), const),
        ],
        out_specs=rows(d),
        out_shape=jax.ShapeDtypeStruct((t, d), F32),
        compiler_params=pltpu.CompilerParams(
            dimension_semantics=("arbitrary",), vmem_limit_bytes=_vmem_limit(est)),
        name="merge",
    )(ga, gb, gm, u, x, wpa, wpb, wpm, wout, g_final.reshape(1, d))


def _w_in_tile_sources(n_cols):
    o_cq = 4 * MOBA_W
    o_zb = o_cq + MLA_Q_LORA + MLA_KV_LORA + MLA_ROPE
    o_qm = o_zb + MLA_W
    o_gl = o_qm + 2 * MEM_W
    srcs = ([o_gl + W_TILE * k for k in range(3 * D_MODEL // W_TILE)]
            + [W_TILE * k for k in range(4 * MOBA_W // W_TILE)] + [o_zb, o_cq, o_qm])
    assert len(srcs) * W_TILE == IN_WIDTH_P and o_gl + 3 * D_MODEL == n_cols
    assert all(src + W_TILE <= n_cols for src in srcs)
    return srcs


def _in_col_scale():
    cs = jnp.ones((1, IN_WIDTH_P), F32)
    cs = cs.at[:, OFF_QA:OFF_QA + MOBA_W].set(HEAD_DIM ** -0.5 * LOG2E)
    return cs.at[:, OFF_QM:OFF_QM + MEM_W].set(MEM_HEAD_DIM ** -0.5 * LOG2E)


def _regroup_w_uq(w):
    r = w.shape[0]
    w3 = w.reshape(r, MLA_HEADS, MLA_NOPE + MLA_ROPE)
    pad = jnp.zeros((r, MLA_HEADS, MLA_QK - MLA_NOPE - MLA_ROPE), w.dtype)
    return jnp.concatenate([w3, pad], axis=-1).reshape(r, MLA_HEADS * MLA_QK).astype(BF16)


def _regroup_w_ukv(w):
    r = w.shape[0]
    w3 = w.reshape(r, MLA_HEADS, MLA_NOPE + MLA_V)
    return jnp.concatenate([w3[:, :, :MLA_NOPE].reshape(r, MLA_W),
                            w3[:, :, MLA_NOPE:].reshape(r, MLA_W)], axis=1).astype(BF16)


def _rope_tables(seq):
    half = MLA_ROPE // 2
    inv = ROPE_THETA ** (-jnp.arange(half, dtype=F32) / half)
    ang = jnp.arange(seq, dtype=jnp.int32).astype(F32)[:, None] * inv[None, :]
    cos, sin = jnp.cos(ang), jnp.sin(ang)
    pad = LANES - MLA_ROPE
    cos_t = jnp.concatenate([cos, cos, jnp.ones((seq, pad), F32)], axis=1)
    sin_t = jnp.concatenate([-sin, sin, jnp.zeros((seq, pad), F32)], axis=1)
    return cos_t, sin_t


def kernel(x, mem, g_norm, w_in, g_cq, w_uq, g_ckv, w_ukv, g_mem, w_mem_kv, rel_bias,
           w_p_moba, w_p_mla, w_p_mem, w_out, g_final):
    b, s, d = x.shape
    m = mem.shape[1]
    depth = w_in.shape[0]
    t = b * s
    assert d == D_MODEL and s % TQ == 0 and m == TQ

    own, prev = _bias_tiles(rel_bias)
    cos_t, sin_t = _rope_tables(s)
    mem2 = mem.reshape(b * m, d)
    xs = x.reshape(t, d)
    in_scale = _in_col_scale()
    w_in_t = jnp.swapaxes(w_in, 1, 2)
    for l in range(depth):
        u = _in_proj(xs, g_norm[l], w_in_t, l, in_scale, tm=1024)
        u3 = u.reshape(b, s, IN_WIDTH_P)
        ga, (wout,) = _moba_attn(u3, rel_bias, own, prev, (w_out,), l)
        q2, k2, v2 = _mla_prep(u, g_cq[l], g_ckv[l], _regroup_w_uq(w_uq[l]), _regroup_w_ukv(w_ukv[l]),
                               cos_t, sin_t, seq=s, tm=512)
        gb = _mla_attn(q2.reshape(b, s, -1), k2.reshape(b, s, -1), v2.reshape(b, s, -1), u3)
        kvm = _norm_matmul(mem2, g_mem[l], w_mem_kv, l, tm=b * m, tn=MEM_W)
        gm, (wpa, wpb, wpm) = _mem_attn(u3, kvm.reshape(b, m, 2 * MEM_W), (w_p_moba, w_p_mla, w_p_mem), l)
        xs = _merge(ga.reshape(t, MOBA_W), gb.reshape(t, MLA_W), gm.reshape(t, MEM_W), u, xs,
                    wpa, wpb, wpm, wout, g_final, tm=256, final_norm=(l == depth - 1))
    return xs.reshape(b, s, d)
```

```python
import functools
import math

import jax
import jax.numpy as jnp
from jax import lax
from jax.experimental import pallas as pl
from jax.experimental.pallas import tpu as pltpu

D_MODEL = 2048
MOBA_HEADS = 8
HEAD_DIM = 128
MOBA_BLOCK = 256
MOBA_TOPK = 3
MLA_HEADS = 8
MLA_Q_LORA = 512
MLA_KV_LORA = 256
MLA_NOPE = 128
MLA_ROPE = 64
MLA_V = 128
ROPE_THETA = 10000.0
MEM_HEADS = 4
MEM_HEAD_DIM = 128
N_BUCKETS = 32
MAX_DISTANCE = 128
EPS = 1e-6

MOBA_W = MOBA_HEADS * HEAD_DIM
MLA_W = MLA_HEADS * MLA_V
MEM_W = MEM_HEADS * MEM_HEAD_DIM

LANES = 128
MXU_DIM = 256
V7X_VMEM_BYTES = 64 * 1024 * 1024
VMEM_HEADROOM_BYTES = 4 * 1024 * 1024
KERNEL_TEMP_BYTES = 8 * 1024 * 1024

BF16 = jnp.bfloat16
F32 = jnp.float32
LOG2E = 1.4426950408889634

KR_PAD = MXU_DIM
OFF_GL = 0
OFF_QA = OFF_GL + 3 * D_MODEL
OFF_KA = OFF_QA + MOBA_W
OFF_VA = OFF_KA + MOBA_W
OFF_ZA = OFF_VA + MOBA_W
OFF_ZB = OFF_ZA + MOBA_W
OFF_CQ = OFF_ZB + MLA_W
OFF_CKV = OFF_CQ + MLA_Q_LORA
OFF_KR = OFF_CKV + MLA_KV_LORA
OFF_QM = OFF_KR + KR_PAD
OFF_ZM = OFF_QM + MEM_W
IN_WIDTH_P = OFF_ZM + MEM_W

TQ = MOBA_BLOCK
BF16_TILE_ROWS = 16
ONES_ROWS = BF16_TILE_ROWS
VT_ROWS = MLA_V + ONES_ROWS
MLA_QK = MXU_DIM

W_TILE = 1024
W_SRC_UNIT = 64
NORM_CHUNK_ROWS = 64


def _vmem_limit(block_bytes):
    return int(min(block_bytes + KERNEL_TEMP_BYTES, V7X_VMEM_BYTES - VMEM_HEADROOM_BYTES))


def _t5_thresholds():
    max_exact = N_BUCKETS // 2

    def bucket(d):
        if d < max_exact:
            return d
        large = max_exact + int(math.log(d / max_exact) / math.log(MAX_DISTANCE / max_exact)
                                * (N_BUCKETS - max_exact))
        return min(large, N_BUCKETS - 1)

    thr, d = [], 0
    for b in range(1, N_BUCKETS):
        while bucket(d) < b:
            d += 1
        thr.append(d)
    return tuple(thr)


T5_THRESHOLDS = _t5_thresholds()
assert T5_THRESHOLDS[-1] <= MOBA_BLOCK + 1


def _norm_rows(x_ref, g_ref, h_ref, chunk):
    def body(r, carry):
        rows = pl.ds(pl.multiple_of(r * chunk, chunk), chunk)
        xv = x_ref[rows, :]
        ms = jnp.mean(xv * xv, axis=-1, keepdims=True)
        h_ref[rows, :] = (xv * lax.rsqrt(ms + EPS) * g_ref[...]).astype(BF16)
        return carry
    lax.fori_loop(0, x_ref.shape[0] // chunk, body, 0, unroll=4)


def _norm_matmul_kernel(x_ref, g_ref, w_ref, o_ref, h_ref, *, chunk):
    pl.when(pl.program_id(1) == 0)(functools.partial(_norm_rows, x_ref, g_ref, h_ref, chunk))
    o_ref[...] = jnp.dot(h_ref[...], w_ref[...].astype(BF16), preferred_element_type=F32).astype(o_ref.dtype)


def _in_proj_kernel(src_ref, x_ref, g_ref, wt_ref, cs_ref, o_ref, h_ref, *, chunk):
    del src_ref
    pl.when(pl.program_id(1) == 0)(functools.partial(_norm_rows, x_ref, g_ref, h_ref, chunk))
    acc = lax.dot_general(h_ref[...], wt_ref[...].astype(BF16), (((1,), (1,)), ((), ())),
                          preferred_element_type=F32)
    o_ref[...] = (acc * cs_ref[...]).astype(o_ref.dtype)


def _in_proj(x, g, wt_all, layer, col_scale, *, tm):
    t, d = x.shape
    srcs = _w_in_tile_sources(wt_all.shape[1])
    unit = W_SRC_UNIT
    assert all(src % unit == 0 for src in srcs)
    src_units = jnp.asarray([src // unit for src in srcs], jnp.int32)
    est = 2 * tm * d * 4 + tm * d * 2 + 2 * W_TILE * d * 4 + W_TILE * d * 2 + 2 * tm * W_TILE * 2 + tm * W_TILE * 4
    return pl.pallas_call(
        functools.partial(_in_proj_kernel, chunk=NORM_CHUNK_ROWS),
        grid_spec=pltpu.PrefetchScalarGridSpec(
            num_scalar_prefetch=1,
            grid=(t // tm, len(srcs)),
            in_specs=[
                pl.BlockSpec((tm, d), lambda i, j, src: (i, 0)),
                pl.BlockSpec((1, d), lambda i, j, src: (0, 0)),
                pl.BlockSpec((pl.Squeezed(), pl.Element(W_TILE), pl.Element(d)),
                             lambda i, j, src: (layer, src[j] * unit, 0)),
                pl.BlockSpec((1, W_TILE), lambda i, j, src: (0, j)),
            ],
            out_specs=pl.BlockSpec((tm, W_TILE), lambda i, j, src: (i, j)),
            scratch_shapes=[pltpu.VMEM((tm, d), BF16)],
        ),
        out_shape=jax.ShapeDtypeStruct((t, IN_WIDTH_P), BF16),
        compiler_params=pltpu.CompilerParams(
            dimension_semantics=("arbitrary", "arbitrary"), vmem_limit_bytes=_vmem_limit(est)),
        name="in_proj",
    )(src_units, x, g.reshape(1, d), wt_all, col_scale)


def _norm_matmul(x, g, w_all, layer, *, tm, tn):
    t, d = x.shape
    n = w_all.shape[2]
    est = 2 * tm * d * 4 + tm * d * 2 + 2 * d * tn * 4 + d * tn * 2 + 2 * tm * tn * 2 + tm * tn * 4
    return pl.pallas_call(
        functools.partial(_norm_matmul_kernel, chunk=NORM_CHUNK_ROWS),
        grid=(t // tm, n // tn),
        in_specs=[
            pl.BlockSpec((tm, d), lambda i, j: (i, 0)),
            pl.BlockSpec((1, d), lambda i, j: (0, 0)),
            pl.BlockSpec((pl.Squeezed(), d, tn), lambda i, j: (layer, 0, j)),
        ],
        out_specs=pl.BlockSpec((tm, tn), lambda i, j: (i, j)),
        out_shape=jax.ShapeDtypeStruct((t, n), BF16),
        scratch_shapes=[pltpu.VMEM((tm, d), BF16)],
        compiler_params=pltpu.CompilerParams(
            dimension_semantics=("arbitrary", "arbitrary"), vmem_limit_bytes=_vmem_limit(est)),
        name="norm_matmul",
    )(x, g.reshape(1, d), w_all)


def _bias_tiles_kernel(rb_ref, own_ref, prev_ref):
    key = lax.broadcasted_iota(jnp.int32, (TQ, TQ), 0)
    qry = lax.broadcasted_iota(jnp.int32, (TQ, TQ), 1)
    d_own = qry - key
    d_prev = d_own + MOBA_BLOCK

    def lookup(dist, h):
        val = jnp.zeros(dist.shape, F32) + rb_ref[0, h]
        for b in range(1, N_BUCKETS):
            val = jnp.where(dist >= T5_THRESHOLDS[b - 1], rb_ref[b, h], val)
        return val * LOG2E

    for h in range(own_ref.shape[0]):
        own_ref[h] = jnp.where(d_own >= 0, lookup(d_own, h), -jnp.inf)
        prev_ref[h] = lookup(d_prev, h)


def _bias_tiles(rel_bias):
    heads = rel_bias.shape[1]
    tile = jax.ShapeDtypeStruct((heads, TQ, TQ), F32)
    spec = pl.BlockSpec((heads, TQ, TQ), lambda i: (0, 0, 0))
    return pl.pallas_call(
        _bias_tiles_kernel,
        grid=(1,),
        in_specs=[pl.BlockSpec(memory_space=pltpu.SMEM)],
        out_specs=[spec, spec],
        out_shape=[tile, tile],
        name="bias_tiles",
    )(rel_bias)


def _build_vt_block(v_ref, vt_ref, j):
    dv = v_ref.shape[-1]
    blk = slice(j * TQ, (j + 1) * TQ)
    vt_ref[0:dv, blk] = v_ref[blk, :].astype(F32).T.astype(BF16)
    row = lax.broadcasted_iota(jnp.int32, (ONES_ROWS, TQ), 0)
    vt_ref[dv:dv + ONES_ROWS, blk] = jnp.where(row == 0, 1.0, 0.0).astype(BF16)


def _attention(n_tiles, q_tile_of, k_block_of, terms_of, prepare, vt_ref, bufs, emit):
    nt = (((1,), (1,)), ((), ()))
    dv = vt_ref.shape[0] - ONES_ROWS
    t_bufs, p_buf = bufs[:2], bufs[2]
    state = {}

    def stage1(i):
        prepare(i)
        q = q_tile_of(i)
        adds, consts, sels = terms_of(i, q)
        st = state[i] = dict(consts=consts, sels=sels, m=None)
        t_buf = t_bufs[i % 2]

        def item(j):
            blk = slice(j * TQ, (j + 1) * TQ)
            t = lax.dot_general(k_block_of(j), q, nt, preferred_element_type=F32)
            if adds[j] is not None:
                t = t + adds[j][...]
            t_buf[blk, :] = t
            mj = jnp.max(t, axis=0, keepdims=True) + consts[j]
            if sels[j] is not None:
                mj = jnp.where(sels[j], mj, -jnp.inf)
            st["m"] = mj if st["m"] is None else jnp.maximum(st["m"], mj)
        return [functools.partial(item, j) for j in range(len(consts))]

    def stage2(i):
        st = state.pop(i)
        t_buf = t_bufs[i % 2]

        def item(j):
            blk = slice(j * TQ, (j + 1) * TQ)
            off = st["m"] - st["consts"][j]
            if st["sels"][j] is not None:
                off = jnp.where(st["sels"][j], off, jnp.inf)
            p_buf[blk, :] = jnp.exp2(t_buf[blk, :] - off).astype(BF16)

        def finish():
            n_keys = len(st["consts"]) * TQ
            acc = jnp.dot(vt_ref[:, 0:n_keys], p_buf[0:n_keys, :], preferred_element_type=F32)
            emit(i, (acc[0:dv, :] / acc[dv:dv + 1, :]).T)
        return [functools.partial(item, j) for j in range(len(st["consts"]))], finish

    for item in stage1(0):
        item()
        yield
    for i in range(n_tiles):
        ahead = stage1(i + 1) if i + 1 < n_tiles else []
        behind, finish = stage2(i)
        for k in range(max(len(ahead), len(behind))):
            if k < len(ahead):
                ahead[k]()
            if k < len(behind):
                behind[k]()
            yield
        finish()


def _run_streams(streams):
    active = list(streams)
    while active:
        for stream in list(active):
            if next(stream, StopIteration) is StopIteration:
                active.remove(stream)


def _cast_plan(weights, layer, n_b, n_h):
    n_steps = n_b * n_h
    ins, outs, shapes, nbytes = [], [], [], 0
    for w in weights:
        _, rows, cols = w.shape
        assert rows % (n_steps * BF16_TILE_ROWS) == 0
        blk_rows = rows // n_steps
        ins.append(pl.BlockSpec((pl.Squeezed(), blk_rows, cols), lambda bi, h: (layer, bi * n_h + h, 0)))
        outs.append(pl.BlockSpec((blk_rows, cols), lambda bi, h: (bi * n_h + h, 0)))
        shapes.append(jax.ShapeDtypeStruct((rows, cols), BF16))
        nbytes += 2 * blk_rows * cols * (4 + 2)
    return ins, outs, shapes, nbytes


def _cast_row_blocks(rest, n_cast):
    cast_in, (o_ref, *cast_out), scratch = rest[:n_cast], rest[n_cast:2 * n_cast + 1], rest[2 * n_cast + 1:]
    for w_ref, wb_ref in zip(cast_in, cast_out):
        wb_ref[...] = w_ref[...].astype(BF16)
    return o_ref, scratch


def _silu_gate(o, z):
    zf = z.astype(F32)
    return (o * (zf / (1.0 + jnp.exp(-zf)))).astype(BF16)


def _write_causal_tile(mask_ref):
    key = lax.broadcasted_iota(jnp.int32, (TQ, TQ), 0)
    qry = lax.broadcasted_iota(jnp.int32, (TQ, TQ), 1)
    mask_ref[...] = jnp.where(key <= qry, 0.0, -jnp.inf).astype(F32)


def _head_view(ref, head, width):
    return ref.at[0, :, head * width:(head + 1) * width]


def _moba_kernel(rb_ref, q_ref, k_ref, v_ref, z_ref, own_ref, prev_ref, *rest, heads, n_cast):
    o_ref, scratch = _cast_row_blocks(rest, n_cast)
    per_head = len(scratch) // heads
    _run_streams([
        _moba_head(rb_ref, pl.program_id(1) * heads + s, _head_view(q_ref, s, HEAD_DIM),
                   _head_view(k_ref, s, HEAD_DIM), _head_view(v_ref, s, HEAD_DIM),
                   _head_view(z_ref, s, HEAD_DIM), own_ref.at[s], prev_ref.at[s],
                   _head_view(o_ref, s, HEAD_DIM), *scratch[s * per_head:(s + 1) * per_head])
        for s in range(heads)])


def _moba_head(rb_ref, head, q_ref, k_ref, v_ref, z_ref, own_ref, prev_ref, o_ref, vt_ref, t0, t1, pb, km_ref):
    seq = q_ref.shape[0]
    n_tiles = seq // TQ
    far_const = rb_ref[N_BUCKETS - 1, head] * LOG2E
    km_ref[...] = jnp.zeros(km_ref.shape, F32)

    def prepare(i):
        _build_vt_block(v_ref, vt_ref, i)
        k_blk = k_ref[i * TQ:(i + 1) * TQ, :].astype(F32)
        km_ref[i:i + 1, :] = jnp.sum(k_blk, axis=0, keepdims=True) * (1.0 / MOBA_BLOCK)

    nt = (((1,), (1,)), ((), ()))

    def terms_of(i, q_tile):
        if i > MOBA_TOPK:
            k_mean = km_ref[...]
            km1 = k_mean.astype(BF16)
            rem = k_mean - km1.astype(F32)
            km2 = rem.astype(BF16)
            km3 = (rem - km2.astype(F32)).astype(BF16)
            gate = (lax.dot_general(km1, q_tile, nt, preferred_element_type=F32)
                    + lax.dot_general(km2, q_tile, nt, preferred_element_type=F32)
                    + lax.dot_general(km3, q_tile, nt, preferred_element_type=F32))
            sels = []
            for j in range(i):
                gj = gate[j:j + 1, :]
                cnt = jnp.zeros(gj.shape, F32)
                for jp in range(i):
                    if jp == j:
                        continue
                    gp = gate[jp:jp + 1, :]
                    beats = (gp >= gj) if jp < j else (gp > gj)
                    cnt = cnt + jnp.where(beats, 1.0, 0.0)
                sels.append(cnt < MOBA_TOPK)
        else:
            sels = [None] * i
        sels.append(None)
        adds = [None] * (i + 1)
        consts = [far_const] * (i + 1)
        adds[i], consts[i] = own_ref, 0.0
        if i >= 1:
            adds[i - 1], consts[i - 1] = prev_ref, 0.0
        return adds, consts, sels

    def emit(i, o):
        rows = slice(i * TQ, (i + 1) * TQ)
        o_ref[rows, :] = _silu_gate(o, z_ref[rows, :])

    return _attention(n_tiles, lambda i: q_ref[i * TQ:(i + 1) * TQ, :],
                      lambda j: k_ref[j * TQ:(j + 1) * TQ, :], terms_of, prepare, vt_ref, (t0, t1, pb), emit)


HEADS_PER_STEP = 4


def _attn_scratch(n_keys, extra=()):
    per_head = [pltpu.VMEM((VT_ROWS, n_keys), BF16), pltpu.VMEM((n_keys, TQ), F32),
                pltpu.VMEM((n_keys, TQ), F32), pltpu.VMEM((n_keys, TQ), BF16), *extra]
    return per_head * HEADS_PER_STEP


def _moba_attn(u3, rel_bias, own, prev, cast_weights, layer):
    b, s, _ = u3.shape
    assert s // MOBA_BLOCK <= BF16_TILE_ROWS
    hp = HEADS_PER_STEP
    width = hp * HEAD_DIM
    n_h = MOBA_HEADS // hp
    col = lambda off: (lambda bi, h: (bi, 0, off // width + h))
    blk = (1, s, width)
    tile_spec = pl.BlockSpec((hp, TQ, TQ), lambda bi, h: (h, 0, 0))
    cast_in, cast_out, cast_shapes, cast_bytes = _cast_plan(cast_weights, layer, b, n_h)
    est = (10 * s * width * 2 + 4 * hp * TQ * TQ * 4 + hp * (VT_ROWS * s * 2 + 2 * s * TQ * 4 + s * TQ * 2)
           + cast_bytes)
    outs = pl.pallas_call(
        functools.partial(_moba_kernel, heads=hp, n_cast=len(cast_weights)),
        grid=(b, n_h),
        in_specs=[
            pl.BlockSpec(memory_space=pltpu.SMEM),
            pl.BlockSpec(blk, col(OFF_QA)),
            pl.BlockSpec(blk, col(OFF_KA)),
            pl.BlockSpec(blk, col(OFF_VA)),
            pl.BlockSpec(blk, col(OFF_ZA)),
            tile_spec, tile_spec,
            *cast_in,
        ],
        out_specs=[pl.BlockSpec(blk, lambda bi, h: (bi, 0, h)), *cast_out],
        out_shape=[jax.ShapeDtypeStruct((b, s, MOBA_W), BF16), *cast_shapes],
        scratch_shapes=_attn_scratch(s, extra=(pltpu.VMEM((BF16_TILE_ROWS, HEAD_DIM), F32),)),
        compiler_params=pltpu.CompilerParams(
            dimension_semantics=("arbitrary", "arbitrary"), vmem_limit_bytes=_vmem_limit(est)),
        name="moba_attn",
    )(rel_bias, u3, u3, u3, u3, own, prev, *cast_weights)
    return outs[0], outs[1:]


def _mla_prep_kernel(cq_ref, ckv_ref, kr_ref, gq_ref, gkv_ref, wuq_ref, wukv_ref, cos_ref, sin_ref,
                     q_out, knope_out, krope_out, v_out, *, q_scale):
    def rms(x_ref, g_ref):
        xf = x_ref[...].astype(F32)
        ms = jnp.mean(xf * xf, axis=-1, keepdims=True)
        return (xf * lax.rsqrt(ms + EPS) * g_ref[...]).astype(BF16)

    cos = cos_ref[...]
    sin = sin_ref[...]
    half = MLA_ROPE // 2
    first_half = lax.broadcasted_iota(jnp.int32, cos.shape, 1) < half

    def rope(xr):
        partner = jnp.where(first_half, pltpu.roll(xr, LANES - half, 1), pltpu.roll(xr, half, 1))
        return xr * cos + partner * sin

    qb = jnp.dot(rms(cq_ref, gq_ref), wuq_ref[...], preferred_element_type=F32) * q_scale
    for h in range(MLA_HEADS):
        base = h * MLA_QK
        q_out[:, base:base + MLA_NOPE] = qb[:, base:base + MLA_NOPE].astype(BF16)
        q_out[:, base + MLA_NOPE:base + MLA_QK] = rope(qb[:, base + MLA_NOPE:base + MLA_QK]).astype(BF16)

    kvb = jnp.dot(rms(ckv_ref, gkv_ref), wukv_ref[...], preferred_element_type=F32)
    in_rope = lax.broadcasted_iota(jnp.int32, cos.shape, 1) < MLA_ROPE
    krope_out[...] = rope(jnp.where(in_rope, kr_ref[:, 0:LANES].astype(F32), 0.0)).astype(BF16)
    knope_out[...] = kvb[:, 0:MLA_W].astype(BF16)
    v_out[...] = kvb[:, MLA_W:2 * MLA_W].astype(BF16)


def _mla_prep(u, g_cq, g_ckv, wuq_p, wukv_p, cos_t, sin_t, *, seq, tm):
    t = u.shape[0]
    s_tiles = seq // tm
    const = lambda i: (0, 0)
    qk_shape = jax.ShapeDtypeStruct((t, MLA_HEADS * MLA_QK), BF16)
    est = (2 * tm * (MLA_Q_LORA + 2 * MLA_KV_LORA) * 2 + 2 * (wuq_p.size + wukv_p.size) * 2
           + 4 * tm * LANES * 4 + 2 * tm * 5 * MLA_W * 2 + 4 * tm * 2 * MLA_W * 4)
    return pl.pallas_call(
        functools.partial(_mla_prep_kernel, q_scale=(MLA_NOPE + MLA_ROPE) ** -0.5 * LOG2E),
        grid=(t // tm,),
        in_specs=[
            pl.BlockSpec((tm, MLA_Q_LORA), lambda i: (i, OFF_CQ // MLA_Q_LORA)),
            pl.BlockSpec((tm, MLA_KV_LORA), lambda i: (i, OFF_CKV // MLA_KV_LORA)),
            pl.BlockSpec((tm, KR_PAD), lambda i: (i, OFF_KR // KR_PAD)),
            pl.BlockSpec((1, MLA_Q_LORA), const),
            pl.BlockSpec((1, MLA_KV_LORA), const),
            pl.BlockSpec(wuq_p.shape, const),
            pl.BlockSpec(wukv_p.shape, const),
            pl.BlockSpec((tm, LANES), lambda i: (i % s_tiles, 0)),
            pl.BlockSpec((tm, LANES), lambda i: (i % s_tiles, 0)),
        ],
        out_specs=[
            pl.BlockSpec((tm, MLA_HEADS * MLA_QK), lambda i: (i, 0)),
            pl.BlockSpec((tm, MLA_W), lambda i: (i, 0)),
            pl.BlockSpec((tm, LANES), lambda i: (i, 0)),
            pl.BlockSpec((tm, MLA_W), lambda i: (i, 0)),
        ],
        out_shape=[qk_shape, jax.ShapeDtypeStruct((t, MLA_W), BF16), jax.ShapeDtypeStruct((t, LANES), BF16),
                   jax.ShapeDtypeStruct((t, MLA_W), BF16)],
        compiler_params=pltpu.CompilerParams(
            dimension_semantics=("arbitrary",), vmem_limit_bytes=_vmem_limit(est)),
        name="mla_prep",
    )(u, u, u, g_cq.reshape(1, -1), g_ckv.reshape(1, -1), wuq_p, wukv_p, cos_t, sin_t)


def _mla_attn_kernel(q_ref, kn_ref, kr_ref, v_ref, z_ref, o_ref, mask_ref, *scratch, heads):
    _write_causal_tile(mask_ref)
    per_head = len(scratch) // heads
    _run_streams([
        _mla_head(_head_view(q_ref, s, MLA_QK), _head_view(kn_ref, s, MLA_NOPE), kr_ref.at[0],
                  _head_view(v_ref, s, MLA_V), _head_view(z_ref, s, MLA_V), _head_view(o_ref, s, MLA_V),
                  mask_ref, *scratch[s * per_head:(s + 1) * per_head])
        for s in range(heads)])


def _mla_head(q_ref, kn_ref, kr_ref, v_ref, z_ref, o_ref, mask_ref, vt_ref, t0, t1, pb):
    seq = q_ref.shape[0]

    def terms_of(i, q_tile):
        return [None] * i + [mask_ref], [0.0] * (i + 1), [None] * (i + 1)

    def k_block_of(j):
        blk = slice(j * TQ, (j + 1) * TQ)
        return jnp.concatenate([kn_ref[blk, :], kr_ref[blk, :]], axis=1)

    def emit(i, o):
        rows = slice(i * TQ, (i + 1) * TQ)
        o_ref[rows, :] = _silu_gate(o, z_ref[rows, :])

    return _attention(seq // TQ, lambda i: q_ref[i * TQ:(i + 1) * TQ, :], k_block_of, terms_of,
                      functools.partial(_build_vt_block, v_ref, vt_ref), vt_ref, (t0, t1, pb), emit)


def _mla_attn(q3, kn3, kr3, v3, u3):
    b, s, _ = q3.shape
    hp = HEADS_PER_STEP
    assert MLA_NOPE + LANES == MLA_QK
    est = (2 * s * hp * MLA_QK * 2 + 2 * s * LANES * 2 + 8 * s * hp * MLA_V * 2
           + hp * (VT_ROWS * s * 2 + 2 * s * TQ * 4 + s * TQ * 2) + TQ * TQ * 4)
    return pl.pallas_call(
        functools.partial(_mla_attn_kernel, heads=hp),
        grid=(b, MLA_HEADS // hp),
        in_specs=[
            pl.BlockSpec((1, s, hp * MLA_QK), lambda bi, h: (bi, 0, h)),
            pl.BlockSpec((1, s, hp * MLA_NOPE), lambda bi, h: (bi, 0, h)),
            pl.BlockSpec((1, s, LANES), lambda bi, h: (bi, 0, 0)),
            pl.BlockSpec((1, s, hp * MLA_V), lambda bi, h: (bi, 0, h)),
            pl.BlockSpec((1, s, hp * MLA_V), lambda bi, h: (bi, 0, OFF_ZB // (hp * MLA_V) + h)),
        ],
        out_specs=pl.BlockSpec((1, s, hp * MLA_V), lambda bi, h: (bi, 0, h)),
        out_shape=jax.ShapeDtypeStruct((b, s, MLA_W), BF16),
        scratch_shapes=[pltpu.VMEM((TQ, TQ), F32)] + _attn_scratch(s),
        compiler_params=pltpu.CompilerParams(
            dimension_semantics=("arbitrary", "arbitrary"), vmem_limit_bytes=_vmem_limit(est)),
        name="mla_attn",
    )(q3, kn3, kr3, v3, u3)


def _mem_attn_kernel(q_ref, k_ref, v_ref, z_ref, *rest, heads, n_cast):
    o_ref, scratch = _cast_row_blocks(rest, n_cast)
    assert k_ref.shape[1] == TQ
    per_head = len(scratch) // heads
    d = MEM_HEAD_DIM
    _run_streams([
        _mem_head(_head_view(q_ref, s, d), _head_view(k_ref, s, d), _head_view(v_ref, s, d),
                  _head_view(z_ref, s, d), _head_view(o_ref, s, d), *scratch[s * per_head:(s + 1) * per_head])
        for s in range(heads)])


def _mem_head(q_ref, k_ref, v_ref, z_ref, o_ref, vt_ref, t0, t1, pb):
    seq = q_ref.shape[0]

    def prepare(i):
        if i == 0:
            _build_vt_block(v_ref, vt_ref, 0)

    def emit(i, o):
        rows = slice(i * TQ, (i + 1) * TQ)
        o_ref[rows, :] = _silu_gate(o, z_ref[rows, :])

    return _attention(seq // TQ, lambda i: q_ref[i * TQ:(i + 1) * TQ, :], lambda j: k_ref[...],
                      lambda i, q_tile: ([None], [0.0], [None]), prepare, vt_ref, (t0, t1, pb), emit)


def _mem_attn(u3, kv3, cast_weights, layer):
    b, s, _ = u3.shape
    m = kv3.shape[1]
    hp = HEADS_PER_STEP
    d = hp * MEM_HEAD_DIM
    n_h = MEM_HEADS // hp
    cast_in, cast_out, cast_shapes, cast_bytes = _cast_plan(cast_weights, layer, b, n_h)
    est = 6 * s * d * 2 + 4 * m * d * 2 + hp * (VT_ROWS * m * 2 + 2 * m * TQ * 4 + m * TQ * 2) + cast_bytes
    outs = pl.pallas_call(
        functools.partial(_mem_attn_kernel, heads=hp, n_cast=len(cast_weights)),
        grid=(b, n_h),
        in_specs=[
            pl.BlockSpec((1, s, d), lambda bi, h: (bi, 0, OFF_QM // d + h)),
            pl.BlockSpec((1, m, d), lambda bi, h: (bi, 0, h)),
            pl.BlockSpec((1, m, d), lambda bi, h: (bi, 0, MEM_W // d + h)),
            pl.BlockSpec((1, s, d), lambda bi, h: (bi, 0, OFF_ZM // d + h)),
            *cast_in,
        ],
        out_specs=[pl.BlockSpec((1, s, d), lambda bi, h: (bi, 0, h)), *cast_out],
        out_shape=[jax.ShapeDtypeStruct((b, s, MEM_W), BF16), *cast_shapes],
        scratch_shapes=_attn_scratch(m),
        compiler_params=pltpu.CompilerParams(
            dimension_semantics=("arbitrary", "arbitrary"), vmem_limit_bytes=_vmem_limit(est)),
        name="mem_attn",
    )(u3, kv3, kv3, u3, *cast_weights)
    return outs[0], outs[1:]


def _merge_kernel(ga_ref, gb_ref, gm_ref, gl_ref, x_ref, wpa_ref, wpb_ref, wpm_ref, wout_ref, gf_ref,
                  o_ref, *, final_norm):
    d = x_ref.shape[1]

    def gated(idx, g_ref, w_ref):
        logit = gl_ref[:, idx * d:(idx + 1) * d].astype(F32)
        return jnp.dot(g_ref[...], w_ref[...], preferred_element_type=F32) / (1.0 + jnp.exp(-logit))

    y = gated(0, ga_ref, wpa_ref) + gated(1, gb_ref, wpb_ref) + gated(2, gm_ref, wpm_ref)
    r = x_ref[...] + jnp.dot(y.astype(BF16), wout_ref[...], preferred_element_type=F32)
    if final_norm:
        ms = jnp.mean(r * r, axis=-1, keepdims=True)
        r = r * lax.rsqrt(ms + EPS) * gf_ref[...]
    o_ref[...] = r


def _merge(ga, gb, gm, u, x, wpa, wpb, wpm, wout, g_final, *, tm, final_norm):
    t, d = x.shape
    const = lambda i: (0, 0)
    resident = lambda w: pl.BlockSpec(w.shape, const, pipeline_mode=pl.Buffered(1))
    rows = lambda width: pl.BlockSpec((tm, width), lambda i: (i, 0))
    w_bytes = (wpa.size + wpb.size + wpm.size + wout.size) * 2
    est = w_bytes + 2 * tm * (2 * MOBA_W + MEM_W + 3 * d) * 2 + 4 * tm * d * 4 + 6 * tm * d * 4
    return pl.pallas_call(
        functools.partial(_merge_kernel, final_norm=final_norm),
        grid=(t // tm,),
        in_specs=[
            rows(MOBA_W), rows(MLA_W), rows(MEM_W),
            pl.BlockSpec((tm, 3 * d), lambda i: (i, OFF_GL // (3 * d))),
            rows(d),
            resident(wpa), resident(wpb), resident(wpm), resident(wout),
            pl.BlockSpec((1, d), const),
        ],
        out_specs=rows(d),
        out_shape=jax.ShapeDtypeStruct((t, d), F32),
        compiler_params=pltpu.CompilerParams(
            dimension_semantics=("arbitrary",), vmem_limit_bytes=_vmem_limit(est)),
        name="merge",
    )(ga, gb, gm, u, x, wpa, wpb, wpm, wout, g_final.reshape(1, d))


def _w_in_tile_sources(n_cols):
    o_cq = 4 * MOBA_W
    o_zb = o_cq + MLA_Q_LORA + MLA_KV_LORA + MLA_ROPE
    o_qm = o_zb + MLA_W
    o_gl = o_qm + 2 * MEM_W
    srcs = ([o_gl + W_TILE * k for k in range(3 * D_MODEL // W_TILE)]
            + [W_TILE * k for k in range(4 * MOBA_W // W_TILE)] + [o_zb, o_cq, o_qm])
    assert len(srcs) * W_TILE == IN_WIDTH_P and o_gl + 3 * D_MODEL == n_cols
    assert all(src + W_TILE <= n_cols for src in srcs)
    return srcs


def _in_col_scale():
    cs = jnp.ones((1, IN_WIDTH_P), F32)
    cs = cs.at[:, OFF_QA:OFF_QA + MOBA_W].set(HEAD_DIM ** -0.5 * LOG2E)
    return cs.at[:, OFF_QM:OFF_QM + MEM_W].set(MEM_HEAD_DIM ** -0.5 * LOG2E)


def _regroup_w_uq(w):
    r = w.shape[0]
    w3 = w.reshape(r, MLA_HEADS, MLA_NOPE + MLA_ROPE)
    pad = jnp.zeros((r, MLA_HEADS, MLA_QK - MLA_NOPE - MLA_ROPE), w.dtype)
    return jnp.concatenate([w3, pad], axis=-1).reshape(r, MLA_HEADS * MLA_QK).astype(BF16)


def _regroup_w_ukv(w):
    r = w.shape[0]
    w3 = w.reshape(r, MLA_HEADS, MLA_NOPE + MLA_V)
    return jnp.concatenate([w3[:, :, :MLA_NOPE].reshape(r, MLA_W),
                            w3[:, :, MLA_NOPE:].reshape(r, MLA_W)], axis=1).astype(BF16)


def _rope_tables(seq):
    half = MLA_ROPE // 2
    inv = ROPE_THETA ** (-jnp.arange(half, dtype=F32) / half)
    ang = jnp.arange(seq, dtype=jnp.int32).astype(F32)[:, None] * inv[None, :]
    cos, sin = jnp.cos(ang), jnp.sin(ang)
    pad = LANES - MLA_ROPE
    cos_t = jnp.concatenate([cos, cos, jnp.ones((seq, pad), F32)], axis=1)
    sin_t = jnp.concatenate([-sin, sin, jnp.zeros((seq, pad), F32)], axis=1)
    return cos_t, sin_t


def kernel(x, mem, g_norm, w_in, g_cq, w_uq, g_ckv, w_ukv, g_mem, w_mem_kv, rel_bias,
           w_p_moba, w_p_mla, w_p_mem, w_out, g_final):
    b, s, d = x.shape
    m = mem.shape[1]
    depth = w_in.shape[0]
    t = b * s
    assert d == D_MODEL and s % TQ == 0 and m == TQ

    own, prev = _bias_tiles(rel_bias)
    cos_t, sin_t = _rope_tables(s)
    mem2 = mem.reshape(b * m, d)
    xs = x.reshape(t, d)
    in_scale = _in_col_scale()
    w_in_t = jnp.swapaxes(w_in, 1, 2)
    for l in range(depth):
        u = _in_proj(xs, g_norm[l], w_in_t, l, in_scale, tm=1024)
        u3 = u.reshape(b, s, IN_WIDTH_P)
        ga, (wout,) = _moba_attn(u3, rel_bias, own, prev, (w_out,), l)
        q2, kn2, kr2, v2 = _mla_prep(u, g_cq[l], g_ckv[l], _regroup_w_uq(w_uq[l]), _regroup_w_ukv(w_ukv[l]),
                                     cos_t, sin_t, seq=s, tm=512)
        gb = _mla_attn(q2.reshape(b, s, -1), kn2.reshape(b, s, -1), kr2.reshape(b, s, -1),
                       v2.reshape(b, s, -1), u3)
        kvm = _norm_matmul(mem2, g_mem[l], w_mem_kv, l, tm=b * m, tn=MEM_W)
        gm, (wpa, wpb, wpm) = _mem_attn(u3, kvm.reshape(b, m, 2 * MEM_W), (w_p_moba, w_p_mla, w_p_mem), l)
        xs = _merge(ga.reshape(t, MOBA_W), gb.reshape(t, MLA_W), gm.reshape(t, MEM_W), u, xs,
                    wpa, wpb, wpm, wout, g_final, tm=256, final_norm=(l == depth - 1))
    return xs.reshape(b, s, d)
```

```python
import functools
import math

import jax
import jax.numpy as jnp
from jax import lax
from jax.experimental import pallas as pl
from jax.experimental.pallas import tpu as pltpu

D_MODEL = 2048
MOBA_HEADS = 8
HEAD_DIM = 128
MOBA_BLOCK = 256
MOBA_TOPK = 3
MLA_HEADS = 8
MLA_Q_LORA = 512
MLA_KV_LORA = 256
MLA_NOPE = 128
MLA_ROPE = 64
MLA_V = 128
ROPE_THETA = 10000.0
MEM_HEADS = 4
MEM_HEAD_DIM = 128
N_BUCKETS = 32
MAX_DISTANCE = 128
EPS = 1e-6

MOBA_W = MOBA_HEADS * HEAD_DIM
MLA_W = MLA_HEADS * MLA_V
MEM_W = MEM_HEADS * MEM_HEAD_DIM

LANES = 128
MXU_DIM = 256
V7X_VMEM_BYTES = 64 * 1024 * 1024
VMEM_HEADROOM_BYTES = 4 * 1024 * 1024
KERNEL_TEMP_BYTES = 8 * 1024 * 1024

BF16 = jnp.bfloat16
F32 = jnp.float32
LOG2E = 1.4426950408889634

KR_PAD = MXU_DIM
OFF_GL = 0
OFF_QA = OFF_GL + 3 * D_MODEL
OFF_KA = OFF_QA + MOBA_W
OFF_VA = OFF_KA + MOBA_W
OFF_ZA = OFF_VA + MOBA_W
OFF_ZB = OFF_ZA + MOBA_W
OFF_CQ = OFF_ZB + MLA_W
OFF_CKV = OFF_CQ + MLA_Q_LORA
OFF_KR = OFF_CKV + MLA_KV_LORA
OFF_QM = OFF_KR + KR_PAD
OFF_ZM = OFF_QM + MEM_W
IN_WIDTH_P = OFF_ZM + MEM_W

TQ = MOBA_BLOCK
BF16_TILE_ROWS = 16
ONES_ROWS = BF16_TILE_ROWS
VT_ROWS = MLA_V + ONES_ROWS
MLA_QK = MXU_DIM

W_TILE = 1024
W_SRC_UNIT = 64
NORM_CHUNK_ROWS = 64


def _vmem_limit(block_bytes):
    return int(min(block_bytes + KERNEL_TEMP_BYTES, V7X_VMEM_BYTES - VMEM_HEADROOM_BYTES))


def _t5_thresholds():
    max_exact = N_BUCKETS // 2

    def bucket(d):
        if d < max_exact:
            return d
        large = max_exact + int(math.log(d / max_exact) / math.log(MAX_DISTANCE / max_exact)
                                * (N_BUCKETS - max_exact))
        return min(large, N_BUCKETS - 1)

    thr, d = [], 0
    for b in range(1, N_BUCKETS):
        while bucket(d) < b:
            d += 1
        thr.append(d)
    return tuple(thr)


T5_THRESHOLDS = _t5_thresholds()
assert T5_THRESHOLDS[-1] <= MOBA_BLOCK + 1


def _norm_rows(x_ref, g_ref, h_ref, chunk):
    def body(r, carry):
        rows = pl.ds(pl.multiple_of(r * chunk, chunk), chunk)
        xv = x_ref[rows, :]
        ms = jnp.mean(xv * xv, axis=-1, keepdims=True)
        h_ref[rows, :] = (xv * lax.rsqrt(ms + EPS) * g_ref[...]).astype(BF16)
        return carry
    lax.fori_loop(0, x_ref.shape[0] // chunk, body, 0, unroll=4)


def _norm_matmul_kernel(x_ref, g_ref, w_ref, o_ref, h_ref, *, chunk):
    pl.when(pl.program_id(1) == 0)(functools.partial(_norm_rows, x_ref, g_ref, h_ref, chunk))
    o_ref[...] = jnp.dot(h_ref[...], w_ref[...].astype(BF16), preferred_element_type=F32).astype(o_ref.dtype)


def _in_proj_kernel(src_ref, x_ref, g_ref, wt_ref, cs_ref, o_ref, h_ref, *, chunk):
    del src_ref
    pl.when(pl.program_id(1) == 0)(functools.partial(_norm_rows, x_ref, g_ref, h_ref, chunk))
    acc = lax.dot_general(h_ref[...], wt_ref[...].astype(BF16), (((1,), (1,)), ((), ())),
                          preferred_element_type=F32)
    o_ref[...] = (acc * cs_ref[...]).astype(o_ref.dtype)


def _in_proj(x, g, wt_all, layer, col_scale, *, tm):
    t, d = x.shape
    srcs = _w_in_tile_sources(wt_all.shape[1])
    unit = W_SRC_UNIT
    assert all(src % unit == 0 for src in srcs)
    src_units = jnp.asarray([src // unit for src in srcs], jnp.int32)
    est = 2 * tm * d * 4 + tm * d * 2 + 2 * W_TILE * d * 4 + W_TILE * d * 2 + 2 * tm * W_TILE * 2 + tm * W_TILE * 4
    return pl.pallas_call(
        functools.partial(_in_proj_kernel, chunk=NORM_CHUNK_ROWS),
        grid_spec=pltpu.PrefetchScalarGridSpec(
            num_scalar_prefetch=1,
            grid=(t // tm, len(srcs)),
            in_specs=[
                pl.BlockSpec((tm, d), lambda i, j, src: (i, 0)),
                pl.BlockSpec((1, d), lambda i, j, src: (0, 0)),
                pl.BlockSpec((pl.Squeezed(), pl.Element(W_TILE), pl.Element(d)),
                             lambda i, j, src: (layer, src[j] * unit, 0)),
                pl.BlockSpec((1, W_TILE), lambda i, j, src: (0, j)),
            ],
            out_specs=pl.BlockSpec((tm, W_TILE), lambda i, j, src: (i, j)),
            scratch_shapes=[pltpu.VMEM((tm, d), BF16)],
        ),
        out_shape=jax.ShapeDtypeStruct((t, IN_WIDTH_P), BF16),
        compiler_params=pltpu.CompilerParams(
            dimension_semantics=("arbitrary", "arbitrary"), vmem_limit_bytes=_vmem_limit(est)),
        name="in_proj",
    )(src_units, x, g.reshape(1, d), wt_all, col_scale)


def _norm_matmul(x, g, w_all, layer, *, tm, tn):
    t, d = x.shape
    n = w_all.shape[2]
    est = 2 * tm * d * 4 + tm * d * 2 + 2 * d * tn * 4 + d * tn * 2 + 2 * tm * tn * 2 + tm * tn * 4
    return pl.pallas_call(
        functools.partial(_norm_matmul_kernel, chunk=NORM_CHUNK_ROWS),
        grid=(t // tm, n // tn),
        in_specs=[
            pl.BlockSpec((tm, d), lambda i, j: (i, 0)),
            pl.BlockSpec((1, d), lambda i, j: (0, 0)),
            pl.BlockSpec((pl.Squeezed(), d, tn), lambda i, j: (layer, 0, j)),
        ],
        out_specs=pl.BlockSpec((tm, tn), lambda i, j: (i, j)),
        out_shape=jax.ShapeDtypeStruct((t, n), BF16),
        scratch_shapes=[pltpu.VMEM((tm, d), BF16)],
        compiler_params=pltpu.CompilerParams(
            dimension_semantics=("arbitrary", "arbitrary"), vmem_limit_bytes=_vmem_limit(est)),
        name="norm_matmul",
    )(x, g.reshape(1, d), w_all)


def _bias_tiles_kernel(rb_ref, own_ref, prev_ref):
    key = lax.broadcasted_iota(jnp.int32, (TQ, TQ), 0)
    qry = lax.broadcasted_iota(jnp.int32, (TQ, TQ), 1)
    d_own = qry - key
    d_prev = d_own + MOBA_BLOCK

    def lookup(dist, h):
        val = jnp.zeros(dist.shape, F32) + rb_ref[0, h]
        for b in range(1, N_BUCKETS):
            val = jnp.where(dist >= T5_THRESHOLDS[b - 1], rb_ref[b, h], val)
        return val * LOG2E

    for h in range(own_ref.shape[0]):
        own_ref[h] = jnp.where(d_own >= 0, lookup(d_own, h), -jnp.inf)
        prev_ref[h] = lookup(d_prev, h)


def _bias_tiles(rel_bias):
    heads = rel_bias.shape[1]
    tile = jax.ShapeDtypeStruct((heads, TQ, TQ), F32)
    spec = pl.BlockSpec((heads, TQ, TQ), lambda i: (0, 0, 0))
    return pl.pallas_call(
        _bias_tiles_kernel,
        grid=(1,),
        in_specs=[pl.BlockSpec(memory_space=pltpu.SMEM)],
        out_specs=[spec, spec],
        out_shape=[tile, tile],
        name="bias_tiles",
    )(rel_bias)


def _build_vt_block(v_ref, vt_ref, j):
    dv = v_ref.shape[-1]
    blk = slice(j * TQ, (j + 1) * TQ)
    vt_ref[0:dv, blk] = v_ref[blk, :].astype(F32).T.astype(BF16)
    row = lax.broadcasted_iota(jnp.int32, (ONES_ROWS, TQ), 0)
    vt_ref[dv:dv + ONES_ROWS, blk] = jnp.where(row == 0, 1.0, 0.0).astype(BF16)


def _attention(n_tiles, q_tile_of, k_block_of, terms_of, prepare, vt_ref, bufs, emit):
    nt = (((1,), (1,)), ((), ()))
    dv = vt_ref.shape[0] - ONES_ROWS
    t_bufs, p_buf = bufs[:2], bufs[2]
    state = {}

    def stage1(i):
        prepare(i)
        q = q_tile_of(i)
        adds, consts, sels = terms_of(i, q)
        st = state[i] = dict(consts=consts, sels=sels, m=None)
        t_buf = t_bufs[i % 2]

        def item(j):
            blk = slice(j * TQ, (j + 1) * TQ)
            t = lax.dot_general(k_block_of(j), q, nt, preferred_element_type=F32)
            if adds[j] is not None:
                t = t + adds[j][...]
            t_buf[blk, :] = t
            mj = jnp.max(t, axis=0, keepdims=True) + consts[j]
            if sels[j] is not None:
                mj = jnp.where(sels[j], mj, -jnp.inf)
            st["m"] = mj if st["m"] is None else jnp.maximum(st["m"], mj)
        return [functools.partial(item, j) for j in range(len(consts))]

    def stage2(i):
        st = state.pop(i)
        t_buf = t_bufs[i % 2]

        def item(j):
            blk = slice(j * TQ, (j + 1) * TQ)
            off = st["m"] - st["consts"][j]
            if st["sels"][j] is not None:
                off = jnp.where(st["sels"][j], off, jnp.inf)
            p_buf[blk, :] = jnp.exp2(t_buf[blk, :] - off).astype(BF16)

        def finish():
            n_keys = len(st["consts"]) * TQ
            acc = jnp.dot(vt_ref[:, 0:n_keys], p_buf[0:n_keys, :], preferred_element_type=F32)
            emit(i, (acc[0:dv, :] / acc[dv:dv + 1, :]).T)
        return [functools.partial(item, j) for j in range(len(st["consts"]))], finish

    for item in stage1(0):
        item()
        yield
    for i in range(n_tiles):
        ahead = stage1(i + 1) if i + 1 < n_tiles else []
        behind, finish = stage2(i)
        for k in range(max(len(ahead), len(behind))):
            if k < len(ahead):
                ahead[k]()
            if k < len(behind):
                behind[k]()
            yield
        finish()


def _run_streams(streams):
    active = list(streams)
    while active:
        for stream in list(active):
            if next(stream, StopIteration) is StopIteration:
                active.remove(stream)


def _cast_plan(weights, layer, n_b, n_h):
    n_steps = n_b * n_h
    ins, outs, shapes, nbytes = [], [], [], 0
    for w in weights:
        _, rows, cols = w.shape
        assert rows % (n_steps * BF16_TILE_ROWS) == 0
        blk_rows = rows // n_steps
        ins.append(pl.BlockSpec((pl.Squeezed(), blk_rows, cols), lambda bi, h: (layer, bi * n_h + h, 0)))
        outs.append(pl.BlockSpec((blk_rows, cols), lambda bi, h: (bi * n_h + h, 0)))
        shapes.append(jax.ShapeDtypeStruct((rows, cols), BF16))
        nbytes += 2 * blk_rows * cols * (4 + 2)
    return ins, outs, shapes, nbytes


def _cast_row_blocks(rest, n_cast):
    cast_in, (o_ref, *cast_out), scratch = rest[:n_cast], rest[n_cast:2 * n_cast + 1], rest[2 * n_cast + 1:]
    for w_ref, wb_ref in zip(cast_in, cast_out):
        wb_ref[...] = w_ref[...].astype(BF16)
    return o_ref, scratch


def _silu_gate(o, z):
    zf = z.astype(F32)
    return (o * (zf / (1.0 + jnp.exp(-zf)))).astype(BF16)


def _write_causal_tile(mask_ref):
    key = lax.broadcasted_iota(jnp.int32, (TQ, TQ), 0)
    qry = lax.broadcasted_iota(jnp.int32, (TQ, TQ), 1)
    mask_ref[...] = jnp.where(key <= qry, 0.0, -jnp.inf).astype(F32)


def _head_view(ref, head, width):
    return ref.at[0, :, head * width:(head + 1) * width]


def _moba_kernel(rb_ref, q_ref, k_ref, v_ref, z_ref, own_ref, prev_ref, *rest, heads, n_cast):
    o_ref, scratch = _cast_row_blocks(rest, n_cast)
    per_head = len(scratch) // heads
    _run_streams([
        _moba_head(rb_ref, pl.program_id(1) * heads + s, _head_view(q_ref, s, HEAD_DIM),
                   _head_view(k_ref, s, HEAD_DIM), _head_view(v_ref, s, HEAD_DIM),
                   _head_view(z_ref, s, HEAD_DIM), own_ref.at[s], prev_ref.at[s],
                   _head_view(o_ref, s, HEAD_DIM), *scratch[s * per_head:(s + 1) * per_head])
        for s in range(heads)])


def _moba_head(rb_ref, head, q_ref, k_ref, v_ref, z_ref, own_ref, prev_ref, o_ref, vt_ref, t0, t1, pb, km_ref):
    seq = q_ref.shape[0]
    n_tiles = seq // TQ
    far_const = rb_ref[N_BUCKETS - 1, head] * LOG2E
    km_ref[...] = jnp.zeros(km_ref.shape, F32)

    def prepare(i):
        _build_vt_block(v_ref, vt_ref, i)
        k_blk = k_ref[i * TQ:(i + 1) * TQ, :].astype(F32)
        km_ref[i:i + 1, :] = jnp.sum(k_blk, axis=0, keepdims=True) * (1.0 / MOBA_BLOCK)

    nt = (((1,), (1,)), ((), ()))

    def terms_of(i, q_tile):
        if i > MOBA_TOPK:
            k_mean = km_ref[...]
            km1 = k_mean.astype(BF16)
            rem = k_mean - km1.astype(F32)
            km2 = rem.astype(BF16)
            km3 = (rem - km2.astype(F32)).astype(BF16)
            gate = (lax.dot_general(km1, q_tile, nt, preferred_element_type=F32)
                    + lax.dot_general(km2, q_tile, nt, preferred_element_type=F32)
                    + lax.dot_general(km3, q_tile, nt, preferred_element_type=F32))
            sels = []
            for j in range(i):
                gj = gate[j:j + 1, :]
                cnt = jnp.zeros(gj.shape, F32)
                for jp in range(i):
                    if jp == j:
                        continue
                    gp = gate[jp:jp + 1, :]
                    beats = (gp >= gj) if jp < j else (gp > gj)
                    cnt = cnt + jnp.where(beats, 1.0, 0.0)
                sels.append(cnt < MOBA_TOPK)
        else:
            sels = [None] * i
        sels.append(None)
        adds = [None] * (i + 1)
        consts = [far_const] * (i + 1)
        adds[i], consts[i] = own_ref, 0.0
        if i >= 1:
            adds[i - 1], consts[i - 1] = prev_ref, 0.0
        return adds, consts, sels

    def emit(i, o):
        rows = slice(i * TQ, (i + 1) * TQ)
        o_ref[rows, :] = _silu_gate(o, z_ref[rows, :])

    return _attention(n_tiles, lambda i: q_ref[i * TQ:(i + 1) * TQ, :],
                      lambda j: k_ref[j * TQ:(j + 1) * TQ, :], terms_of, prepare, vt_ref, (t0, t1, pb), emit)


HEADS_PER_STEP = 4


def _attn_scratch(n_keys, extra=()):
    per_head = [pltpu.VMEM((VT_ROWS, n_keys), BF16), pltpu.VMEM((n_keys, TQ), F32),
                pltpu.VMEM((n_keys, TQ), F32), pltpu.VMEM((n_keys, TQ), BF16), *extra]
    return per_head * HEADS_PER_STEP


def _moba_attn(u3, rel_bias, own, prev, cast_weights, layer):
    b, s, _ = u3.shape
    assert s // MOBA_BLOCK <= BF16_TILE_ROWS
    hp = HEADS_PER_STEP
    width = hp * HEAD_DIM
    n_h = MOBA_HEADS // hp
    col = lambda off: (lambda bi, h: (bi, 0, off // width + h))
    blk = (1, s, width)
    tile_spec = pl.BlockSpec((hp, TQ, TQ), lambda bi, h: (h, 0, 0))
    cast_in, cast_out, cast_shapes, cast_bytes = _cast_plan(cast_weights, layer, b, n_h)
    est = (10 * s * width * 2 + 4 * hp * TQ * TQ * 4 + hp * (VT_ROWS * s * 2 + 2 * s * TQ * 4 + s * TQ * 2)
           + cast_bytes)
    outs = pl.pallas_call(
        functools.partial(_moba_kernel, heads=hp, n_cast=len(cast_weights)),
        grid=(b, n_h),
        in_specs=[
            pl.BlockSpec(memory_space=pltpu.SMEM),
            pl.BlockSpec(blk, col(OFF_QA)),
            pl.BlockSpec(blk, col(OFF_KA)),
            pl.BlockSpec(blk, col(OFF_VA)),
            pl.BlockSpec(blk, col(OFF_ZA)),
            tile_spec, tile_spec,
            *cast_in,
        ],
        out_specs=[pl.BlockSpec(blk, lambda bi, h: (bi, 0, h)), *cast_out],
        out_shape=[jax.ShapeDtypeStruct((b, s, MOBA_W), BF16), *cast_shapes],
        scratch_shapes=_attn_scratch(s, extra=(pltpu.VMEM((BF16_TILE_ROWS, HEAD_DIM), F32),)),
        compiler_params=pltpu.CompilerParams(
            dimension_semantics=("arbitrary", "arbitrary"), vmem_limit_bytes=_vmem_limit(est)),
        name="moba_attn",
    )(rel_bias, u3, u3, u3, u3, own, prev, *cast_weights)
    return outs[0], outs[1:]


def _mla_prep_kernel(cq_ref, ckv_ref, kr_ref, gq_ref, gkv_ref, wuq_ref, wukv_ref, cos_ref, sin_ref,
                     q_out, k_out, v_out, *, q_scale):
    def rms(x_ref, g_ref):
        xf = x_ref[...].astype(F32)
        ms = jnp.mean(xf * xf, axis=-1, keepdims=True)
        return (xf * lax.rsqrt(ms + EPS) * g_ref[...]).astype(BF16)

    cos = cos_ref[...]
    sin = sin_ref[...]
    half = MLA_ROPE // 2
    first_half = lax.broadcasted_iota(jnp.int32, cos.shape, 1) < half

    def rope(xr):
        partner = jnp.where(first_half, pltpu.roll(xr, LANES - half, 1), pltpu.roll(xr, half, 1))
        return xr * cos + partner * sin

    qb = jnp.dot(rms(cq_ref, gq_ref), wuq_ref[...], preferred_element_type=F32) * q_scale
    for h in range(MLA_HEADS):
        base = h * MLA_QK
        q_out[:, base:base + MLA_NOPE] = qb[:, base:base + MLA_NOPE].astype(BF16)
        q_out[:, base + MLA_NOPE:base + MLA_QK] = rope(qb[:, base + MLA_NOPE:base + MLA_QK]).astype(BF16)

    kvb = jnp.dot(rms(ckv_ref, gkv_ref), wukv_ref[...], preferred_element_type=F32)
    in_rope = lax.broadcasted_iota(jnp.int32, cos.shape, 1) < MLA_ROPE
    k_rope = rope(jnp.where(in_rope, kr_ref[:, 0:LANES].astype(F32), 0.0)).astype(BF16)
    for h in range(MLA_HEADS):
        base = h * MLA_QK
        k_out[:, base:base + MLA_NOPE] = kvb[:, h * MLA_NOPE:(h + 1) * MLA_NOPE].astype(BF16)
        k_out[:, base + MLA_NOPE:base + MLA_QK] = k_rope
    v_out[...] = kvb[:, MLA_W:2 * MLA_W].astype(BF16)


def _mla_prep(u, g_cq, g_ckv, wuq_p, wukv_p, cos_t, sin_t, *, seq, tm):
    t = u.shape[0]
    s_tiles = seq // tm
    const = lambda i: (0, 0)
    qk_shape = jax.ShapeDtypeStruct((t, MLA_HEADS * MLA_QK), BF16)
    est = (2 * tm * (MLA_Q_LORA + 2 * MLA_KV_LORA) * 2 + 2 * (wuq_p.size + wukv_p.size) * 2
           + 4 * tm * LANES * 4 + 2 * tm * 5 * MLA_W * 2 + 4 * tm * 2 * MLA_W * 4)
    return pl.pallas_call(
        functools.partial(_mla_prep_kernel, q_scale=(MLA_NOPE + MLA_ROPE) ** -0.5 * LOG2E),
        grid=(t // tm,),
        in_specs=[
            pl.BlockSpec((tm, MLA_Q_LORA), lambda i: (i, OFF_CQ // MLA_Q_LORA)),
            pl.BlockSpec((tm, MLA_KV_LORA), lambda i: (i, OFF_CKV // MLA_KV_LORA)),
            pl.BlockSpec((tm, KR_PAD), lambda i: (i, OFF_KR // KR_PAD)),
            pl.BlockSpec((1, MLA_Q_LORA), const),
            pl.BlockSpec((1, MLA_KV_LORA), const),
            pl.BlockSpec(wuq_p.shape, const),
            pl.BlockSpec(wukv_p.shape, const),
            pl.BlockSpec((tm, LANES), lambda i: (i % s_tiles, 0)),
            pl.BlockSpec((tm, LANES), lambda i: (i % s_tiles, 0)),
        ],
        out_specs=[
            pl.BlockSpec((tm, MLA_HEADS * MLA_QK), lambda i: (i, 0)),
            pl.BlockSpec((tm, MLA_HEADS * MLA_QK), lambda i: (i, 0)),
            pl.BlockSpec((tm, MLA_W), lambda i: (i, 0)),
        ],
        out_shape=[qk_shape, qk_shape, jax.ShapeDtypeStruct((t, MLA_W), BF16)],
        compiler_params=pltpu.CompilerParams(
            dimension_semantics=("arbitrary",), vmem_limit_bytes=_vmem_limit(est)),
        name="mla_prep",
    )(u, u, u, g_cq.reshape(1, -1), g_ckv.reshape(1, -1), wuq_p, wukv_p, cos_t, sin_t)


def _mla_attn_kernel(q_ref, k_ref, v_ref, z_ref, o_ref, mask_ref, *scratch, heads):
    _write_causal_tile(mask_ref)
    per_head = len(scratch) // heads
    _run_streams([
        _mla_head(_head_view(q_ref, s, MLA_QK), _head_view(k_ref, s, MLA_QK), _head_view(v_ref, s, MLA_V),
                  _head_view(z_ref, s, MLA_V), _head_view(o_ref, s, MLA_V), mask_ref,
                  *scratch[s * per_head:(s + 1) * per_head])
        for s in range(heads)])


def _mla_head(q_ref, k_ref, v_ref, z_ref, o_ref, mask_ref, vt_ref, t0, t1, pb):
    seq = q_ref.shape[0]

    def terms_of(i, q_tile):
        return [None] * i + [mask_ref], [0.0] * (i + 1), [None] * (i + 1)

    def emit(i, o):
        rows = slice(i * TQ, (i + 1) * TQ)
        o_ref[rows, :] = _silu_gate(o, z_ref[rows, :])

    return _attention(seq // TQ, lambda i: q_ref[i * TQ:(i + 1) * TQ, :],
                      lambda j: k_ref[j * TQ:(j + 1) * TQ, :], terms_of,
                      functools.partial(_build_vt_block, v_ref, vt_ref), vt_ref, (t0, t1, pb), emit)


def _mla_attn(q3, k3, v3, u3):
    b, s, _ = q3.shape
    hp = HEADS_PER_STEP
    est = (4 * s * hp * MLA_QK * 2 + 6 * s * hp * MLA_V * 2 + hp * (VT_ROWS * s * 2 + 2 * s * TQ * 4 + s * TQ * 2)
           + TQ * TQ * 4)
    return pl.pallas_call(
        functools.partial(_mla_attn_kernel, heads=hp),
        grid=(b, MLA_HEADS // hp),
        in_specs=[
            pl.BlockSpec((1, s, hp * MLA_QK), lambda bi, h: (bi, 0, h)),
            pl.BlockSpec((1, s, hp * MLA_QK), lambda bi, h: (bi, 0, h)),
            pl.BlockSpec((1, s, hp * MLA_V), lambda bi, h: (bi, 0, h)),
            pl.BlockSpec((1, s, hp * MLA_V), lambda bi, h: (bi, 0, OFF_ZB // (hp * MLA_V) + h)),
        ],
        out_specs=pl.BlockSpec((1, s, hp * MLA_V), lambda bi, h: (bi, 0, h)),
        out_shape=jax.ShapeDtypeStruct((b, s, MLA_W), BF16),
        scratch_shapes=[pltpu.VMEM((TQ, TQ), F32)] + _attn_scratch(s),
        compiler_params=pltpu.CompilerParams(
            dimension_semantics=("arbitrary", "arbitrary"), vmem_limit_bytes=_vmem_limit(est)),
        name="mla_attn",
    )(q3, k3, v3, u3)


def _mem_attn_kernel(q_ref, k_ref, v_ref, z_ref, *rest, heads, n_cast):
    o_ref, scratch = _cast_row_blocks(rest, n_cast)
    assert k_ref.shape[1] == TQ
    per_head = len(scratch) // heads
    d = MEM_HEAD_DIM
    _run_streams([
        _mem_head(_head_view(q_ref, s, d), _head_view(k_ref, s, d), _head_view(v_ref, s, d),
                  _head_view(z_ref, s, d), _head_view(o_ref, s, d), *scratch[s * per_head:(s + 1) * per_head])
        for s in range(heads)])


def _mem_head(q_ref, k_ref, v_ref, z_ref, o_ref, vt_ref, t0, t1, pb):
    seq = q_ref.shape[0]

    def prepare(i):
        if i == 0:
            _build_vt_block(v_ref, vt_ref, 0)

    def emit(i, o):
        rows = slice(i * TQ, (i + 1) * TQ)
        o_ref[rows, :] = _silu_gate(o, z_ref[rows, :])

    return _attention(seq // TQ, lambda i: q_ref[i * TQ:(i + 1) * TQ, :], lambda j: k_ref[...],
                      lambda i, q_tile: ([None], [0.0], [None]), prepare, vt_ref, (t0, t1, pb), emit)


def _mem_attn(u3, kv3, cast_weights, layer):
    b, s, _ = u3.shape
    m = kv3.shape[1]
    hp = HEADS_PER_STEP
    d = hp * MEM_HEAD_DIM
    n_h = MEM_HEADS // hp
    cast_in, cast_out, cast_shapes, cast_bytes = _cast_plan(cast_weights, layer, b, n_h)
    est = 6 * s * d * 2 + 4 * m * d * 2 + hp * (VT_ROWS * m * 2 + 2 * m * TQ * 4 + m * TQ * 2) + cast_bytes
    outs = pl.pallas_call(
        functools.partial(_mem_attn_kernel, heads=hp, n_cast=len(cast_weights)),
        grid=(b, n_h),
        in_specs=[
            pl.BlockSpec((1, s, d), lambda bi, h: (bi, 0, OFF_QM // d + h)),
            pl.BlockSpec((1, m, d), lambda bi, h: (bi, 0, h)),
            pl.BlockSpec((1, m, d), lambda bi, h: (bi, 0, MEM_W // d + h)),
            pl.BlockSpec((1, s, d), lambda bi, h: (bi, 0, OFF_ZM // d + h)),
            *cast_in,
        ],
        out_specs=[pl.BlockSpec((1, s, d), lambda bi, h: (bi, 0, h)), *cast_out],
        out_shape=[jax.ShapeDtypeStruct((b, s, MEM_W), BF16), *cast_shapes],
        scratch_shapes=_attn_scratch(m),
        compiler_params=pltpu.CompilerParams(
            dimension_semantics=("arbitrary", "arbitrary"), vmem_limit_bytes=_vmem_limit(est)),
        name="mem_attn",
    )(u3, kv3, kv3, u3, *cast_weights)
    return outs[0], outs[1:]


def _merge_kernel(ga_ref, gb_ref, gm_ref, gl_ref, x_ref, wpa_ref, wpb_ref, wpm_ref, wout_ref, gf_ref,
                  o_ref, *, final_norm):
    d = x_ref.shape[1]

    def gated(idx, g_ref, w_ref):
        logit = gl_ref[:, idx * d:(idx + 1) * d].astype(F32)
        return jnp.dot(g_ref[...], w_ref[...], preferred_element_type=F32) / (1.0 + jnp.exp(-logit))

    y = gated(0, ga_ref, wpa_ref) + gated(1, gb_ref, wpb_ref) + gated(2, gm_ref, wpm_ref)
    r = x_ref[...] + jnp.dot(y.astype(BF16), wout_ref[...], preferred_element_type=F32)
    if final_norm:
        ms = jnp.mean(r * r, axis=-1, keepdims=True)
        r = r * lax.rsqrt(ms + EPS) * gf_ref[...]
    o_ref[...] = r


def _merge(ga, gb, gm, u, x, wpa, wpb, wpm, wout, g_final, *, tm, final_norm):
    t, d = x.shape
    const = lambda i: (0, 0)
    resident = lambda w: pl.BlockSpec(w.shape, const, pipeline_mode=pl.Buffered(1))
    rows = lambda width: pl.BlockSpec((tm, width), lambda i: (i, 0))
    w_bytes = (wpa.size + wpb.size + wpm.size + wout.size) * 2
    est = w_bytes + 2 * tm * (2 * MOBA_W + MEM_W + 3 * d) * 2 + 4 * tm * d * 4 + 6 * tm * d * 4
    return pl.pallas_call(
        functools.partial(_merge_kernel, final_norm=final_norm),
        grid=(t // tm,),
        in_specs=[
            rows(MOBA_W), rows(MLA_W), rows(MEM_W),
            pl.BlockSpec((tm, 3 * d), lambda i: (i, OFF_GL // (3 * d))),
            rows(d),
            resident(wpa), resident(wpb), resident(wpm), resident(wout),
            pl.BlockSpec((1, d), const),
        ],
        out_specs=rows(d),
        out_shape=jax.ShapeDtypeStruct((t, d), F32),
        compiler_params=pltpu.CompilerParams(
            dimension_semantics=("arbitrary",), vmem_limit_bytes=_vmem_limit(est)),
        name="merge",
    )(ga, gb, gm, u, x, wpa, wpb, wpm, wout, g_final.reshape(1, d))


def _w_in_tile_sources(n_cols):
    o_cq = 4 * MOBA_W
    o_zb = o_cq + MLA_Q_LORA + MLA_KV_LORA + MLA_ROPE
    o_qm = o_zb + MLA_W
    o_gl = o_qm + 2 * MEM_W
    srcs = ([o_gl + W_TILE * k for k in range(3 * D_MODEL // W_TILE)]
            + [W_TILE * k for k in range(4 * MOBA_W // W_TILE)] + [o_zb, o_cq, o_qm])
    assert len(srcs) * W_TILE == IN_WIDTH_P and o_gl + 3 * D_MODEL == n_cols
    assert all(src + W_TILE <= n_cols for src in srcs)
    return srcs


def _in_col_scale():
    cs = jnp.ones((1, IN_WIDTH_P), F32)
    cs = cs.at[:, OFF_QA:OFF_QA + MOBA_W].set(HEAD_DIM ** -0.5 * LOG2E)
    return cs.at[:, OFF_QM:OFF_QM + MEM_W].set(MEM_HEAD_DIM ** -0.5 * LOG2E)


def _regroup_w_uq(w):
    r = w.shape[0]
    w3 = w.reshape(r, MLA_HEADS, MLA_NOPE + MLA_ROPE)
    pad = jnp.zeros((r, MLA_HEADS, MLA_QK - MLA_NOPE - MLA_ROPE), w.dtype)
    return jnp.concatenate([w3, pad], axis=-1).reshape(r, MLA_HEADS * MLA_QK).astype(BF16)


def _regroup_w_ukv(w):
    r = w.shape[0]
    w3 = w.reshape(r, MLA_HEADS, MLA_NOPE + MLA_V)
    return jnp.concatenate([w3[:, :, :MLA_NOPE].reshape(r, MLA_W),
                            w3[:, :, MLA_NOPE:].reshape(r, MLA_W)], axis=1).astype(BF16)


def _rope_tables(seq):
    half = MLA_ROPE // 2
    inv = ROPE_THETA ** (-jnp.arange(half, dtype=F32) / half)
    ang = jnp.arange(seq, dtype=jnp.int32).astype(F32)[:, None] * inv[None, :]
    cos, sin = jnp.cos(ang), jnp.sin(ang)
    pad = LANES - MLA_ROPE
    cos_t = jnp.concatenate([cos, cos, jnp.ones((seq, pad), F32)], axis=1)
    sin_t = jnp.concatenate([-sin, sin, jnp.zeros((seq, pad), F32)], axis=1)
    return cos_t, sin_t


def kernel(x, mem, g_norm, w_in, g_cq, w_uq, g_ckv, w_ukv, g_mem, w_mem_kv, rel_bias,
           w_p_moba, w_p_mla, w_p_mem, w_out, g_final):
    b, s, d = x.shape
    m = mem.shape[1]
    depth = w_in.shape[0]
    t = b * s
    assert d == D_MODEL and s % TQ == 0 and m == TQ

    own, prev = _bias_tiles(rel_bias)
    cos_t, sin_t = _rope_tables(s)
    mem2 = mem.reshape(b * m, d)
    xs = x.reshape(t, d)
    in_scale = _in_col_scale()
    w_in_t = jnp.swapaxes(w_in, 1, 2)
    for l in range(depth):
        u = _in_proj(xs, g_norm[l], w_in_t, l, in_scale, tm=1024)
        u3 = u.reshape(b, s, IN_WIDTH_P)
        ga, (wout,) = _moba_attn(u3, rel_bias, own, prev, (w_out,), l)
        q2, k2, v2 = _mla_prep(u, g_cq[l], g_ckv[l], _regroup_w_uq(w_uq[l]), _regroup_w_ukv(w_ukv[l]),
                               cos_t, sin_t, seq=s, tm=512)
        gb = _mla_attn(q2.reshape(b, s, -1), k2.reshape(b, s, -1), v2.reshape(b, s, -1), u3)
        kvm = _norm_matmul(mem2, g_mem[l], w_mem_kv, l, tm=b * m, tn=MEM_W)
        gm, (wpa, wpb, wpm) = _mem_attn(u3, kvm.reshape(b, m, 2 * MEM_W), (w_p_moba, w_p_mla, w_p_mem), l)
        xs = _merge(ga.reshape(t, MOBA_W), gb.reshape(t, MLA_W), gm.reshape(t, MEM_W), u, xs,
                    wpa, wpb, wpm, wout, g_final, tm=512, final_norm=(l == depth - 1))
    return xs.reshape(b, s, d)
```

```python
import functools
import math

import jax
import jax.numpy as jnp
from jax import lax
from jax.experimental import pallas as pl
from jax.experimental.pallas import tpu as pltpu

D_MODEL = 2048
MOBA_HEADS = 8
HEAD_DIM = 128
MOBA_BLOCK = 256
MOBA_TOPK = 3
MLA_HEADS = 8
MLA_Q_LORA = 512
MLA_KV_LORA = 256
MLA_NOPE = 128
MLA_ROPE = 64
MLA_V = 128
ROPE_THETA = 10000.0
MEM_HEADS = 4
MEM_HEAD_DIM = 128
N_BUCKETS = 32
MAX_DISTANCE = 128
EPS = 1e-6

MOBA_W = MOBA_HEADS * HEAD_DIM
MLA_W = MLA_HEADS * MLA_V
MEM_W = MEM_HEADS * MEM_HEAD_DIM

LANES = 128
MXU_DIM = 256
V7X_VMEM_BYTES = 64 * 1024 * 1024
VMEM_HEADROOM_BYTES = 4 * 1024 * 1024
KERNEL_TEMP_BYTES = 8 * 1024 * 1024

BF16 = jnp.bfloat16
F32 = jnp.float32
LOG2E = 1.4426950408889634

KR_PAD = MXU_DIM
OFF_GL = 0
OFF_QA = OFF_GL + 3 * D_MODEL
OFF_KA = OFF_QA + MOBA_W
OFF_VA = OFF_KA + MOBA_W
OFF_ZA = OFF_VA + MOBA_W
OFF_ZB = OFF_ZA + MOBA_W
OFF_CQ = OFF_ZB + MLA_W
OFF_CKV = OFF_CQ + MLA_Q_LORA
OFF_KR = OFF_CKV + MLA_KV_LORA
OFF_QM = OFF_KR + KR_PAD
OFF_ZM = OFF_QM + MEM_W
IN_WIDTH_P = OFF_ZM + MEM_W

TQ = MOBA_BLOCK
BF16_TILE_ROWS = 16
ONES_ROWS = BF16_TILE_ROWS
VT_ROWS = MLA_V + ONES_ROWS
MLA_QK = MXU_DIM

W_TILE = 1024
W_SRC_UNIT = 64
NORM_CHUNK_ROWS = 64


def _vmem_limit(block_bytes):
    return int(min(block_bytes + KERNEL_TEMP_BYTES, V7X_VMEM_BYTES - VMEM_HEADROOM_BYTES))


def _t5_thresholds():
    max_exact = N_BUCKETS // 2

    def bucket(d):
        if d < max_exact:
            return d
        large = max_exact + int(math.log(d / max_exact) / math.log(MAX_DISTANCE / max_exact)
                                * (N_BUCKETS - max_exact))
        return min(large, N_BUCKETS - 1)

    thr, d = [], 0
    for b in range(1, N_BUCKETS):
        while bucket(d) < b:
            d += 1
        thr.append(d)
    return tuple(thr)


T5_THRESHOLDS = _t5_thresholds()
assert T5_THRESHOLDS[-1] <= MOBA_BLOCK + 1


def _norm_rows(x_ref, g_ref, h_ref, chunk):
    def body(r, carry):
        rows = pl.ds(pl.multiple_of(r * chunk, chunk), chunk)
        xv = x_ref[rows, :]
        ms = jnp.mean(xv * xv, axis=-1, keepdims=True)
        h_ref[rows, :] = (xv * lax.rsqrt(ms + EPS) * g_ref[...]).astype(BF16)
        return carry
    lax.fori_loop(0, x_ref.shape[0] // chunk, body, 0, unroll=4)


def _norm_matmul_kernel(x_ref, g_ref, w_ref, o_ref, h_ref, *, chunk):
    pl.when(pl.program_id(1) == 0)(functools.partial(_norm_rows, x_ref, g_ref, h_ref, chunk))
    o_ref[...] = jnp.dot(h_ref[...], w_ref[...].astype(BF16), preferred_element_type=F32).astype(o_ref.dtype)


def _in_proj_kernel(src_ref, x_ref, g_ref, wt_ref, cs_ref, o_ref, h_ref, *, chunk):
    del src_ref
    pl.when(pl.program_id(1) == 0)(functools.partial(_norm_rows, x_ref, g_ref, h_ref, chunk))
    acc = lax.dot_general(h_ref[...], wt_ref[...].astype(BF16), (((1,), (1,)), ((), ())),
                          preferred_element_type=F32)
    o_ref[...] = (acc * cs_ref[...]).astype(o_ref.dtype)


def _in_proj(x, g, wt_all, layer, col_scale, *, tm):
    t, d = x.shape
    srcs = _w_in_tile_sources(wt_all.shape[1])
    unit = W_SRC_UNIT
    assert all(src % unit == 0 for src in srcs)
    src_units = jnp.asarray([src // unit for src in srcs], jnp.int32)
    est = 2 * tm * d * 4 + tm * d * 2 + 2 * W_TILE * d * 4 + W_TILE * d * 2 + 2 * tm * W_TILE * 2 + tm * W_TILE * 4
    return pl.pallas_call(
        functools.partial(_in_proj_kernel, chunk=NORM_CHUNK_ROWS),
        grid_spec=pltpu.PrefetchScalarGridSpec(
            num_scalar_prefetch=1,
            grid=(t // tm, len(srcs)),
            in_specs=[
                pl.BlockSpec((tm, d), lambda i, j, src: (i, 0)),
                pl.BlockSpec((1, d), lambda i, j, src: (0, 0)),
                pl.BlockSpec((pl.Squeezed(), pl.Element(W_TILE), pl.Element(d)),
                             lambda i, j, src: (layer, src[j] * unit, 0)),
                pl.BlockSpec((1, W_TILE), lambda i, j, src: (0, j)),
            ],
            out_specs=pl.BlockSpec((tm, W_TILE), lambda i, j, src: (i, j)),
            scratch_shapes=[pltpu.VMEM((tm, d), BF16)],
        ),
        out_shape=jax.ShapeDtypeStruct((t, IN_WIDTH_P), BF16),
        compiler_params=pltpu.CompilerParams(
            dimension_semantics=("arbitrary", "arbitrary"), vmem_limit_bytes=_vmem_limit(est)),
        name="in_proj",
    )(src_units, x, g.reshape(1, d), wt_all, col_scale)


def _norm_matmul(x, g, w_all, layer, *, tm, tn):
    t, d = x.shape
    n = w_all.shape[2]
    est = 2 * tm * d * 4 + tm * d * 2 + 2 * d * tn * 4 + d * tn * 2 + 2 * tm * tn * 2 + tm * tn * 4
    return pl.pallas_call(
        functools.partial(_norm_matmul_kernel, chunk=NORM_CHUNK_ROWS),
        grid=(t // tm, n // tn),
        in_specs=[
            pl.BlockSpec((tm, d), lambda i, j: (i, 0)),
            pl.BlockSpec((1, d), lambda i, j: (0, 0)),
            pl.BlockSpec((pl.Squeezed(), d, tn), lambda i, j: (layer, 0, j)),
        ],
        out_specs=pl.BlockSpec((tm, tn), lambda i, j: (i, j)),
        out_shape=jax.ShapeDtypeStruct((t, n), BF16),
        scratch_shapes=[pltpu.VMEM((tm, d), BF16)],
        compiler_params=pltpu.CompilerParams(
            dimension_semantics=("arbitrary", "arbitrary"), vmem_limit_bytes=_vmem_limit(est)),
        name="norm_matmul",
    )(x, g.reshape(1, d), w_all)


def _bias_tiles_kernel(rb_ref, own_ref, prev_ref):
    key = lax.broadcasted_iota(jnp.int32, (TQ, TQ), 0)
    qry = lax.broadcasted_iota(jnp.int32, (TQ, TQ), 1)
    d_own = qry - key
    d_prev = d_own + MOBA_BLOCK

    def lookup(dist, h):
        val = jnp.zeros(dist.shape, F32) + rb_ref[0, h]
        for b in range(1, N_BUCKETS):
            val = jnp.where(dist >= T5_THRESHOLDS[b - 1], rb_ref[b, h], val)
        return val * LOG2E

    for h in range(own_ref.shape[0]):
        own_ref[h] = jnp.where(d_own >= 0, lookup(d_own, h), -jnp.inf)
        prev_ref[h] = lookup(d_prev, h)


def _bias_tiles(rel_bias):
    heads = rel_bias.shape[1]
    tile = jax.ShapeDtypeStruct((heads, TQ, TQ), F32)
    spec = pl.BlockSpec((heads, TQ, TQ), lambda i: (0, 0, 0))
    return pl.pallas_call(
        _bias_tiles_kernel,
        grid=(1,),
        in_specs=[pl.BlockSpec(memory_space=pltpu.SMEM)],
        out_specs=[spec, spec],
        out_shape=[tile, tile],
        name="bias_tiles",
    )(rel_bias)


def _build_vt_block(v_ref, vt_ref, j):
    dv = v_ref.shape[-1]
    blk = slice(j * TQ, (j + 1) * TQ)
    vt_ref[0:dv, blk] = v_ref[blk, :].astype(F32).T.astype(BF16)
    row = lax.broadcasted_iota(jnp.int32, (ONES_ROWS, TQ), 0)
    vt_ref[dv:dv + ONES_ROWS, blk] = jnp.where(row == 0, 1.0, 0.0).astype(BF16)


EXP_CHUNK_ROWS = 64


def _attention(n_tiles, q_tile_of, k_block_of, terms_of, prepare, vt_ref, bufs, emit):
    nt = (((1,), (1,)), ((), ()))
    dv = vt_ref.shape[0] - ONES_ROWS
    t_bufs, p_buf = bufs[:2], bufs[2]
    state = {}

    def stage1(i):
        prepare(i)
        q = q_tile_of(i)
        adds, consts, sels = terms_of(i, q)
        st = state[i] = dict(consts=consts, sels=sels, m=None)
        t_buf = t_bufs[i % 2]

        def item(j):
            blk = slice(j * TQ, (j + 1) * TQ)
            t = lax.dot_general(k_block_of(j), q, nt, preferred_element_type=F32)
            if adds[j] is not None:
                t = t + adds[j][...]
            t_buf[blk, :] = t
            mj = jnp.max(t, axis=0, keepdims=True) + consts[j]
            if sels[j] is not None:
                mj = jnp.where(sels[j], mj, -jnp.inf)
            st["m"] = mj if st["m"] is None else jnp.maximum(st["m"], mj)
        return [functools.partial(item, j) for j in range(len(consts))]

    def stage2(i):
        st = state.pop(i)
        t_buf = t_bufs[i % 2]

        def item(j):
            blk = slice(j * TQ, (j + 1) * TQ)
            off = st["m"] - st["consts"][j]
            if st["sels"][j] is not None:
                off = jnp.where(st["sels"][j], off, jnp.inf)
            def chunk(c, carry):
                rows = pl.ds(pl.multiple_of(j * TQ + c * EXP_CHUNK_ROWS, EXP_CHUNK_ROWS), EXP_CHUNK_ROWS)
                p_buf[rows, :] = jnp.exp2(t_buf[rows, :] - off).astype(BF16)
                return carry
            lax.fori_loop(0, TQ // EXP_CHUNK_ROWS, chunk, 0)

        def finish():
            n_keys = len(st["consts"]) * TQ
            acc = jnp.dot(vt_ref[:, 0:n_keys], p_buf[0:n_keys, :], preferred_element_type=F32)
            emit(i, (acc[0:dv, :] / acc[dv:dv + 1, :]).T)
        return [functools.partial(item, j) for j in range(len(st["consts"]))], finish

    for item in stage1(0):
        item()
        yield
    for i in range(n_tiles):
        ahead = stage1(i + 1) if i + 1 < n_tiles else []
        behind, finish = stage2(i)
        for k in range(max(len(ahead), len(behind))):
            if k < len(ahead):
                ahead[k]()
            if k < len(behind):
                behind[k]()
            yield
        finish()


def _run_streams(streams):
    active = list(streams)
    while active:
        for stream in list(active):
            if next(stream, StopIteration) is StopIteration:
                active.remove(stream)


def _cast_plan(weights, layer, n_b, n_h):
    n_steps = n_b * n_h
    ins, outs, shapes, nbytes = [], [], [], 0
    for w in weights:
        _, rows, cols = w.shape
        assert rows % (n_steps * BF16_TILE_ROWS) == 0
        blk_rows = rows // n_steps
        ins.append(pl.BlockSpec((pl.Squeezed(), blk_rows, cols), lambda bi, h: (layer, bi * n_h + h, 0)))
        outs.append(pl.BlockSpec((blk_rows, cols), lambda bi, h: (bi * n_h + h, 0)))
        shapes.append(jax.ShapeDtypeStruct((rows, cols), BF16))
        nbytes += 2 * blk_rows * cols * (4 + 2)
    return ins, outs, shapes, nbytes


def _cast_row_blocks(rest, n_cast):
    cast_in, (o_ref, *cast_out), scratch = rest[:n_cast], rest[n_cast:2 * n_cast + 1], rest[2 * n_cast + 1:]
    for w_ref, wb_ref in zip(cast_in, cast_out):
        wb_ref[...] = w_ref[...].astype(BF16)
    return o_ref, scratch


def _silu_gate(o, z):
    zf = z.astype(F32)
    return (o * (zf / (1.0 + jnp.exp(-zf)))).astype(BF16)


def _write_causal_tile(mask_ref):
    key = lax.broadcasted_iota(jnp.int32, (TQ, TQ), 0)
    qry = lax.broadcasted_iota(jnp.int32, (TQ, TQ), 1)
    mask_ref[...] = jnp.where(key <= qry, 0.0, -jnp.inf).astype(F32)


def _head_view(ref, head, width):
    return ref.at[0, :, head * width:(head + 1) * width]


def _moba_kernel(rb_ref, q_ref, k_ref, v_ref, z_ref, own_ref, prev_ref, *rest, heads, n_cast):
    o_ref, scratch = _cast_row_blocks(rest, n_cast)
    per_head = len(scratch) // heads
    _run_streams([
        _moba_head(rb_ref, pl.program_id(1) * heads + s, _head_view(q_ref, s, HEAD_DIM),
                   _head_view(k_ref, s, HEAD_DIM), _head_view(v_ref, s, HEAD_DIM),
                   _head_view(z_ref, s, HEAD_DIM), own_ref.at[s], prev_ref.at[s],
                   _head_view(o_ref, s, HEAD_DIM), *scratch[s * per_head:(s + 1) * per_head])
        for s in range(heads)])


def _moba_head(rb_ref, head, q_ref, k_ref, v_ref, z_ref, own_ref, prev_ref, o_ref, vt_ref, t0, t1, pb, km_ref):
    seq = q_ref.shape[0]
    n_tiles = seq // TQ
    far_const = rb_ref[N_BUCKETS - 1, head] * LOG2E
    km_ref[...] = jnp.zeros(km_ref.shape, F32)

    def prepare(i):
        _build_vt_block(v_ref, vt_ref, i)
        k_blk = k_ref[i * TQ:(i + 1) * TQ, :].astype(F32)
        km_ref[i:i + 1, :] = jnp.sum(k_blk, axis=0, keepdims=True) * (1.0 / MOBA_BLOCK)

    nt = (((1,), (1,)), ((), ()))

    def terms_of(i, q_tile):
        if i > MOBA_TOPK:
            k_mean = km_ref[...]
            km1 = k_mean.astype(BF16)
            rem = k_mean - km1.astype(F32)
            km2 = rem.astype(BF16)
            km3 = (rem - km2.astype(F32)).astype(BF16)
            gate = (lax.dot_general(km1, q_tile, nt, preferred_element_type=F32)
                    + lax.dot_general(km2, q_tile, nt, preferred_element_type=F32)
                    + lax.dot_general(km3, q_tile, nt, preferred_element_type=F32))
            sels = []
            for j in range(i):
                gj = gate[j:j + 1, :]
                cnt = jnp.zeros(gj.shape, F32)
                for jp in range(i):
                    if jp == j:
                        continue
                    gp = gate[jp:jp + 1, :]
                    beats = (gp >= gj) if jp < j else (gp > gj)
                    cnt = cnt + jnp.where(beats, 1.0, 0.0)
                sels.append(cnt < MOBA_TOPK)
        else:
            sels = [None] * i
        sels.append(None)
        adds = [None] * (i + 1)
        consts = [far_const] * (i + 1)
        adds[i], consts[i] = own_ref, 0.0
        if i >= 1:
            adds[i - 1], consts[i - 1] = prev_ref, 0.0
        return adds, consts, sels

    def emit(i, o):
        rows = slice(i * TQ, (i + 1) * TQ)
        o_ref[rows, :] = _silu_gate(o, z_ref[rows, :])

    return _attention(n_tiles, lambda i: q_ref[i * TQ:(i + 1) * TQ, :],
                      lambda j: k_ref[j * TQ:(j + 1) * TQ, :], terms_of, prepare, vt_ref, (t0, t1, pb), emit)


HEADS_PER_STEP = 4


def _attn_scratch(n_keys, extra=()):
    per_head = [pltpu.VMEM((VT_ROWS, n_keys), BF16), pltpu.VMEM((n_keys, TQ), F32),
                pltpu.VMEM((n_keys, TQ), F32), pltpu.VMEM((n_keys, TQ), BF16), *extra]
    return per_head * HEADS_PER_STEP


def _moba_attn(u3, rel_bias, own, prev, cast_weights, layer):
    b, s, _ = u3.shape
    assert s // MOBA_BLOCK <= BF16_TILE_ROWS
    hp = HEADS_PER_STEP
    width = hp * HEAD_DIM
    n_h = MOBA_HEADS // hp
    col = lambda off: (lambda bi, h: (bi, 0, off // width + h))
    blk = (1, s, width)
    tile_spec = pl.BlockSpec((hp, TQ, TQ), lambda bi, h: (h, 0, 0))
    cast_in, cast_out, cast_shapes, cast_bytes = _cast_plan(cast_weights, layer, b, n_h)
    est = (10 * s * width * 2 + 4 * hp * TQ * TQ * 4 + hp * (VT_ROWS * s * 2 + 2 * s * TQ * 4 + s * TQ * 2)
           + cast_bytes)
    outs = pl.pallas_call(
        functools.partial(_moba_kernel, heads=hp, n_cast=len(cast_weights)),
        grid=(b, n_h),
        in_specs=[
            pl.BlockSpec(memory_space=pltpu.SMEM),
            pl.BlockSpec(blk, col(OFF_QA)),
            pl.BlockSpec(blk, col(OFF_KA)),
            pl.BlockSpec(blk, col(OFF_VA)),
            pl.BlockSpec(blk, col(OFF_ZA)),
            tile_spec, tile_spec,
            *cast_in,
        ],
        out_specs=[pl.BlockSpec(blk, lambda bi, h: (bi, 0, h)), *cast_out],
        out_shape=[jax.ShapeDtypeStruct((b, s, MOBA_W), BF16), *cast_shapes],
        scratch_shapes=_attn_scratch(s, extra=(pltpu.VMEM((BF16_TILE_ROWS, HEAD_DIM), F32),)),
        compiler_params=pltpu.CompilerParams(
            dimension_semantics=("arbitrary", "arbitrary"), vmem_limit_bytes=_vmem_limit(est)),
        name="moba_attn",
    )(rel_bias, u3, u3, u3, u3, own, prev, *cast_weights)
    return outs[0], outs[1:]


def _mla_prep_kernel(cq_ref, ckv_ref, kr_ref, gq_ref, gkv_ref, wuq_ref, wukv_ref, cos_ref, sin_ref,
                     q_out, k_out, v_out, *, q_scale):
    def rms(x_ref, g_ref):
        xf = x_ref[...].astype(F32)
        ms = jnp.mean(xf * xf, axis=-1, keepdims=True)
        return (xf * lax.rsqrt(ms + EPS) * g_ref[...]).astype(BF16)

    cos = cos_ref[...]
    sin = sin_ref[...]
    half = MLA_ROPE // 2
    first_half = lax.broadcasted_iota(jnp.int32, cos.shape, 1) < half

    def rope(xr):
        partner = jnp.where(first_half, pltpu.roll(xr, LANES - half, 1), pltpu.roll(xr, half, 1))
        return xr * cos + partner * sin

    qb = jnp.dot(rms(cq_ref, gq_ref), wuq_ref[...], preferred_element_type=F32) * q_scale
    for h in range(MLA_HEADS):
        base = h * MLA_QK
        q_out[:, base:base + MLA_NOPE] = qb[:, base:base + MLA_NOPE].astype(BF16)
        q_out[:, base + MLA_NOPE:base + MLA_QK] = rope(qb[:, base + MLA_NOPE:base + MLA_QK]).astype(BF16)

    kvb = jnp.dot(rms(ckv_ref, gkv_ref), wukv_ref[...], preferred_element_type=F32)
    in_rope = lax.broadcasted_iota(jnp.int32, cos.shape, 1) < MLA_ROPE
    k_rope = rope(jnp.where(in_rope, kr_ref[:, 0:LANES].astype(F32), 0.0)).astype(BF16)
    for h in range(MLA_HEADS):
        base = h * MLA_QK
        k_out[:, base:base + MLA_NOPE] = kvb[:, h * MLA_NOPE:(h + 1) * MLA_NOPE].astype(BF16)
        k_out[:, base + MLA_NOPE:base + MLA_QK] = k_rope
    v_out[...] = kvb[:, MLA_W:2 * MLA_W].astype(BF16)


def _mla_prep(u, g_cq, g_ckv, wuq_p, wukv_p, cos_t, sin_t, *, seq, tm):
    t = u.shape[0]
    s_tiles = seq // tm
    const = lambda i: (0, 0)
    qk_shape = jax.ShapeDtypeStruct((t, MLA_HEADS * MLA_QK), BF16)
    est = (2 * tm * (MLA_Q_LORA + 2 * MLA_KV_LORA) * 2 + 2 * (wuq_p.size + wukv_p.size) * 2
           + 4 * tm * LANES * 4 + 2 * tm * 5 * MLA_W * 2 + 4 * tm * 2 * MLA_W * 4)
    return pl.pallas_call(
        functools.partial(_mla_prep_kernel, q_scale=(MLA_NOPE + MLA_ROPE) ** -0.5 * LOG2E),
        grid=(t // tm,),
        in_specs=[
            pl.BlockSpec((tm, MLA_Q_LORA), lambda i: (i, OFF_CQ // MLA_Q_LORA)),
            pl.BlockSpec((tm, MLA_KV_LORA), lambda i: (i, OFF_CKV // MLA_KV_LORA)),
            pl.BlockSpec((tm, KR_PAD), lambda i: (i, OFF_KR // KR_PAD)),
            pl.BlockSpec((1, MLA_Q_LORA), const),
            pl.BlockSpec((1, MLA_KV_LORA), const),
            pl.BlockSpec(wuq_p.shape, const),
            pl.BlockSpec(wukv_p.shape, const),
            pl.BlockSpec((tm, LANES), lambda i: (i % s_tiles, 0)),
            pl.BlockSpec((tm, LANES), lambda i: (i % s_tiles, 0)),
        ],
        out_specs=[
            pl.BlockSpec((tm, MLA_HEADS * MLA_QK), lambda i: (i, 0)),
            pl.BlockSpec((tm, MLA_HEADS * MLA_QK), lambda i: (i, 0)),
            pl.BlockSpec((tm, MLA_W), lambda i: (i, 0)),
        ],
        out_shape=[qk_shape, qk_shape, jax.ShapeDtypeStruct((t, MLA_W), BF16)],
        compiler_params=pltpu.CompilerParams(
            dimension_semantics=("arbitrary",), vmem_limit_bytes=_vmem_limit(est)),
        name="mla_prep",
    )(u, u, u, g_cq.reshape(1, -1), g_ckv.reshape(1, -1), wuq_p, wukv_p, cos_t, sin_t)


def _mla_attn_kernel(q_ref, k_ref, v_ref, z_ref, o_ref, mask_ref, *scratch, heads):
    _write_causal_tile(mask_ref)
    per_head = len(scratch) // heads
    _run_streams([
        _mla_head(_head_view(q_ref, s, MLA_QK), _head_view(k_ref, s, MLA_QK), _head_view(v_ref, s, MLA_V),
                  _head_view(z_ref, s, MLA_V), _head_view(o_ref, s, MLA_V), mask_ref,
                  *scratch[s * per_head:(s + 1) * per_head])
        for s in range(heads)])


def _mla_head(q_ref, k_ref, v_ref, z_ref, o_ref, mask_ref, vt_ref, t0, t1, pb):
    seq = q_ref.shape[0]

    def terms_of(i, q_tile):
        return [None] * i + [mask_ref], [0.0] * (i + 1), [None] * (i + 1)

    def emit(i, o):
        rows = slice(i * TQ, (i + 1) * TQ)
        o_ref[rows, :] = _silu_gate(o, z_ref[rows, :])

    return _attention(seq // TQ, lambda i: q_ref[i * TQ:(i + 1) * TQ, :],
                      lambda j: k_ref[j * TQ:(j + 1) * TQ, :], terms_of,
                      functools.partial(_build_vt_block, v_ref, vt_ref), vt_ref, (t0, t1, pb), emit)


def _mla_attn(q3, k3, v3, u3):
    b, s, _ = q3.shape
    hp = HEADS_PER_STEP
    est = (4 * s * hp * MLA_QK * 2 + 6 * s * hp * MLA_V * 2 + hp * (VT_ROWS * s * 2 + 2 * s * TQ * 4 + s * TQ * 2)
           + TQ * TQ * 4)
    return pl.pallas_call(
        functools.partial(_mla_attn_kernel, heads=hp),
        grid=(b, MLA_HEADS // hp),
        in_specs=[
            pl.BlockSpec((1, s, hp * MLA_QK), lambda bi, h: (bi, 0, h)),
            pl.BlockSpec((1, s, hp * MLA_QK), lambda bi, h: (bi, 0, h)),
            pl.BlockSpec((1, s, hp * MLA_V), lambda bi, h: (bi, 0, h)),
            pl.BlockSpec((1, s, hp * MLA_V), lambda bi, h: (bi, 0, OFF_ZB // (hp * MLA_V) + h)),
        ],
        out_specs=pl.BlockSpec((1, s, hp * MLA_V), lambda bi, h: (bi, 0, h)),
        out_shape=jax.ShapeDtypeStruct((b, s, MLA_W), BF16),
        scratch_shapes=[pltpu.VMEM((TQ, TQ), F32)] + _attn_scratch(s),
        compiler_params=pltpu.CompilerParams(
            dimension_semantics=("arbitrary", "arbitrary"), vmem_limit_bytes=_vmem_limit(est)),
        name="mla_attn",
    )(q3, k3, v3, u3)


def _mem_attn_kernel(q_ref, k_ref, v_ref, z_ref, *rest, heads, n_cast):
    o_ref, scratch = _cast_row_blocks(rest, n_cast)
    assert k_ref.shape[1] == TQ
    per_head = len(scratch) // heads
    d = MEM_HEAD_DIM
    _run_streams([
        _mem_head(_head_view(q_ref, s, d), _head_view(k_ref, s, d), _head_view(v_ref, s, d),
                  _head_view(z_ref, s, d), _head_view(o_ref, s, d), *scratch[s * per_head:(s + 1) * per_head])
        for s in range(heads)])


def _mem_head(q_ref, k_ref, v_ref, z_ref, o_ref, vt_ref, t0, t1, pb):
    seq = q_ref.shape[0]

    def prepare(i):
        if i == 0:
            _build_vt_block(v_ref, vt_ref, 0)

    def emit(i, o):
        rows = slice(i * TQ, (i + 1) * TQ)
        o_ref[rows, :] = _silu_gate(o, z_ref[rows, :])

    return _attention(seq // TQ, lambda i: q_ref[i * TQ:(i + 1) * TQ, :], lambda j: k_ref[...],
                      lambda i, q_tile: ([None], [0.0], [None]), prepare, vt_ref, (t0, t1, pb), emit)


def _mem_attn(u3, kv3, cast_weights, layer):
    b, s, _ = u3.shape
    m = kv3.shape[1]
    hp = HEADS_PER_STEP
    d = hp * MEM_HEAD_DIM
    n_h = MEM_HEADS // hp
    cast_in, cast_out, cast_shapes, cast_bytes = _cast_plan(cast_weights, layer, b, n_h)
    est = 6 * s * d * 2 + 4 * m * d * 2 + hp * (VT_ROWS * m * 2 + 2 * m * TQ * 4 + m * TQ * 2) + cast_bytes
    outs = pl.pallas_call(
        functools.partial(_mem_attn_kernel, heads=hp, n_cast=len(cast_weights)),
        grid=(b, n_h),
        in_specs=[
            pl.BlockSpec((1, s, d), lambda bi, h: (bi, 0, OFF_QM // d + h)),
            pl.BlockSpec((1, m, d), lambda bi, h: (bi, 0, h)),
            pl.BlockSpec((1, m, d), lambda bi, h: (bi, 0, MEM_W // d + h)),
            pl.BlockSpec((1, s, d), lambda bi, h: (bi, 0, OFF_ZM // d + h)),
            *cast_in,
        ],
        out_specs=[pl.BlockSpec((1, s, d), lambda bi, h: (bi, 0, h)), *cast_out],
        out_shape=[jax.ShapeDtypeStruct((b, s, MEM_W), BF16), *cast_shapes],
        scratch_shapes=_attn_scratch(m),
        compiler_params=pltpu.CompilerParams(
            dimension_semantics=("arbitrary", "arbitrary"), vmem_limit_bytes=_vmem_limit(est)),
        name="mem_attn",
    )(u3, kv3, kv3, u3, *cast_weights)
    return outs[0], outs[1:]


def _merge_kernel(ga_ref, gb_ref, gm_ref, gl_ref, x_ref, wpa_ref, wpb_ref, wpm_ref, wout_ref, gf_ref,
                  o_ref, *, final_norm):
    d = x_ref.shape[1]

    def gated(idx, g_ref, w_ref):
        logit = gl_ref[:, idx * d:(idx + 1) * d].astype(F32)
        return jnp.dot(g_ref[...], w_ref[...], preferred_element_type=F32) / (1.0 + jnp.exp(-logit))

    y = gated(0, ga_ref, wpa_ref) + gated(1, gb_ref, wpb_ref) + gated(2, gm_ref, wpm_ref)
    r = x_ref[...] + jnp.dot(y.astype(BF16), wout_ref[...], preferred_element_type=F32)
    if final_norm:
        ms = jnp.mean(r * r, axis=-1, keepdims=True)
        r = r * lax.rsqrt(ms + EPS) * gf_ref[...]
    o_ref[...] = r


def _merge(ga, gb, gm, u, x, wpa, wpb, wpm, wout, g_final, *, tm, final_norm):
    t, d = x.shape
    const = lambda i: (0, 0)
    resident = lambda w: pl.BlockSpec(w.shape, const, pipeline_mode=pl.Buffered(1))
    rows = lambda width: pl.BlockSpec((tm, width), lambda i: (i, 0))
    w_bytes = (wpa.size + wpb.size + wpm.size + wout.size) * 2
    est = w_bytes + 2 * tm * (2 * MOBA_W + MEM_W + 3 * d) * 2 + 4 * tm * d * 4 + 6 * tm * d * 4
    return pl.pallas_call(
        functools.partial(_merge_kernel, final_norm=final_norm),
        grid=(t // tm,),
        in_specs=[
            rows(MOBA_W), rows(MLA_W), rows(MEM_W),
            pl.BlockSpec((tm, 3 * d), lambda i: (i, OFF_GL // (3 * d))),
            rows(d),
            resident(wpa), resident(wpb), resident(wpm), resident(wout),
            pl.BlockSpec((1, d), const),
        ],
        out_specs=rows(d),
        out_shape=jax.ShapeDtypeStruct((t, d), F32),
        compiler_params=pltpu.CompilerParams(
            dimension_semantics=("arbitrary",), vmem_limit_bytes=_vmem_limit(est)),
        name="merge",
    )(ga, gb, gm, u, x, wpa, wpb, wpm, wout, g_final.reshape(1, d))


def _w_in_tile_sources(n_cols):
    o_cq = 4 * MOBA_W
    o_zb = o_cq + MLA_Q_LORA + MLA_KV_LORA + MLA_ROPE
    o_qm = o_zb + MLA_W
    o_gl = o_qm + 2 * MEM_W
    srcs = ([o_gl + W_TILE * k for k in range(3 * D_MODEL // W_TILE)]
            + [W_TILE * k for k in range(4 * MOBA_W // W_TILE)] + [o_zb, o_cq, o_qm])
    assert len(srcs) * W_TILE == IN_WIDTH_P and o_gl + 3 * D_MODEL == n_cols
    assert all(src + W_TILE <= n_cols for src in srcs)
    return srcs


def _in_col_scale():
    cs = jnp.ones((1, IN_WIDTH_P), F32)
    cs = cs.at[:, OFF_QA:OFF_QA + MOBA_W].set(HEAD_DIM ** -0.5 * LOG2E)
    return cs.at[:, OFF_QM:OFF_QM + MEM_W].set(MEM_HEAD_DIM ** -0.5 * LOG2E)


def _regroup_w_uq(w):
    r = w.shape[0]
    w3 = w.reshape(r, MLA_HEADS, MLA_NOPE + MLA_ROPE)
    pad = jnp.zeros((r, MLA_HEADS, MLA_QK - MLA_NOPE - MLA_ROPE), w.dtype)
    return jnp.concatenate([w3, pad], axis=-1).reshape(r, MLA_HEADS * MLA_QK).astype(BF16)


def _regroup_w_ukv(w):
    r = w.shape[0]
    w3 = w.reshape(r, MLA_HEADS, MLA_NOPE + MLA_V)
    return jnp.concatenate([w3[:, :, :MLA_NOPE].reshape(r, MLA_W),
                            w3[:, :, MLA_NOPE:].reshape(r, MLA_W)], axis=1).astype(BF16)


def _rope_tables(seq):
    half = MLA_ROPE // 2
    inv = ROPE_THETA ** (-jnp.arange(half, dtype=F32) / half)
    ang = jnp.arange(seq, dtype=jnp.int32).astype(F32)[:, None] * inv[None, :]
    cos, sin = jnp.cos(ang), jnp.sin(ang)
    pad = LANES - MLA_ROPE
    cos_t = jnp.concatenate([cos, cos, jnp.ones((seq, pad), F32)], axis=1)
    sin_t = jnp.concatenate([-sin, sin, jnp.zeros((seq, pad), F32)], axis=1)
    return cos_t, sin_t


def kernel(x, mem, g_norm, w_in, g_cq, w_uq, g_ckv, w_ukv, g_mem, w_mem_kv, rel_bias,
           w_p_moba, w_p_mla, w_p_mem, w_out, g_final):
    b, s, d = x.shape
    m = mem.shape[1]
    depth = w_in.shape[0]
    t = b * s
    assert d == D_MODEL and s % TQ == 0 and m == TQ

    own, prev = _bias_tiles(rel_bias)
    cos_t, sin_t = _rope_tables(s)
    mem2 = mem.reshape(b * m, d)
    xs = x.reshape(t, d)
    in_scale = _in_col_scale()
    w_in_t = jnp.swapaxes(w_in, 1, 2)
    for l in range(depth):
        u = _in_proj(xs, g_norm[l], w_in_t, l, in_scale, tm=1024)
        u3 = u.reshape(b, s, IN_WIDTH_P)
        ga, (wout,) = _moba_attn(u3, rel_bias, own, prev, (w_out,), l)
        q2, k2, v2 = _mla_prep(u, g_cq[l], g_ckv[l], _regroup_w_uq(w_uq[l]), _regroup_w_ukv(w_ukv[l]),
                               cos_t, sin_t, seq=s, tm=512)
        gb = _mla_attn(q2.reshape(b, s, -1), k2.reshape(b, s, -1), v2.reshape(b, s, -1), u3)
        kvm = _norm_matmul(mem2, g_mem[l], w_mem_kv, l, tm=b * m, tn=MEM_W)
        gm, (wpa, wpb, wpm) = _mem_attn(u3, kvm.reshape(b, m, 2 * MEM_W), (w_p_moba, w_p_mla, w_p_mem), l)
        xs = _merge(ga.reshape(t, MOBA_W), gb.reshape(t, MLA_W), gm.reshape(t, MEM_W), u, xs,
                    wpa, wpb, wpm, wout, g_final, tm=256, final_norm=(l == depth - 1))
    return xs.reshape(b, s, d)
```
